```python
import jax
import jax.numpy as jnp
from jax import lax
import numpy as np

D_MODEL = 1024
BATCH = 32
SEQ = 256
DEPTH = 2
DEC_BATCH = 4
DEC_SEQ = 4096
PAST_LEN = 512

GRID_W = 64
N_EVEN = (DEPTH + 1) // 2
N_ODD = DEPTH // 2
EPS = 1e-6
RET_HEADS = 4
RET_DK = 128
RET_DV = 128
RET_CHUNK = 128
ROPE_BASE = 10000.0
GMLP_GROUPS = 4
GMLP_CH = 128
GMLP_CHUNK = 128
AB_SPLITS = (RET_HEADS * RET_DK, RET_HEADS * RET_DK, RET_HEADS * RET_DV, RET_HEADS * RET_DV, GMLP_GROUPS * GMLP_CH, GMLP_GROUPS * GMLP_CH)
AB_IN = sum(AB_SPLITS)
AB_OUT = RET_HEADS * RET_DV + GMLP_GROUPS * GMLP_CH
D_RNN = D_MODEL
LRU_BLOCKS = 8
LRU_BS = D_RNN // LRU_BLOCKS
LRU_C = 8.0
CONV_W = 4
CONV_PAD_L = 2
CONV_PAD_R = 1
D_FF = 4 * D_MODEL

kernel_name = 'hybrid_retention_gmlp_rglru_dit_step'


def rmsnorm(x, g):
    xf = x.astype(jnp.float32)
    y = xf * lax.rsqrt(jnp.mean(xf * xf, axis=-1, keepdims=True) + EPS)
    return (y * g.astype(jnp.float32)).astype(x.dtype)


def to_heads(t, n_heads):
    b, l, _ = t.shape
    return t.reshape(b, l, n_heads, -1).transpose(0, 2, 1, 3)


def axial_rope(t):
    length = t.shape[2]
    pos = jnp.arange(length)
    row = (pos // GRID_W).astype(jnp.float32)
    col = (pos % GRID_W).astype(jnp.float32)
    half = t.shape[-1] // 2
    nf = half // 2
    inv_freq = ROPE_BASE ** (-jnp.arange(nf, dtype=jnp.float32) / nf)

    def rot(xa, p):
        ang = p[:, None] * inv_freq[None, :]
        cos, sin = jnp.cos(ang), jnp.sin(ang)
        x1, x2 = xa[..., :nf], xa[..., nf:]
        return jnp.concatenate([x1 * cos - x2 * sin, x1 * sin + x2 * cos], axis=-1)

    tf = t.astype(jnp.float32)
    return jnp.concatenate([rot(tf[..., :half], row), rot(tf[..., half:], col)], axis=-1)


def retention_dir(q, k, v, log_gamma, s0, exclusive):
    b, h, l, dk = q.shape
    dv = v.shape[-1]
    c = RET_CHUNK
    nc = l // c
    qc = q.reshape(b, h, nc, c, dk)
    kc = k.reshape(b, h, nc, c, dk)
    vc = v.reshape(b, h, nc, c, dv)
    i = jnp.arange(c, dtype=jnp.float32)
    off = 1.0 if exclusive else 0.0
    expo = i[:, None] - i[None, :] - off
    lg = log_gamma[:, None, None]
    intra_decay = jnp.where(expo >= 0.0, jnp.exp(lg * jnp.maximum(expo, 0.0)), 0.0)
    scores = jnp.einsum('bhncd,bhnjd->bhncj', qc, kc) * intra_decay[None, :, None]
    o = jnp.einsum('bhncj,bhnje->bhnce', scores, vc)
    k_decay = jnp.exp(log_gamma[:, None] * (c - 1.0 - i)[None, :])
    chunk_kv = jnp.einsum('bhnjd,bhnje->nbhde', kc * k_decay[None, :, None, :, None], vc)
    g_chunk = jnp.exp(log_gamma * c)[None, :, None, None]

    def step(s, kv):
        return g_chunk * s + kv, s

    s_final, s_prev = lax.scan(step, s0, chunk_kv)
    q_decay = jnp.exp(log_gamma[:, None] * (i + 1.0 - off)[None, :])
    o = o + jnp.einsum('bhncd,nbhde->bhnce', qc * q_decay[None, :, None, :, None], s_prev)
    return o.reshape(b, h, l, dv), s_final


def retention_bidir(q, k, v, gate, lg_f, lg_b, s0_f, s0_b):
    o_f, s_f = retention_dir(q, k, v, lg_f, s0_f, False)
    o_b, s_b = retention_dir(jnp.flip(q, 2), jnp.flip(k, 2), jnp.flip(v, 2), lg_b, s0_b, True)
    o = o_f + jnp.flip(o_b, 2)
    mu = jnp.mean(o, axis=-1, keepdims=True)
    var = jnp.mean(jnp.square(o - mu), axis=-1, keepdims=True)
    o = (o - mu) * lax.rsqrt(var + EPS)
    b, h, l, dv = o.shape
    o = o.transpose(0, 2, 1, 3).reshape(b, l, h * dv)
    return jax.nn.silu(gate.astype(jnp.float32)) * o, s_f, s_b


def chunk_gmlp(u, v, ws, bs):
    b, l, _ = u.shape
    nc = l // GMLP_CHUNK
    u = jax.nn.gelu(u.astype(jnp.float32))
    vf = jax.nn.gelu(v.astype(jnp.float32)).reshape(b, nc, GMLP_CHUNK, GMLP_GROUPS, GMLP_CH)
    vf = vf * lax.rsqrt(jnp.mean(vf * vf, axis=-1, keepdims=True) + EPS)
    s = jnp.einsum('gpq,bnqgc->bnpgc', ws.astype(jnp.float32), vf) + bs.astype(jnp.float32).T[None, None, :, :, None]
    return u * s.reshape(b, l, GMLP_GROUPS * GMLP_CH)


def centred_dwconv(x, w, bias):
    l = x.shape[1]
    xp = jnp.pad(x, ((0, 0), (CONV_PAD_L, CONV_PAD_R), (0, 0)))
    out = bias
    for j in range(CONV_W):
        out = out + xp[:, j:j + l, :] * w[j]
    return out


def block_diag_linear(x, w, bias):
    b, l, _ = x.shape
    y = jnp.einsum('blnc,ncd->blnd', x.reshape(b, l, LRU_BLOCKS, LRU_BS), w)
    return y.reshape(b, l, D_RNN) + bias


def linear_scan(a, bx, h0):
    def combine(e1, e2):
        a1, b1 = e1
        a2, b2 = e2
        return a1 * a2, a2 * b1 + b2

    a_cum, b_cum = lax.associative_scan(combine, (a, bx), axis=1)
    h = a_cum * h0[:, None, :] + b_cum
    return h, h[:, -1, :]


def rg_lru_dir(xc, wa, ba, wx, bx, lam, h0):
    r = jax.nn.sigmoid(block_diag_linear(xc, wa, ba))
    ig = jax.nn.sigmoid(block_diag_linear(xc, wx, bx))
    log_a = -LRU_C * r * jax.nn.softplus(-lam)
    a = jnp.exp(log_a)
    mult = jnp.sqrt(-jnp.expm1(2.0 * log_a))
    return linear_scan(a, mult * (ig * xc), h0)


def setup_inputs(seed: int = 0) -> dict:
    key = jax.random.key(seed)
    keys = list(jax.random.split(key, 40))

    def nrm(shape, scale):
        return scale * jax.random.normal(keys.pop(), shape, jnp.float32)

    f32 = jnp.float32
    ret_base = jnp.asarray(np.log(2.0 ** (5 + np.arange(RET_HEADS)) - 1.0), f32)
    a0 = jax.random.uniform(keys.pop(), (N_ODD, 2, D_RNN), f32, 0.9, 0.999)
    s0 = a0 ** (1.0 / LRU_C)
    return {
        'x_prompt': nrm((BATCH, SEQ, D_MODEL), 1.0),
        'x_sample': nrm((DEC_BATCH, DEC_SEQ, D_MODEL), 1.0),
        'c': nrm((DEC_BATCH, D_MODEL), 1.0),
        'c_ctx': nrm((D_MODEL,), 1.0),
        'state_ret_fwd': nrm((DEC_BATCH, N_EVEN, RET_HEADS, RET_DK, RET_DV), 0.3),
        'state_ret_bwd': nrm((DEC_BATCH, N_EVEN, RET_HEADS, RET_DK, RET_DV), 0.3),
        'state_lru_fwd': nrm((DEC_BATCH, N_ODD, D_RNN), 0.5),
        'state_lru_bwd': nrm((DEC_BATCH, N_ODD, D_RNN), 0.5),
        'mod_w': nrm((DEPTH, D_MODEL, 6 * D_MODEL), D_MODEL ** -0.5),
        'mod_b': nrm((DEPTH, 6 * D_MODEL), 0.01),
        'norm_mix': 1.0 + nrm((DEPTH, D_MODEL), 0.02),
        'norm_mlp': 1.0 + nrm((DEPTH, D_MODEL), 0.02),
        'mlp_w1': nrm((DEPTH, D_MODEL, D_FF), D_MODEL ** -0.5),
        'mlp_w2': nrm((DEPTH, D_FF, D_MODEL), D_FF ** -0.5),
        'ab_w_in': nrm((N_EVEN, D_MODEL, AB_IN), D_MODEL ** -0.5),
        'ab_w_out': nrm((N_EVEN, AB_OUT, D_MODEL), AB_OUT ** -0.5),
        'ret_decay_fwd': ret_base[None, :] + nrm((N_EVEN, RET_HEADS), 0.05),
        'ret_decay_bwd': ret_base[None, :] + nrm((N_EVEN, RET_HEADS), 0.05),
        'gmlp_ws': nrm((N_EVEN, GMLP_GROUPS, GMLP_CHUNK, GMLP_CHUNK), GMLP_CHUNK ** -0.5),
        'gmlp_bs': 1.0 + nrm((N_EVEN, GMLP_GROUPS, GMLP_CHUNK), 0.1),
        'lru_w_in': nrm((N_ODD, D_MODEL, 2 * D_RNN), D_MODEL ** -0.5),
        'lru_conv_w': nrm((N_ODD, CONV_W, D_RNN), CONV_W ** -0.5),
        'lru_conv_b': nrm((N_ODD, D_RNN), 0.01),
        'lru_gate_a_w': nrm((N_ODD, 2, LRU_BLOCKS, LRU_BS, LRU_BS), LRU_BS ** -0.5),
        'lru_gate_a_b': nrm((N_ODD, 2, D_RNN), 0.01),
        'lru_gate_x_w': nrm((N_ODD, 2, LRU_BLOCKS, LRU_BS, LRU_BS), LRU_BS ** -0.5),
        'lru_gate_x_b': nrm((N_ODD, 2, D_RNN), 0.01),
        'lru_lambda': jnp.log(s0) - jnp.log1p(-s0),
        'lru_w_out': nrm((N_ODD, D_RNN, D_MODEL), D_RNN ** -0.5),
        'final_norm': 1.0 + nrm((D_MODEL,), 0.02),
    }


def reference(x_prompt, x_sample, c, c_ctx, state_ret_fwd, state_ret_bwd, state_lru_fwd, state_lru_bwd,
              mod_w, mod_b, norm_mix, norm_mlp, mlp_w1, mlp_w2,
              ab_w_in, ab_w_out, ret_decay_fwd, ret_decay_bwd, gmlp_ws, gmlp_bs,
              lru_w_in, lru_conv_w, lru_conv_b, lru_gate_a_w, lru_gate_a_b, lru_gate_x_w, lru_gate_x_b,
              lru_lambda, lru_w_out, final_norm):
    f32 = jnp.float32
    ab_cuts = np.cumsum(AB_SPLITS)[:-1].tolist()

    def trunk(x, cond, ret_f0, ret_b0, lru_f0, lru_b0, latent):
        ret_f, ret_b, lru_f, lru_b = [], [], [], []
        for l in range(DEPTH):
            mod = (jax.nn.silu(cond.astype(f32)) @ mod_w[l].astype(f32) + mod_b[l].astype(f32))[:, None, :]
            sh1, sc1, g1, sh2, sc2, g2 = jnp.split(mod, 6, axis=-1)
            h = rmsnorm(x, norm_mix[l]).astype(f32) * (1.0 + sc1) + sh1
            if l % 2 == 0:
                e = l // 2
                q, k, v, g, u, vv = jnp.split(h @ ab_w_in[e].astype(f32), ab_cuts, axis=-1)
                q = to_heads(q, RET_HEADS)
                k = to_heads(k, RET_HEADS) * (RET_DK ** -0.5)
                v = to_heads(v, RET_HEADS)
                if latent:
                    q, k = axial_rope(q), axial_rope(k)
                o_ret, s_f, s_b = retention_bidir(
                    q, k, v, g,
                    jax.nn.log_sigmoid(ret_decay_fwd[e].astype(f32)),
                    jax.nn.log_sigmoid(ret_decay_bwd[e].astype(f32)),
                    ret_f0[:, e].astype(f32), ret_b0[:, e].astype(f32))
                o_gm = chunk_gmlp(u, vv, gmlp_ws[e], gmlp_bs[e])
                mix = jnp.concatenate([o_ret, o_gm], axis=-1) @ ab_w_out[e].astype(f32)
                ret_f.append(s_f)
                ret_b.append(s_b)
            else:
                o = l // 2
                gate_in, x_in = jnp.split(h @ lru_w_in[o].astype(f32), 2, axis=-1)
                xc = centred_dwconv(x_in, lru_conv_w[o].astype(f32), lru_conv_b[o].astype(f32))
                h_f, s_f = rg_lru_dir(xc, lru_gate_a_w[o, 0].astype(f32), lru_gate_a_b[o, 0].astype(f32),
                                      lru_gate_x_w[o, 0].astype(f32), lru_gate_x_b[o, 0].astype(f32),
                                      lru_lambda[o, 0].astype(f32), lru_f0[:, o].astype(f32))
                h_b, s_b = rg_lru_dir(jnp.flip(xc, 1), lru_gate_a_w[o, 1].astype(f32), lru_gate_a_b[o, 1].astype(f32),
                                      lru_gate_x_w[o, 1].astype(f32), lru_gate_x_b[o, 1].astype(f32),
                                      lru_lambda[o, 1].astype(f32), lru_b0[:, o].astype(f32))
                mix = (jax.nn.gelu(gate_in) * (h_f + jnp.flip(h_b, 1))) @ lru_w_out[o].astype(f32)
                lru_f.append(s_f)
                lru_b.append(s_b)
            x = x + (g1 * mix).astype(x.dtype)
            h = rmsnorm(x, norm_mlp[l]).astype(f32) * (1.0 + sc2) + sh2
            ff = jnp.square(jax.nn.relu(h @ mlp_w1[l].astype(f32))) @ mlp_w2[l].astype(f32)
            x = x + (g2 * ff).astype(x.dtype)
        y = rmsnorm(x, final_norm)
        return y, jnp.stack(ret_f, axis=1), jnp.stack(ret_b, axis=1), jnp.stack(lru_f, axis=1), jnp.stack(lru_b, axis=1)

    bp = x_prompt.shape[0]
    zero_ret = jnp.zeros((bp, N_EVEN, RET_HEADS, RET_DK, RET_DV), f32)
    zero_lru = jnp.zeros((bp, N_ODD, D_RNN), f32)
    y_prompt, new_ret_fwd, new_ret_bwd, new_lru_fwd, new_lru_bwd = trunk(
        x_prompt, c_ctx[None, :], zero_ret, zero_ret, zero_lru, zero_lru, False)

    y_sample = trunk(x_sample, c, state_ret_fwd, state_ret_bwd, state_lru_fwd, state_lru_bwd, True)[0]

    return (y_prompt, y_sample, new_ret_fwd, new_ret_bwd, new_lru_fwd, new_lru_bwd)
```

```python
import functools

import jax
import jax.numpy as jnp
from jax import lax
from jax.experimental import pallas as pl
from jax.experimental.pallas import tpu as pltpu

f32 = jnp.float32
bf16 = jnp.bfloat16

D_MODEL = 1024
DEPTH = 2
GRID_W = 64
EPS = 1e-6
RET_HEADS = 4
RET_DK = 128
RET_DV = 128
RET_CHUNK = 128
ROPE_BASE = 10000.0
GMLP_GROUPS = 4
GMLP_CH = 128
AB_IN = 3072
AB_OUT = 1024
D_RNN = D_MODEL
LRU_BLOCKS = 8
LRU_BS = 128
LRU_C = 8.0
D_FF = 4 * D_MODEL

LANES = 128
SUBLANES = 8
MIB = 1024 * 1024

TOK_TILE = 512
COND_ROWS = 8
MOD_TN = 1536
LRU_SEG = 256
LRU_NSEG = 16
LRU_PITCH = LRU_SEG + SUBLANES
FF_CHUNK = 1024


def _cparams(sem, vmem_mib):
    return pltpu.CompilerParams(dimension_semantics=sem, vmem_limit_bytes=vmem_mib * MIB)


def _rms(x, gain):
    ms = jnp.mean(x * x, axis=-1, keepdims=True)
    return x * lax.rsqrt(ms + EPS) * gain


def _mod_kernel(cond_ref, w_ref, b_ref, o_ref):
    c = cond_ref[...]
    s = c * jax.nn.sigmoid(c)
    o_ref[...] = jnp.dot(s.astype(bf16), w_ref[...].astype(bf16), preferred_element_type=f32) + b_ref[...]


def _mod_call(conds, mod_w, mod_b):
    depth, d, n = mod_w.shape
    return pl.pallas_call(
        _mod_kernel,
        grid=(depth, n // MOD_TN),
        in_specs=[
            pl.BlockSpec((COND_ROWS, d), lambda l, j: (0, 0)),
            pl.BlockSpec((None, d, MOD_TN), lambda l, j: (l, 0, j)),
            pl.BlockSpec((None, 1, MOD_TN), lambda l, j: (l, 0, j)),
        ],
        out_specs=pl.BlockSpec((None, COND_ROWS, MOD_TN), lambda l, j: (l, 0, j)),
        out_shape=jax.ShapeDtypeStruct((depth, COND_ROWS, n), f32),
        compiler_params=_cparams(("parallel", "parallel"), 32),
        name="mod",
    )(conds, mod_w, mod_b.reshape(depth, 1, n))


def _rope(t, cos, sin_signed, first_half):
    up = pltpu.roll(t, LANES - 32, axis=1)
    dn = pltpu.roll(t, 32, axis=1)
    return t * cos + jnp.where(first_half, up, dn) * sin_signed


def _inproj_kernel(*refs, retention, rope):
    if rope:
        x_ref, sh_ref, sc_ref, gain_ref, w_ref, cos_ref, sin_ref, o_ref = refs
    else:
        x_ref, sh_ref, sc_ref, gain_ref, w_ref, o_ref = refs
    h = _rms(x_ref[...], gain_ref[...]) * (1.0 + sc_ref[...]) + sh_ref[...]
    hb = h.astype(bf16)
    n = w_ref.shape[1]
    if not retention:
        for j in range(0, n, 512):
            o_ref[:, j:j + 512] = jnp.dot(hb, w_ref[:, j:j + 512], preferred_element_type=f32).astype(bf16)
        return
    qk = RET_HEADS * RET_DK
    if rope:
        cos = cos_ref[...]
        sin = sin_ref[...]
        lane = lax.broadcasted_iota(jnp.int32, cos.shape, 1)
        first_half = (lane % 64) < 32
    for part in range(2):
        y = jnp.dot(hb, w_ref[:, part * qk:(part + 1) * qk], preferred_element_type=f32)
        for hh in range(RET_HEADS):
            t = y[:, hh * RET_DK:(hh + 1) * RET_DK]
            if part == 1:
                t = t * (RET_DK ** -0.5)
            if rope:
                t = _rope(t, cos, sin, first_half)
            c0 = part * qk + hh * RET_DK
            o_ref[:, c0:c0 + RET_DK] = t.astype(bf16)
    for j in range(2 * qk, n, 512):
        o_ref[:, j:j + 512] = jnp.dot(hb, w_ref[:, j:j + 512], preferred_element_type=f32).astype(bf16)


def _inproj_call(x3, modl, gain, w, rope_tabs, *, retention, sh_col, sc_col):
    bg, lg, d = x3.shape
    n = w.shape[1]
    t = TOK_TILE
    rope = rope_tabs is not None
    in_specs = [
        pl.BlockSpec((None, t, d), lambda b, i: (b, i, 0)),
        pl.BlockSpec((None, 1, d), lambda b, i: (b, 0, sh_col)),
        pl.BlockSpec((None, 1, d), lambda b, i: (b, 0, sc_col)),
        pl.BlockSpec((1, d), lambda b, i: (0, 0)),
        pl.BlockSpec((d, n), lambda b, i: (0, 0)),
    ]
    args = [x3, modl, modl, gain.reshape(1, d), w]
    if rope:
        in_specs += [pl.BlockSpec((t, LANES), lambda b, i: (i, 0))] * 2
        args += list(rope_tabs)
    return pl.pallas_call(
        functools.partial(_inproj_kernel, retention=retention, rope=rope),
        grid=(bg, lg // t),
        in_specs=in_specs,
        out_specs=pl.BlockSpec((None, t, n), lambda b, i: (b, i, 0)),
        out_shape=jax.ShapeDtypeStruct((bg, lg, n), bf16),
        compiler_params=_cparams(("parallel", "parallel"), 48),
        name="inproj_ret" if retention else "inproj_lru",
    )(*args)


def _row_iota():
    return lax.broadcasted_iota(jnp.int32, (RET_CHUNK, LANES), 0).astype(f32)


def _retstate_kernel(dec_ref, kf_ref, vf_ref, kb_ref, vb_ref, s0f_ref, s0b_ref,
                     sfp_ref, sbn_ref, sff_ref, sbf_ref, sf_scr, sb_scr, *, tc):
    i = pl.program_id(1)

    @pl.when(i == 0)
    def _():
        sf_scr[...] = s0f_ref[...]
        sb_scr[...] = s0b_ref[...]

    row = _row_iota()
    c = float(RET_CHUNK)
    for hh in range(RET_HEADS):
        lg_f = jax.nn.log_sigmoid(dec_ref[0, hh])
        lg_b = jax.nn.log_sigmoid(dec_ref[1, hh])
        kd_f = jnp.exp(lg_f * (c - 1.0 - row))
        kd_b = jnp.exp(lg_b * row)
        gc_f = jnp.exp(lg_f * c)
        gc_b = jnp.exp(lg_b * c)
        cs = slice(hh * RET_DK, (hh + 1) * RET_DK)
        for cc in range(tc):
            rs = slice(cc * RET_CHUNK, (cc + 1) * RET_CHUNK)
            s_prev = sf_scr[hh]
            sfp_ref[cc, hh] = s_prev.astype(bf16)
            kd = (kf_ref[rs, cs].astype(f32) * kd_f).astype(bf16)
            kv = lax.dot_general(kd, vf_ref[rs, cs], (((0,), (0,)), ((), ())), preferred_element_type=f32)
            sf_scr[hh] = gc_f * s_prev + kv
        for cc in reversed(range(tc)):
            rs = slice(cc * RET_CHUNK, (cc + 1) * RET_CHUNK)
            s_next = sb_scr[hh]
            sbn_ref[cc, hh] = s_next.astype(bf16)
            kd = (kb_ref[rs, cs].astype(f32) * kd_b).astype(bf16)
            kv = lax.dot_general(kd, vb_ref[rs, cs], (((0,), (0,)), ((), ())), preferred_element_type=f32)
            sb_scr[hh] = gc_b * s_next + kv

    @pl.when(i == pl.num_programs(1) - 1)
    def _():
        sff_ref[...] = sf_scr[...]
        sbf_ref[...] = sb_scr[...]


def _retstate_call(qkv, dec, s0f, s0b, tc):
    bs, ls, _ = qkv.shape
    nc = ls // RET_CHUNK
    t = tc * RET_CHUNK
    ns = nc // tc
    hw = RET_HEADS * RET_DK
    st = (RET_HEADS, RET_DK, RET_DV)
    return pl.pallas_call(
        functools.partial(_retstate_kernel, tc=tc),
        grid=(bs, ns),
        in_specs=[
            pl.BlockSpec((2, RET_HEADS, 1, LANES), lambda b, i: (0, 0, 0, 0)),
            pl.BlockSpec((None, t, hw), lambda b, i: (b, i, 1)),
            pl.BlockSpec((None, t, hw), lambda b, i: (b, i, 2)),
            pl.BlockSpec((None, t, hw), lambda b, i: (b, ns - 1 - i, 1)),
            pl.BlockSpec((None, t, hw), lambda b, i: (b, ns - 1 - i, 2)),
            pl.BlockSpec((None,) + st, lambda b, i: (b, 0, 0, 0)),
            pl.BlockSpec((None,) + st, lambda b, i: (b, 0, 0, 0)),
        ],
        out_specs=[
            pl.BlockSpec((None, tc) + st, lambda b, i: (b, i, 0, 0, 0)),
            pl.BlockSpec((None, tc) + st, lambda b, i: (b, ns - 1 - i, 0, 0, 0)),
            pl.BlockSpec((None,) + st, lambda b, i: (b, 0, 0, 0)),
            pl.BlockSpec((None,) + st, lambda b, i: (b, 0, 0, 0)),
        ],
        out_shape=[
            jax.ShapeDtypeStruct((bs, nc) + st, bf16),
            jax.ShapeDtypeStruct((bs, nc) + st, bf16),
            jax.ShapeDtypeStruct((bs,) + st, f32),
            jax.ShapeDtypeStruct((bs,) + st, f32),
        ],
        scratch_shapes=[pltpu.VMEM(st, f32), pltpu.VMEM(st, f32)],
        compiler_params=_cparams(("parallel", "arbitrary"), 32),
        name="retstate",
    )(dec, qkv, qkv, qkv, qkv, s0f, s0b)


def _mix0_kernel(dec_ref, q_ref, k_ref, v_ref, g_ref, u_ref, vv_ref, sfp_ref, sbn_ref, ws_ref, bs_ref,
                 o_ref, *, tc):
    row = _row_iota()
    col = lax.broadcasted_iota(jnp.int32, (RET_CHUNK, LANES), 1).astype(f32)
    c = float(RET_CHUNK)
    nt = (((1,), (1,)), ((), ()))
    for hh in range(RET_HEADS):
        lg_f = jax.nn.log_sigmoid(dec_ref[0, hh])
        lg_b = jax.nn.log_sigmoid(dec_ref[1, hh])
        decay = jnp.where(row >= col,
                          jnp.exp(lg_f * jnp.maximum(row - col, 0.0)),
                          jnp.exp(lg_b * jnp.maximum(col - row - 1.0, 0.0)))
        qd_f = jnp.exp(lg_f * (row + 1.0))
        qd_b = jnp.exp(lg_b * (c - 1.0 - row))
        cs = slice(hh * RET_DK, (hh + 1) * RET_DK)
        for cc in range(tc):
            rs = slice(cc * RET_CHUNK, (cc + 1) * RET_CHUNK)
            q = q_ref[rs, cs]
            qf = q.astype(f32)
            s = lax.dot_general(q, k_ref[rs, cs], nt, preferred_element_type=f32)
            o = jnp.dot((s * decay).astype(bf16), v_ref[rs, cs], preferred_element_type=f32)
            o = o + jnp.dot((qf * qd_f).astype(bf16), sfp_ref[cc, hh], preferred_element_type=f32)
            o = o + jnp.dot((qf * qd_b).astype(bf16), sbn_ref[cc, hh], preferred_element_type=f32)
            mu = jnp.mean(o, axis=-1, keepdims=True)
            var = jnp.mean(jnp.square(o - mu), axis=-1, keepdims=True)
            on = (o - mu) * lax.rsqrt(var + EPS)
            gate = g_ref[rs, cs].astype(f32)
            o_ref[rs, cs] = (gate * jax.nn.sigmoid(gate) * on).astype(bf16)
    off = RET_HEADS * RET_DV
    for gg in range(GMLP_GROUPS):
        cs = slice(gg * GMLP_CH, (gg + 1) * GMLP_CH)
        w = ws_ref[gg]
        bias = bs_ref[gg]
        for cc in range(tc):
            rs = slice(cc * RET_CHUNK, (cc + 1) * RET_CHUNK)
            gu = jax.nn.gelu(u_ref[rs, cs].astype(f32))
            gv = jax.nn.gelu(vv_ref[rs, cs].astype(f32))
            gv = gv * lax.rsqrt(jnp.mean(gv * gv, axis=-1, keepdims=True) + EPS)
            sp = jnp.dot(w, gv.astype(bf16), preferred_element_type=f32) + bias
            o_ref[rs, off + gg * GMLP_CH: off + (gg + 1) * GMLP_CH] = (gu * sp).astype(bf16)


def _mix0_call(qkv, dec, sfp, sbn, ws, bs, tc):
    bsz, ls, _ = qkv.shape
    t = tc * RET_CHUNK
    hw = RET_HEADS * RET_DK
    st = (RET_HEADS, RET_DK, RET_DV)
    col_spec = lambda j: pl.BlockSpec((None, t, hw), lambda b, i, j=j: (b, i, j))
    return pl.pallas_call(
        functools.partial(_mix0_kernel, tc=tc),
        grid=(bsz, ls // t),
        in_specs=[
            pl.BlockSpec((2, RET_HEADS, 1, LANES), lambda b, i: (0, 0, 0, 0)),
            col_spec(0), col_spec(1), col_spec(2), col_spec(3), col_spec(4), col_spec(5),
            pl.BlockSpec((None, tc) + st, lambda b, i: (b, i, 0, 0, 0)),
            pl.BlockSpec((None, tc) + st, lambda b, i: (b, i, 0, 0, 0)),
            pl.BlockSpec((GMLP_GROUPS, RET_CHUNK, RET_CHUNK), lambda b, i: (0, 0, 0)),
            pl.BlockSpec((GMLP_GROUPS, RET_CHUNK, 1), lambda b, i: (0, 0, 0)),
        ],
        out_specs=pl.BlockSpec((None, t, AB_OUT), lambda b, i: (b, i, 0)),
        out_shape=jax.ShapeDtypeStruct((bsz, ls, AB_OUT), bf16),
        compiler_params=_cparams(("parallel", "parallel"), 32),
        name="mix0",
    )(dec, qkv, qkv, qkv, qkv, qkv, qkv, sfp, sbn, ws, bs)


def _lru_kernel(gate_ref, xin_ref, cw_ref, cb_ref, wg_ref, bg_ref, lam_ref, h0f_ref, h0b_ref,
                o_ref, hfl_ref, hbf_ref, xs, af, bfs, ab, bbs, *, chained):
    seg, nseg, pitch = LRU_SEG, LRU_NSEG, LRU_PITCH
    rows = seg * nseg
    pad = SUBLANES

    xs[0:pad, :] = jnp.zeros((pad, LANES), f32)
    xs[pad + rows:pad + rows + pad, :] = jnp.zeros((pad, LANES), f32)

    def stage(s, carry):
        r0 = pl.multiple_of(s * seg, seg)
        xs[pl.ds(pad + r0, seg), :] = xin_ref[pl.ds(r0, seg), :].astype(f32)
        return carry

    lax.fori_loop(0, nseg, stage, 0)

    cw = cw_ref[...]
    cb = cb_ref[...]
    bg = bg_ref[...]
    sp = jax.nn.softplus(-lam_ref[...])
    local = lax.broadcasted_iota(jnp.int32, (seg, LANES), 0)

    def gates(s, carry):
        r0 = pl.multiple_of(s * seg, seg)
        win = xs[pl.ds(r0, seg + 2 * pad), :]
        x0 = win[pad:pad + seg]
        xm2 = pltpu.roll(win, 2, axis=0)[pad:pad + seg]
        xm1 = pltpu.roll(win, 1, axis=0)[pad:pad + seg]
        xp1 = pltpu.roll(win, seg + 2 * pad - 1, axis=0)[pad:pad + seg]
        if not chained:
            xm2 = jnp.where(local >= 2, xm2, 0.0)
            xm1 = jnp.where(local >= 1, xm1, 0.0)
            xp1 = jnp.where(local < seg - 1, xp1, 0.0)
        xc = cb + xm2 * cw[0:1] + xm1 * cw[1:2] + x0 * cw[2:3] + xp1 * cw[3:4]
        gt = jnp.dot(xc.astype(bf16), wg_ref[...], preferred_element_type=f32) + bg
        p0 = pl.multiple_of(s * pitch, SUBLANES)
        for d, (a_s, b_s) in enumerate(((af, bfs), (ab, bbs))):
            r = jax.nn.sigmoid(gt[:, (2 * d) * LANES:(2 * d + 1) * LANES])
            ig = jax.nn.sigmoid(gt[:, (2 * d + 1) * LANES:(2 * d + 2) * LANES])
            la = (-LRU_C) * r * sp[d:d + 1]
            a = jnp.exp(la)
            mult = jnp.sqrt(-jnp.tanh(la) * (1.0 + a * a))
            a_s[pl.ds(p0, seg), :] = a
            b_s[pl.ds(p0, seg), :] = mult * (ig * xc)
        return carry

    lax.fori_loop(0, nseg, gates, 0)

    def scan(j, carry):
        hf, pf, hb, pb = carry
        a = af[pl.ds(j, nseg, stride=pitch), :]
        b = bfs[pl.ds(j, nseg, stride=pitch), :]
        hf = a * hf + b
        pf = a * pf
        bfs[pl.ds(j, nseg, stride=pitch), :] = hf
        af[pl.ds(j, nseg, stride=pitch), :] = pf
        jj = seg - 1 - j
        a2 = ab[pl.ds(jj, nseg, stride=pitch), :]
        b2 = bbs[pl.ds(jj, nseg, stride=pitch), :]
        hb = a2 * hb + b2
        pb = a2 * pb
        bbs[pl.ds(jj, nseg, stride=pitch), :] = hb
        ab[pl.ds(jj, nseg, stride=pitch), :] = pb
        return hf, pf, hb, pb

    zero = jnp.zeros((nseg, LANES), f32)
    one = jnp.ones((nseg, LANES), f32)
    lax.fori_loop(0, seg, scan, (zero, one, zero, one), unroll=4)

    def fix_f(s, carry):
        p0 = pl.multiple_of(s * pitch, SUBLANES)
        cin = carry if chained else h0f_ref[pl.ds(s, 1), :]
        h = bfs[pl.ds(p0, seg), :] + af[pl.ds(p0, seg), :] * cin
        bfs[pl.ds(p0, seg), :] = h
        last = h[seg - 1:seg, :]
        if not chained:
            hfl_ref[pl.ds(s, 1), :] = last
        return last

    last_f = lax.fori_loop(0, nseg, fix_f, h0f_ref[0:1, :])
    if chained:
        hfl_ref[...] = last_f

    def fix_b(k, carry):
        s = nseg - 1 - k
        p0 = pl.multiple_of(s * pitch, SUBLANES)
        r0 = pl.multiple_of(s * seg, seg)
        cin = carry if chained else h0b_ref[pl.ds(s, 1), :]
        hb = bbs[pl.ds(p0, seg), :] + ab[pl.ds(p0, seg), :] * cin
        first = hb[0:1, :]
        if not chained:
            hbf_ref[pl.ds(s, 1), :] = first
        g = jax.nn.gelu(gate_ref[pl.ds(r0, seg), :].astype(f32))
        o_ref[pl.ds(r0, seg), :] = (g * (bfs[pl.ds(p0, seg), :] + hb)).astype(bf16)
        return first

    first_b = lax.fori_loop(0, nseg, fix_b, h0b_ref[0:1, :])
    if chained:
        hbf_ref[...] = first_b


def _lru_call(gx, cw, cb, wg, bg, lam, h0f, h0b, chained):
    ntok = gx.shape[0]
    rows = LRU_SEG * LRU_NSEG
    nt = ntok // rows
    gc = h0f.shape[1]
    nb = LRU_BLOCKS
    scr = pltpu.VMEM((LRU_NSEG * LRU_PITCH, LANES), f32)
    return pl.pallas_call(
        functools.partial(_lru_kernel, chained=chained),
        grid=(nt, nb),
        in_specs=[
            pl.BlockSpec((rows, LANES), lambda t, n: (t, n)),
            pl.BlockSpec((rows, LANES), lambda t, n: (t, nb + n)),
            pl.BlockSpec((4, LANES), lambda t, n: (0, n)),
            pl.BlockSpec((1, LANES), lambda t, n: (0, n)),
            pl.BlockSpec((None, LANES, 4 * LANES), lambda t, n: (n, 0, 0)),
            pl.BlockSpec((None, 1, 4 * LANES), lambda t, n: (n, 0, 0)),
            pl.BlockSpec((2, LANES), lambda t, n: (0, n)),
            pl.BlockSpec((None, gc, LANES), lambda t, n: (t, 0, n)),
            pl.BlockSpec((None, gc, LANES), lambda t, n: (t, 0, n)),
        ],
        out_specs=[
            pl.BlockSpec((rows, LANES), lambda t, n: (t, n)),
            pl.BlockSpec((None, gc, LANES), lambda t, n: (t, 0, n)),
            pl.BlockSpec((None, gc, LANES), lambda t, n: (t, 0, n)),
        ],
        out_shape=[
            jax.ShapeDtypeStruct((ntok, D_RNN), bf16),
            jax.ShapeDtypeStruct((nt, gc, D_RNN), f32),
            jax.ShapeDtypeStruct((nt, gc, D_RNN), f32),
        ],
        scratch_shapes=[pltpu.VMEM((rows + 2 * SUBLANES, LANES), f32), scr, scr, scr, scr],
        compiler_params=_cparams(("parallel", "parallel"), 40),
        name="lru",
    )(gx, gx, cw, cb, wg, bg, lam, h0f, h0b)


def _outmlp_kernel(x_ref, mix_ref, g1_ref, sh2_ref, sc2_ref, g2_ref, gain_ref, fgain_ref,
                   wo_ref, w1_ref, w2_ref, o_ref, *, final):
    mix = jnp.dot(mix_ref[...], wo_ref[...], preferred_element_type=f32)
    x1 = x_ref[...] + g1_ref[...] * mix
    h = (_rms(x1, gain_ref[...]) * (1.0 + sc2_ref[...]) + sh2_ref[...]).astype(bf16)
    acc = None
    for j in range(0, D_FF, FF_CHUNK):
        hid = jnp.dot(h, w1_ref[:, j:j + FF_CHUNK], preferred_element_type=f32)
        hid = jnp.square(jnp.maximum(hid, 0.0)).astype(bf16)
        part = jnp.dot(hid, w2_ref[j:j + FF_CHUNK, :], preferred_element_type=f32)
        acc = part if acc is None else acc + part
    x2 = x1 + g2_ref[...] * acc
    if final:
        x2 = _rms(x2, fgain_ref[...])
    o_ref[...] = x2


def _outmlp_call(x3, mix3, modl, gain, fgain, wo, w1, w2, *, final):
    bg, lg, d = x3.shape
    t = TOK_TILE
    dm = mix3.shape[-1]
    mod_spec = lambda k: pl.BlockSpec((None, 1, d), lambda b, i, k=k: (b, 0, k))
    resident = lambda shape: pl.BlockSpec(shape, lambda b, i: (0, 0), pipeline_mode=pl.Buffered(1))
    return pl.pallas_call(
        functools.partial(_outmlp_kernel, final=final),
        grid=(bg, lg // t),
        in_specs=[
            pl.BlockSpec((None, t, d), lambda b, i: (b, i, 0)),
            pl.BlockSpec((None, t, dm), lambda b, i: (b, i, 0)),
            mod_spec(2), mod_spec(3), mod_spec(4), mod_spec(5),
            pl.BlockSpec((1, d), lambda b, i: (0, 0)),
            pl.BlockSpec((1, d), lambda b, i: (0, 0)),
            resident((dm, d)), resident((d, D_FF)), resident((D_FF, d)),
        ],
        out_specs=pl.BlockSpec((None, t, d), lambda b, i: (b, i, 0)),
        out_shape=jax.ShapeDtypeStruct((bg, lg, d), f32),
        compiler_params=_cparams(("parallel", "parallel"), 56),
        name="outmlp_final" if final else "outmlp",
    )(x3, mix3, modl, modl, modl, modl, gain.reshape(1, d), fgain.reshape(1, d), wo, w1, w2)


def _rope_tables(length):
    pos = jnp.arange(length)
    rowp = (pos // GRID_W).astype(f32)
    colp = (pos % GRID_W).astype(f32)
    nf = RET_DK // 4
    inv_freq = ROPE_BASE ** (-jnp.arange(nf, dtype=f32) / nf)
    ar = rowp[:, None] * inv_freq[None, :]
    ac = colp[:, None] * inv_freq[None, :]
    cos = jnp.concatenate([jnp.cos(ar), jnp.cos(ar), jnp.cos(ac), jnp.cos(ac)], axis=-1)
    sin = jnp.concatenate([-jnp.sin(ar), jnp.sin(ar), -jnp.sin(ac), jnp.sin(ac)], axis=-1)
    return cos, sin


def kernel(x_prompt, x_sample, c, c_ctx, state_ret_fwd, state_ret_bwd, state_lru_fwd, state_lru_bwd, mod_w, mod_b, norm_mix, norm_mlp, mlp_w1, mlp_w2, ab_w_in, ab_w_out, ret_decay_fwd, ret_decay_bwd, gmlp_ws, gmlp_bs, lru_w_in, lru_conv_w, lru_conv_b, lru_gate_a_w, lru_gate_a_b, lru_gate_x_w, lru_gate_x_b, lru_lambda, lru_w_out, final_norm):
    d = D_MODEL
    n_lat = c.shape[0]
    conds = jnp.concatenate([c_ctx[None, :], c, jnp.zeros((COND_ROWS - 1 - n_lat, d), f32)], axis=0)
    mod = _mod_call(conds, mod_w, mod_b)

    w_in0 = ab_w_in[0].astype(bf16)
    w_out0 = ab_w_out[0].astype(bf16)
    w1 = mlp_w1.astype(bf16)
    w2 = mlp_w2.astype(bf16)
    w_in1 = lru_w_in[0].astype(bf16)
    w_out1 = lru_w_out[0].astype(bf16)
    ws = gmlp_ws[0].astype(bf16)
    bs = gmlp_bs[0][:, :, None]
    dec = jnp.broadcast_to(jnp.stack([ret_decay_fwd[0], ret_decay_bwd[0]])[:, :, None, None],
                           (2, RET_HEADS, 1, LANES))
    wg = jnp.concatenate([lru_gate_a_w[0, 0], lru_gate_x_w[0, 0], lru_gate_a_w[0, 1], lru_gate_x_w[0, 1]],
                         axis=-1).astype(bf16)
    bgate = jnp.stack([lru_gate_a_b[0, 0], lru_gate_x_b[0, 0], lru_gate_a_b[0, 1], lru_gate_x_b[0, 1]], axis=0)
    bgate = bgate.reshape(4, LRU_BLOCKS, LRU_BS).transpose(1, 0, 2).reshape(LRU_BLOCKS, 1, 4 * LRU_BS)

    def trunk(x, mod_rows, ret_f0, ret_b0, lru_f0, lru_b0, latent):
        bsz, ls, _ = x.shape
        x3 = x if latent else x.reshape(1, bsz * ls, d)
        modl = [mod[l, mod_rows][:, None, :] for l in range(DEPTH)]
        tc = min(4, ls // RET_CHUNK)

        qkv = _inproj_call(x3, modl[0], norm_mix[0], w_in0, _rope_tables(ls) if latent else None,
                           retention=True, sh_col=0, sc_col=1)
        qkv = qkv.reshape(bsz, ls, AB_IN)
        sfp, sbn, sff, sbf = _retstate_call(qkv, dec, ret_f0, ret_b0, tc)
        mix = _mix0_call(qkv, dec, sfp, sbn, ws, bs, tc)
        x3 = _outmlp_call(x3, mix.reshape(x3.shape[0], x3.shape[1], AB_OUT), modl[0], norm_mlp[0], final_norm,
                          w_out0, w1[0], w2[0], final=False)

        gx = _inproj_call(x3, modl[1], norm_mix[1], w_in1, None, retention=False, sh_col=0, sc_col=1)
        rows = LRU_SEG * LRU_NSEG
        nt = bsz * ls // rows
        gc = 1 if latent else LRU_NSEG
        pre, hfl, hbf = _lru_call(gx.reshape(bsz * ls, 2 * D_RNN), lru_conv_w[0], lru_conv_b[0].reshape(1, D_RNN),
                                  wg, bgate, lru_lambda[0], lru_f0.reshape(nt, gc, D_RNN),
                                  lru_b0.reshape(nt, gc, D_RNN), chained=latent)
        y = _outmlp_call(x3, pre.reshape(x3.shape[0], x3.shape[1], D_RNN), modl[1], norm_mlp[1], final_norm,
                         w_out1, w1[1], w2[1], final=True)
        return (y.reshape(bsz, ls, d), sff[:, None], sbf[:, None],
                hfl.reshape(bsz, 1, D_RNN), hbf.reshape(bsz, 1, D_RNN))

    bp = x_prompt.shape[0]
    zero_ret = jnp.zeros((bp, RET_HEADS, RET_DK, RET_DV), f32)
    zero_lru = jnp.zeros((bp, D_RNN), f32)
    y_prompt, new_ret_fwd, new_ret_bwd, new_lru_fwd, new_lru_bwd = trunk(
        x_prompt, slice(0, 1), zero_ret, zero_ret, zero_lru, zero_lru, False)
    y_sample = trunk(x_sample, slice(1, 1 + n_lat), state_ret_fwd[:, 0], state_ret_bwd[:, 0],
                     state_lru_fwd[:, 0], state_lru_bwd[:, 0], True)[0]
    return (y_prompt, y_sample, new_ret_fwd, new_ret_bwd, new_lru_fwd, new_lru_bwd)
```

```python
import functools

import jax
import jax.numpy as jnp
import numpy as np
from jax import lax
from jax.experimental import pallas as pl
from jax.experimental.pallas import tpu as pltpu

f32 = jnp.float32
bf16 = jnp.bfloat16

D_MODEL = 1024
DEPTH = 2
GRID_W = 64
EPS = 1e-6
RET_HEADS = 4
RET_DK = 128
RET_DV = 128
RET_CHUNK = 128
ROPE_BASE = 10000.0
GMLP_GROUPS = 4
GMLP_CH = 128
AB_IN = 3072
AB_OUT = 1024
D_RNN = D_MODEL
LRU_BLOCKS = 8
LRU_BS = 128
LRU_C = 8.0
D_FF = 4 * D_MODEL

LANES = 128
SUBLANES = 8
MIB = 1024 * 1024

TOK_TILE = 512
COND_ROWS = 8
MOD_TN = 1536
LRU_SEG = 256
LRU_NSEG = 16
LRU_PITCH = LRU_SEG + SUBLANES
FF_CHUNK = 1024


def _cparams(sem, vmem_mib):
    return pltpu.CompilerParams(dimension_semantics=sem, vmem_limit_bytes=vmem_mib * MIB)


def _rms(x, gain):
    ms = jnp.mean(x * x, axis=-1, keepdims=True)
    return x * lax.rsqrt(ms + EPS) * gain


def _mod_kernel(cond_ref, w_ref, b_ref, o_ref):
    c = cond_ref[...]
    s = c * jax.nn.sigmoid(c)
    o_ref[...] = jnp.dot(s.astype(bf16), w_ref[...].astype(bf16), preferred_element_type=f32) + b_ref[...]


def _mod_call(conds, mod_w, mod_b):
    depth, d, n = mod_w.shape
    return pl.pallas_call(
        _mod_kernel,
        grid=(depth, n // MOD_TN),
        in_specs=[
            pl.BlockSpec((COND_ROWS, d), lambda l, j: (0, 0)),
            pl.BlockSpec((None, d, MOD_TN), lambda l, j: (l, 0, j)),
            pl.BlockSpec((None, 1, MOD_TN), lambda l, j: (l, 0, j)),
        ],
        out_specs=pl.BlockSpec((None, COND_ROWS, MOD_TN), lambda l, j: (l, 0, j)),
        out_shape=jax.ShapeDtypeStruct((depth, COND_ROWS, n), f32),
        compiler_params=_cparams(("parallel", "parallel"), 32),
        name="mod",
    )(conds, mod_w, mod_b.reshape(depth, 1, n))


def _rope(t, cos, sin_signed, first_half):
    up = pltpu.roll(t, LANES - 32, axis=1)
    dn = pltpu.roll(t, 32, axis=1)
    return t * cos + jnp.where(first_half, up, dn) * sin_signed


def _inproj_kernel(*refs, retention, rope):
    if rope:
        x_ref, sh_ref, sc_ref, gain_ref, w_ref, cos_ref, sin_ref, o_ref = refs
    else:
        x_ref, sh_ref, sc_ref, gain_ref, w_ref, o_ref = refs
    h = _rms(x_ref[...], gain_ref[...]) * (1.0 + sc_ref[...]) + sh_ref[...]
    hb = h.astype(bf16)
    n = w_ref.shape[1]
    if not retention:
        for j in range(0, n, 512):
            o_ref[:, j:j + 512] = jnp.dot(hb, w_ref[:, j:j + 512], preferred_element_type=f32).astype(bf16)
        return
    qk = RET_HEADS * RET_DK
    if rope:
        cos = cos_ref[...]
        sin = sin_ref[...]
        lane = lax.broadcasted_iota(jnp.int32, cos.shape, 1)
        first_half = (lane % 64) < 32
    for part in range(2):
        y = jnp.dot(hb, w_ref[:, part * qk:(part + 1) * qk], preferred_element_type=f32)
        for hh in range(RET_HEADS):
            t = y[:, hh * RET_DK:(hh + 1) * RET_DK]
            if part == 1:
                t = t * (RET_DK ** -0.5)
            if rope:
                t = _rope(t, cos, sin, first_half)
            c0 = part * qk + hh * RET_DK
            o_ref[:, c0:c0 + RET_DK] = t.astype(bf16)
    for j in range(2 * qk, n, 512):
        o_ref[:, j:j + 512] = jnp.dot(hb, w_ref[:, j:j + 512], preferred_element_type=f32).astype(bf16)


def _inproj_call(x3, modl, gain, w, rope_tabs, *, retention, sh_col, sc_col):
    bg, lg, d = x3.shape
    n = w.shape[1]
    t = TOK_TILE
    rope = rope_tabs is not None
    in_specs = [
        pl.BlockSpec((None, t, d), lambda b, i: (b, i, 0)),
        pl.BlockSpec((None, 1, d), lambda b, i: (b, 0, sh_col)),
        pl.BlockSpec((None, 1, d), lambda b, i: (b, 0, sc_col)),
        pl.BlockSpec((1, d), lambda b, i: (0, 0)),
        pl.BlockSpec((d, n), lambda b, i: (0, 0)),
    ]
    args = [x3, modl, modl, gain.reshape(1, d), w]
    if rope:
        in_specs += [pl.BlockSpec((t, LANES), lambda b, i: (i, 0))] * 2
        args += list(rope_tabs)
    return pl.pallas_call(
        functools.partial(_inproj_kernel, retention=retention, rope=rope),
        grid=(bg, lg // t),
        in_specs=in_specs,
        out_specs=pl.BlockSpec((None, t, n), lambda b, i: (b, i, 0)),
        out_shape=jax.ShapeDtypeStruct((bg, lg, n), bf16),
        compiler_params=_cparams(("parallel", "parallel"), 48),
        name="inproj_ret" if retention else "inproj_lru",
    )(*args)


def _row_iota():
    return lax.broadcasted_iota(jnp.int32, (RET_CHUNK, LANES), 0).astype(f32)


def _retstate_kernel(*refs, tc, zero_init):
    if zero_init:
        dec_ref, kf_ref, vf_ref, kb_ref, vb_ref, sfp_ref, sbn_ref, sff_ref, sbf_ref, sf_scr, sb_scr = refs
    else:
        (dec_ref, kf_ref, vf_ref, kb_ref, vb_ref, s0f_ref, s0b_ref,
         sfp_ref, sbn_ref, sff_ref, sbf_ref, sf_scr, sb_scr) = refs
    i = pl.program_id(1)

    @pl.when(i == 0)
    def _():
        if zero_init:
            sf_scr[...] = jnp.zeros(sf_scr.shape, f32)
            sb_scr[...] = jnp.zeros(sb_scr.shape, f32)
        else:
            sf_scr[...] = s0f_ref[...]
            sb_scr[...] = s0b_ref[...]

    row = _row_iota()
    c = float(RET_CHUNK)
    for hh in range(RET_HEADS):
        lg_f = jax.nn.log_sigmoid(dec_ref[0, hh])
        lg_b = jax.nn.log_sigmoid(dec_ref[1, hh])
        kd_f = jnp.exp(lg_f * (c - 1.0 - row))
        kd_b = jnp.exp(lg_b * row)
        gc_f = jnp.exp(lg_f * c)
        gc_b = jnp.exp(lg_b * c)
        cs = slice(hh * RET_DK, (hh + 1) * RET_DK)
        for cc in range(tc):
            rs = slice(cc * RET_CHUNK, (cc + 1) * RET_CHUNK)
            s_prev = sf_scr[hh]
            sfp_ref[cc, hh] = s_prev.astype(bf16)
            kd = (kf_ref[rs, cs].astype(f32) * kd_f).astype(bf16)
            kv = lax.dot_general(kd, vf_ref[rs, cs], (((0,), (0,)), ((), ())), preferred_element_type=f32)
            sf_scr[hh] = gc_f * s_prev + kv
        for cc in reversed(range(tc)):
            rs = slice(cc * RET_CHUNK, (cc + 1) * RET_CHUNK)
            s_next = sb_scr[hh]
            sbn_ref[cc, hh] = s_next.astype(bf16)
            kd = (kb_ref[rs, cs].astype(f32) * kd_b).astype(bf16)
            kv = lax.dot_general(kd, vb_ref[rs, cs], (((0,), (0,)), ((), ())), preferred_element_type=f32)
            sb_scr[hh] = gc_b * s_next + kv

    @pl.when(i == pl.num_programs(1) - 1)
    def _():
        sff_ref[...] = sf_scr[...]
        sbf_ref[...] = sb_scr[...]


def _retstate_call(qkv, dec, s0f, s0b, tc):
    bs, ls, _ = qkv.shape
    nc = ls // RET_CHUNK
    t = tc * RET_CHUNK
    ns = nc // tc
    hw = RET_HEADS * RET_DK
    st = (RET_HEADS, RET_DK, RET_DV)
    zero_init = s0f is None
    in_specs = [
        pl.BlockSpec((2, RET_HEADS, 1, LANES), lambda b, i: (0, 0, 0, 0)),
        pl.BlockSpec((None, t, hw), lambda b, i: (b, i, 1)),
        pl.BlockSpec((None, t, hw), lambda b, i: (b, i, 2)),
        pl.BlockSpec((None, t, hw), lambda b, i: (b, ns - 1 - i, 1)),
        pl.BlockSpec((None, t, hw), lambda b, i: (b, ns - 1 - i, 2)),
    ]
    args = [dec, qkv, qkv, qkv, qkv]
    if not zero_init:
        in_specs += [pl.BlockSpec((None,) + st, lambda b, i: (b, 0, 0, 0))] * 2
        args += [s0f, s0b]
    return pl.pallas_call(
        functools.partial(_retstate_kernel, tc=tc, zero_init=zero_init),
        grid=(bs, ns),
        in_specs=in_specs,
        out_specs=[
            pl.BlockSpec((None, tc) + st, lambda b, i: (b, i, 0, 0, 0)),
            pl.BlockSpec((None, tc) + st, lambda b, i: (b, ns - 1 - i, 0, 0, 0)),
            pl.BlockSpec((None,) + st, lambda b, i: (b, 0, 0, 0)),
            pl.BlockSpec((None,) + st, lambda b, i: (b, 0, 0, 0)),
        ],
        out_shape=[
            jax.ShapeDtypeStruct((bs, nc) + st, bf16),
            jax.ShapeDtypeStruct((bs, nc) + st, bf16),
            jax.ShapeDtypeStruct((bs,) + st, f32),
            jax.ShapeDtypeStruct((bs,) + st, f32),
        ],
        scratch_shapes=[pltpu.VMEM(st, f32), pltpu.VMEM(st, f32)],
        compiler_params=_cparams(("parallel", "arbitrary"), 32),
        name="retstate",
    )(*args)


def _mix0_kernel(dec_ref, q_ref, k_ref, v_ref, g_ref, u_ref, vv_ref, sfp_ref, sbn_ref, ws_ref, bs_ref,
                 o_ref, *, tc):
    row = _row_iota()
    col = lax.broadcasted_iota(jnp.int32, (RET_CHUNK, LANES), 1).astype(f32)
    c = float(RET_CHUNK)
    nt = (((1,), (1,)), ((), ()))
    for hh in range(RET_HEADS):
        lg_f = jax.nn.log_sigmoid(dec_ref[0, hh])
        lg_b = jax.nn.log_sigmoid(dec_ref[1, hh])
        decay = jnp.where(row >= col,
                          jnp.exp(lg_f * jnp.maximum(row - col, 0.0)),
                          jnp.exp(lg_b * jnp.maximum(col - row - 1.0, 0.0)))
        qd_f = jnp.exp(lg_f * (row + 1.0))
        qd_b = jnp.exp(lg_b * (c - 1.0 - row))
        cs = slice(hh * RET_DK, (hh + 1) * RET_DK)
        for cc in range(tc):
            rs = slice(cc * RET_CHUNK, (cc + 1) * RET_CHUNK)
            q = q_ref[rs, cs]
            qf = q.astype(f32)
            s = lax.dot_general(q, k_ref[rs, cs], nt, preferred_element_type=f32)
            o = jnp.dot((s * decay).astype(bf16), v_ref[rs, cs], preferred_element_type=f32)
            o = o + jnp.dot((qf * qd_f).astype(bf16), sfp_ref[cc, hh], preferred_element_type=f32)
            o = o + jnp.dot((qf * qd_b).astype(bf16), sbn_ref[cc, hh], preferred_element_type=f32)
            mu = jnp.mean(o, axis=-1, keepdims=True)
            var = jnp.mean(jnp.square(o - mu), axis=-1, keepdims=True)
            on = (o - mu) * lax.rsqrt(var + EPS)
            gate = g_ref[rs, cs].astype(f32)
            o_ref[rs, cs] = (gate * jax.nn.sigmoid(gate) * on).astype(bf16)
    off = RET_HEADS * RET_DV
    for gg in range(GMLP_GROUPS):
        cs = slice(gg * GMLP_CH, (gg + 1) * GMLP_CH)
        w = ws_ref[gg]
        bias = bs_ref[gg]
        for cc in range(tc):
            rs = slice(cc * RET_CHUNK, (cc + 1) * RET_CHUNK)
            gu = jax.nn.gelu(u_ref[rs, cs].astype(f32))
            gv = jax.nn.gelu(vv_ref[rs, cs].astype(f32))
            gv = gv * lax.rsqrt(jnp.mean(gv * gv, axis=-1, keepdims=True) + EPS)
            sp = jnp.dot(w, gv.astype(bf16), preferred_element_type=f32) + bias
            o_ref[rs, off + gg * GMLP_CH: off + (gg + 1) * GMLP_CH] = (gu * sp).astype(bf16)


def _mix0_call(qkv, dec, sfp, sbn, ws, bs, tc):
    bsz, ls, _ = qkv.shape
    t = tc * RET_CHUNK
    hw = RET_HEADS * RET_DK
    st = (RET_HEADS, RET_DK, RET_DV)
    col_spec = lambda j: pl.BlockSpec((None, t, hw), lambda b, i, j=j: (b, i, j))
    return pl.pallas_call(
        functools.partial(_mix0_kernel, tc=tc),
        grid=(bsz, ls // t),
        in_specs=[
            pl.BlockSpec((2, RET_HEADS, 1, LANES), lambda b, i: (0, 0, 0, 0)),
            col_spec(0), col_spec(1), col_spec(2), col_spec(3), col_spec(4), col_spec(5),
            pl.BlockSpec((None, tc) + st, lambda b, i: (b, i, 0, 0, 0)),
            pl.BlockSpec((None, tc) + st, lambda b, i: (b, i, 0, 0, 0)),
            pl.BlockSpec((GMLP_GROUPS, RET_CHUNK, RET_CHUNK), lambda b, i: (0, 0, 0)),
            pl.BlockSpec((GMLP_GROUPS, RET_CHUNK, 1), lambda b, i: (0, 0, 0)),
        ],
        out_specs=pl.BlockSpec((None, t, AB_OUT), lambda b, i: (b, i, 0)),
        out_shape=jax.ShapeDtypeStruct((bsz, ls, AB_OUT), bf16),
        compiler_params=_cparams(("parallel", "parallel"), 32),
        name="mix0",
    )(dec, qkv, qkv, qkv, qkv, qkv, qkv, sfp, sbn, ws, bs)


def _lru_kernel(gate_ref, xin_ref, cw_ref, cb_ref, wg_ref, bg_ref, lam_ref, h0f_ref, h0b_ref,
                o_ref, hfl_ref, hbf_ref, xs, af, bfs, ab, bbs, *, chained):
    seg, nseg, pitch = LRU_SEG, LRU_NSEG, LRU_PITCH
    rows = seg * nseg
    pad = SUBLANES

    xs[0:pad, :] = jnp.zeros((pad, LANES), f32)
    xs[pad + rows:pad + rows + pad, :] = jnp.zeros((pad, LANES), f32)

    def stage(s, carry):
        r0 = pl.multiple_of(s * seg, seg)
        xs[pl.ds(pad + r0, seg), :] = xin_ref[pl.ds(r0, seg), :].astype(f32)
        return carry

    lax.fori_loop(0, nseg, stage, 0)

    cw = cw_ref[...]
    cb = cb_ref[...]
    bg = bg_ref[...]
    sp8 = LRU_C * jax.nn.softplus(-lam_ref[...])
    local = lax.broadcasted_iota(jnp.int32, (seg, LANES), 0)

    def gates(s, carry):
        r0 = pl.multiple_of(s * seg, seg)
        win = xs[pl.ds(r0, seg + 2 * pad), :]
        x0 = win[pad:pad + seg]
        xm2 = pltpu.roll(win, 2, axis=0)[pad:pad + seg]
        xm1 = pltpu.roll(win, 1, axis=0)[pad:pad + seg]
        xp1 = pltpu.roll(win, seg + 2 * pad - 1, axis=0)[pad:pad + seg]
        if not chained:
            xm2 = jnp.where(local >= 2, xm2, 0.0)
            xm1 = jnp.where(local >= 1, xm1, 0.0)
            xp1 = jnp.where(local < seg - 1, xp1, 0.0)
        xc = cb + xm2 * cw[0:1] + xm1 * cw[1:2] + x0 * cw[2:3] + xp1 * cw[3:4]
        gt = jnp.dot(xc.astype(bf16), wg_ref[...], preferred_element_type=f32) + bg
        p0 = pl.multiple_of(s * pitch, SUBLANES)
        for d, (a_s, b_s) in enumerate(((af, bfs), (ab, bbs))):
            r = jax.nn.sigmoid(gt[:, (2 * d) * LANES:(2 * d + 1) * LANES])
            ig = jax.nn.sigmoid(gt[:, (2 * d + 1) * LANES:(2 * d + 2) * LANES])
            nla = r * sp8[d:d + 1]
            a = jnp.exp(-nla)
            m2 = jnp.tanh(nla) * (1.0 + a * a)
            mult = m2 * lax.rsqrt(jnp.maximum(m2, 1e-30))
            a_s[pl.ds(p0, seg), :] = a
            b_s[pl.ds(p0, seg), :] = mult * (ig * xc)
        return carry

    lax.fori_loop(0, nseg, gates, 0, unroll=4)

    def ld(ref, j):
        return ref[pl.ds(j, nseg, stride=pitch), :]

    if chained:
        def ends(j, carry):
            hf, pf, hb, pb = carry
            a = ld(af, j)
            hf = a * hf + ld(bfs, j)
            pf = a * pf
            a2 = ld(ab, seg - 1 - j)
            hb = a2 * hb + ld(bbs, seg - 1 - j)
            pb = a2 * pb
            return hf, pf, hb, pb

        zero = jnp.zeros((nseg, LANES), f32)
        one = jnp.ones((nseg, LANES), f32)
        hf_e, pf_e, hb_e, pb_e = lax.fori_loop(0, seg, ends, (zero, one, zero, one), unroll=4)
        cf = [h0f_ref[0:1, :]]
        for s in range(1, nseg):
            cf.append(hf_e[s - 1:s] + pf_e[s - 1:s] * cf[-1])
        cbk = [h0b_ref[0:1, :]]
        for s in range(nseg - 2, -1, -1):
            cbk.append(hb_e[s + 1:s + 2] + pb_e[s + 1:s + 2] * cbk[-1])
        hf0 = jnp.concatenate(cf, axis=0)
        hb0 = jnp.concatenate(cbk[::-1], axis=0)
    else:
        hf0 = h0f_ref[...]
        hb0 = h0b_ref[...]

    def scan(j, carry):
        hf, hb = carry
        hf = ld(af, j) * hf + ld(bfs, j)
        bfs[pl.ds(j, nseg, stride=pitch), :] = hf
        jj = seg - 1 - j
        hb = ld(ab, jj) * hb + ld(bbs, jj)
        bbs[pl.ds(jj, nseg, stride=pitch), :] = hb
        return hf, hb

    hf_l, hb_l = lax.fori_loop(0, seg, scan, (hf0, hb0), unroll=4)
    if chained:
        hfl_ref[...] = hf_l[nseg - 1:nseg]
        hbf_ref[...] = hb_l[0:1]
    else:
        hfl_ref[...] = hf_l
        hbf_ref[...] = hb_l

    def combine(s, carry):
        p0 = pl.multiple_of(s * pitch, SUBLANES)
        r0 = pl.multiple_of(s * seg, seg)
        g = jax.nn.gelu(gate_ref[pl.ds(r0, seg), :].astype(f32))
        o_ref[pl.ds(r0, seg), :] = (g * (bfs[pl.ds(p0, seg), :] + bbs[pl.ds(p0, seg), :])).astype(bf16)
        return carry

    lax.fori_loop(0, nseg, combine, 0)


def _lru_call(gx, cw, cb, wg, bg, lam, h0f, h0b, chained):
    ntok = gx.shape[0]
    rows = LRU_SEG * LRU_NSEG
    nt = ntok // rows
    gc = h0f.shape[1]
    nb = LRU_BLOCKS
    scr = pltpu.VMEM((LRU_NSEG * LRU_PITCH, LANES), f32)
    return pl.pallas_call(
        functools.partial(_lru_kernel, chained=chained),
        grid=(nt, nb),
        in_specs=[
            pl.BlockSpec((rows, LANES), lambda t, n: (t, n)),
            pl.BlockSpec((rows, LANES), lambda t, n: (t, nb + n)),
            pl.BlockSpec((4, LANES), lambda t, n: (0, n)),
            pl.BlockSpec((1, LANES), lambda t, n: (0, n)),
            pl.BlockSpec((None, LANES, 4 * LANES), lambda t, n: (n, 0, 0)),
            pl.BlockSpec((None, 1, 4 * LANES), lambda t, n: (n, 0, 0)),
            pl.BlockSpec((2, LANES), lambda t, n: (0, n)),
            pl.BlockSpec((None, gc, LANES), lambda t, n: (t, 0, n)),
            pl.BlockSpec((None, gc, LANES), lambda t, n: (t, 0, n)),
        ],
        out_specs=[
            pl.BlockSpec((rows, LANES), lambda t, n: (t, n)),
            pl.BlockSpec((None, gc, LANES), lambda t, n: (t, 0, n)),
            pl.BlockSpec((None, gc, LANES), lambda t, n: (t, 0, n)),
        ],
        out_shape=[
            jax.ShapeDtypeStruct((ntok, D_RNN), bf16),
            jax.ShapeDtypeStruct((nt, gc, D_RNN), f32),
            jax.ShapeDtypeStruct((nt, gc, D_RNN), f32),
        ],
        scratch_shapes=[pltpu.VMEM((rows + 2 * SUBLANES, LANES), f32), scr, scr, scr, scr],
        compiler_params=_cparams(("parallel", "parallel"), 40),
        name="rglru",
    )(gx, gx, cw, cb, wg, bg, lam, h0f, h0b)


def _outmlp_kernel(x_ref, mix_ref, g1_ref, sh2_ref, sc2_ref, g2_ref, gain_ref, fgain_ref,
                   wo_ref, w1_ref, w2_ref, o_ref, *, final):
    mix = jnp.dot(mix_ref[...], wo_ref[...], preferred_element_type=f32)
    x1 = x_ref[...] + g1_ref[...] * mix
    h = (_rms(x1, gain_ref[...]) * (1.0 + sc2_ref[...]) + sh2_ref[...]).astype(bf16)
    acc = None
    for j in range(0, D_FF, FF_CHUNK):
        hid = jnp.dot(h, w1_ref[:, j:j + FF_CHUNK], preferred_element_type=f32)
        hid = jnp.square(jnp.maximum(hid, 0.0)).astype(bf16)
        part = jnp.dot(hid, w2_ref[j:j + FF_CHUNK, :], preferred_element_type=f32)
        acc = part if acc is None else acc + part
    x2 = x1 + g2_ref[...] * acc
    if final:
        x2 = _rms(x2, fgain_ref[...])
    o_ref[...] = x2


def _outmlp_call(x3, mix3, modl, gain, fgain, wo, w1, w2, *, layer, final):
    bg, lg, d = x3.shape
    t = TOK_TILE
    dm = mix3.shape[-1]
    mod_spec = lambda k: pl.BlockSpec((None, 1, d), lambda b, i, k=k: (b, 0, k))

    def resident(shape, lead=None):
        if lead is None:
            return pl.BlockSpec(shape, lambda b, i: (0, 0), pipeline_mode=pl.Buffered(1))
        return pl.BlockSpec((None,) + shape, lambda b, i: (lead, 0, 0), pipeline_mode=pl.Buffered(1))

    return pl.pallas_call(
        functools.partial(_outmlp_kernel, final=final),
        grid=(bg, lg // t),
        in_specs=[
            pl.BlockSpec((None, t, d), lambda b, i: (b, i, 0)),
            pl.BlockSpec((None, t, dm), lambda b, i: (b, i, 0)),
            mod_spec(2), mod_spec(3), mod_spec(4), mod_spec(5),
            pl.BlockSpec((1, d), lambda b, i: (0, 0)),
            pl.BlockSpec((1, d), lambda b, i: (0, 0)),
            resident((dm, d)), resident((d, D_FF), layer), resident((D_FF, d), layer),
        ],
        out_specs=pl.BlockSpec((None, t, d), lambda b, i: (b, i, 0)),
        out_shape=jax.ShapeDtypeStruct((bg, lg, d), f32),
        compiler_params=_cparams(("parallel", "parallel"), 56),
        name="outmlp_final" if final else "outmlp",
    )(x3, mix3, modl, modl, modl, modl, gain.reshape(1, d), fgain.reshape(1, d), wo, w1, w2)


@functools.lru_cache(maxsize=None)
def _rope_tables(length):
    pos = np.arange(length)
    rowp = (pos // GRID_W).astype(np.float64)
    colp = (pos % GRID_W).astype(np.float64)
    nf = RET_DK // 4
    inv_freq = ROPE_BASE ** (-np.arange(nf, dtype=np.float64) / nf)
    ar = rowp[:, None] * inv_freq[None, :]
    ac = colp[:, None] * inv_freq[None, :]
    cos = np.concatenate([np.cos(ar), np.cos(ar), np.cos(ac), np.cos(ac)], axis=-1).astype(np.float32)
    sin = np.concatenate([-np.sin(ar), np.sin(ar), -np.sin(ac), np.sin(ac)], axis=-1).astype(np.float32)
    return cos, sin


def kernel(x_prompt, x_sample, c, c_ctx, state_ret_fwd, state_ret_bwd, state_lru_fwd, state_lru_bwd, mod_w, mod_b, norm_mix, norm_mlp, mlp_w1, mlp_w2, ab_w_in, ab_w_out, ret_decay_fwd, ret_decay_bwd, gmlp_ws, gmlp_bs, lru_w_in, lru_conv_w, lru_conv_b, lru_gate_a_w, lru_gate_a_b, lru_gate_x_w, lru_gate_x_b, lru_lambda, lru_w_out, final_norm):
    d = D_MODEL
    n_lat = c.shape[0]
    conds = jnp.concatenate([c_ctx[None, :], c, jnp.zeros((COND_ROWS - 1 - n_lat, d), f32)], axis=0)
    mod = _mod_call(conds, mod_w, mod_b)

    w_in0 = ab_w_in[0].astype(bf16)
    w_out0 = ab_w_out[0].astype(bf16)
    w1 = mlp_w1.astype(bf16)
    w2 = mlp_w2.astype(bf16)
    w_in1 = lru_w_in[0].astype(bf16)
    w_out1 = lru_w_out[0].astype(bf16)
    ws = gmlp_ws[0].astype(bf16)
    bs = gmlp_bs[0][:, :, None]
    dec = jnp.broadcast_to(jnp.stack([ret_decay_fwd[0], ret_decay_bwd[0]])[:, :, None, None],
                           (2, RET_HEADS, 1, LANES))
    wg = jnp.concatenate([lru_gate_a_w[0, 0], lru_gate_x_w[0, 0], lru_gate_a_w[0, 1], lru_gate_x_w[0, 1]],
                         axis=-1).astype(bf16)
    bgate = jnp.stack([lru_gate_a_b[0, 0], lru_gate_x_b[0, 0], lru_gate_a_b[0, 1], lru_gate_x_b[0, 1]], axis=0)
    bgate = bgate.reshape(4, LRU_BLOCKS, LRU_BS).transpose(1, 0, 2).reshape(LRU_BLOCKS, 1, 4 * LRU_BS)

    def trunk(x, mod_rows, ret_f0, ret_b0, lru_f0, lru_b0, latent):
        bsz, ls, _ = x.shape
        x3 = x if latent else x.reshape(1, bsz * ls, d)
        modl = [mod[l, mod_rows][:, None, :] for l in range(DEPTH)]
        tc = min(4, ls // RET_CHUNK)

        qkv = _inproj_call(x3, modl[0], norm_mix[0], w_in0, _rope_tables(ls) if latent else None,
                           retention=True, sh_col=0, sc_col=1)
        qkv = qkv.reshape(bsz, ls, AB_IN)
        sfp, sbn, sff, sbf = _retstate_call(qkv, dec, ret_f0, ret_b0, tc)
        mix = _mix0_call(qkv, dec, sfp, sbn, ws, bs, tc)
        x3 = _outmlp_call(x3, mix.reshape(x3.shape[0], x3.shape[1], AB_OUT), modl[0], norm_mlp[0], final_norm,
                          w_out0, w1, w2, layer=0, final=False)

        gx = _inproj_call(x3, modl[1], norm_mix[1], w_in1, None, retention=False, sh_col=0, sc_col=1)
        rows = LRU_SEG * LRU_NSEG
        nt = bsz * ls // rows
        gc = 1 if latent else LRU_NSEG
        pre, hfl, hbf = _lru_call(gx.reshape(bsz * ls, 2 * D_RNN), lru_conv_w[0], lru_conv_b[0].reshape(1, D_RNN),
                                  wg, bgate, lru_lambda[0], lru_f0.reshape(nt, gc, D_RNN),
                                  lru_b0.reshape(nt, gc, D_RNN), chained=latent)
        y = _outmlp_call(x3, pre.reshape(x3.shape[0], x3.shape[1], D_RNN), modl[1], norm_mlp[1], final_norm,
                         w_out1, w1, w2, layer=1, final=True)
        return (y.reshape(bsz, ls, d), sff[:, None], sbf[:, None],
                hfl.reshape(bsz, 1, D_RNN), hbf.reshape(bsz, 1, D_RNN))

    bp = x_prompt.shape[0]
    zero_lru = jnp.zeros((bp, D_RNN), f32)
    y_prompt, new_ret_fwd, new_ret_bwd, new_lru_fwd, new_lru_bwd = trunk(
        x_prompt, slice(0, 1), None, None, zero_lru, zero_lru, False)
    y_sample = trunk(x_sample, slice(1, 1 + n_lat), state_ret_fwd[:, 0], state_ret_bwd[:, 0],
                     state_lru_fwd[:, 0], state_lru_bwd[:, 0], True)[0]
    return (y_prompt, y_sample, new_ret_fwd, new_ret_bwd, new_lru_fwd, new_lru_bwd)
```

```python
import functools

import jax
import jax.numpy as jnp
import numpy as np
from jax import lax
from jax.experimental import pallas as pl
from jax.experimental.pallas import tpu as pltpu

f32 = jnp.float32
bf16 = jnp.bfloat16

D_MODEL = 1024
DEPTH = 2
GRID_W = 64
EPS = 1e-6
RET_HEADS = 4
RET_DK = 128
RET_DV = 128
RET_CHUNK = 128
ROPE_BASE = 10000.0
GMLP_GROUPS = 4
GMLP_CH = 128
AB_IN = 3072
AB_OUT = 1024
D_RNN = D_MODEL
LRU_BLOCKS = 8
LRU_BS = 128
LRU_C = 8.0
D_FF = 4 * D_MODEL

LANES = 128
SUBLANES = 8
MIB = 1024 * 1024

TOK_TILE = 512
COND_ROWS = 8
MOD_TN = 1536
LRU_SEG = 256
LRU_NSEG = 16
LRU_PITCH = LRU_SEG + SUBLANES
FF_CHUNK = 1024


def _cparams(sem, vmem_mib):
    return pltpu.CompilerParams(dimension_semantics=sem, vmem_limit_bytes=vmem_mib * MIB)


def _rms(x, gain):
    ms = jnp.mean(x * x, axis=-1, keepdims=True)
    return x * lax.rsqrt(ms + EPS) * gain


def _mod_kernel(cond_ref, w_ref, b_ref, o_ref):
    c = cond_ref[...]
    s = c * jax.nn.sigmoid(c)
    o_ref[...] = jnp.dot(s.astype(bf16), w_ref[...].astype(bf16), preferred_element_type=f32) + b_ref[...]


def _mod_call(conds, mod_w, mod_b):
    depth, d, n = mod_w.shape
    return pl.pallas_call(
        _mod_kernel,
        grid=(depth, n // MOD_TN),
        in_specs=[
            pl.BlockSpec((COND_ROWS, d), lambda l, j: (0, 0)),
            pl.BlockSpec((None, d, MOD_TN), lambda l, j: (l, 0, j)),
            pl.BlockSpec((None, 1, MOD_TN), lambda l, j: (l, 0, j)),
        ],
        out_specs=pl.BlockSpec((None, COND_ROWS, MOD_TN), lambda l, j: (l, 0, j)),
        out_shape=jax.ShapeDtypeStruct((depth, COND_ROWS, n), f32),
        compiler_params=_cparams(("parallel", "parallel"), 32),
        name="mod",
    )(conds, mod_w, mod_b.reshape(depth, 1, n))


def _rope(t, cos, sin_signed, first_half):
    up = pltpu.roll(t, LANES - 32, axis=1)
    dn = pltpu.roll(t, 32, axis=1)
    return t * cos + jnp.where(first_half, up, dn) * sin_signed


def _inproj_kernel(*refs, retention, rope):
    if rope:
        x_ref, sh_ref, sc_ref, gain_ref, w_ref, cos_ref, sin_ref, o_ref = refs
    else:
        x_ref, sh_ref, sc_ref, gain_ref, w_ref, o_ref = refs
    h = _rms(x_ref[...], gain_ref[...]) * (1.0 + sc_ref[...]) + sh_ref[...]
    hb = h.astype(bf16)
    n = w_ref.shape[1]
    if not retention:
        for j in range(0, n, 512):
            o_ref[:, j:j + 512] = jnp.dot(hb, w_ref[:, j:j + 512], preferred_element_type=f32).astype(bf16)
        return
    qk = RET_HEADS * RET_DK
    if rope:
        cos = cos_ref[...]
        sin = sin_ref[...]
        lane = lax.broadcasted_iota(jnp.int32, cos.shape, 1)
        first_half = (lane % 64) < 32
    for part in range(2):
        y = jnp.dot(hb, w_ref[:, part * qk:(part + 1) * qk], preferred_element_type=f32)
        for hh in range(RET_HEADS):
            t = y[:, hh * RET_DK:(hh + 1) * RET_DK]
            if part == 1:
                t = t * (RET_DK ** -0.5)
            if rope:
                t = _rope(t, cos, sin, first_half)
            c0 = part * qk + hh * RET_DK
            o_ref[:, c0:c0 + RET_DK] = t.astype(bf16)
    for j in range(2 * qk, n, 512):
        o_ref[:, j:j + 512] = jnp.dot(hb, w_ref[:, j:j + 512], preferred_element_type=f32).astype(bf16)


def _inproj_call(x3, modl, gain, w, rope_tabs, *, retention, sh_col, sc_col):
    bg, lg, d = x3.shape
    n = w.shape[1]
    t = TOK_TILE
    rope = rope_tabs is not None
    in_specs = [
        pl.BlockSpec((None, t, d), lambda b, i: (b, i, 0)),
        pl.BlockSpec((None, 1, d), lambda b, i: (b, 0, sh_col)),
        pl.BlockSpec((None, 1, d), lambda b, i: (b, 0, sc_col)),
        pl.BlockSpec((1, d), lambda b, i: (0, 0)),
        pl.BlockSpec((d, n), lambda b, i: (0, 0)),
    ]
    args = [x3, modl, modl, gain.reshape(1, d), w]
    if rope:
        in_specs += [pl.BlockSpec((t, LANES), lambda b, i: (i, 0))] * 2
        args += list(rope_tabs)
    return pl.pallas_call(
        functools.partial(_inproj_kernel, retention=retention, rope=rope),
        grid=(bg, lg // t),
        in_specs=in_specs,
        out_specs=pl.BlockSpec((None, t, n), lambda b, i: (b, i, 0)),
        out_shape=jax.ShapeDtypeStruct((bg, lg, n), bf16),
        compiler_params=_cparams(("parallel", "parallel"), 48),
        name="inproj_ret" if retention else "inproj_lru",
    )(*args)


def _row_iota():
    return lax.broadcasted_iota(jnp.int32, (RET_CHUNK, LANES), 0).astype(f32)


def _retstate_kernel(*refs, tc, zero_init):
    if zero_init:
        dec_ref, kf_ref, vf_ref, kb_ref, vb_ref, sfp_ref, sbn_ref, sff_ref, sbf_ref, sf_scr, sb_scr = refs
    else:
        (dec_ref, kf_ref, vf_ref, kb_ref, vb_ref, s0f_ref, s0b_ref,
         sfp_ref, sbn_ref, sff_ref, sbf_ref, sf_scr, sb_scr) = refs
    i = pl.program_id(1)

    @pl.when(i == 0)
    def _():
        if zero_init:
            sf_scr[...] = jnp.zeros(sf_scr.shape, f32)
            sb_scr[...] = jnp.zeros(sb_scr.shape, f32)
        else:
            sf_scr[...] = s0f_ref[...]
            sb_scr[...] = s0b_ref[...]

    row = _row_iota()
    c = float(RET_CHUNK)
    for hh in range(RET_HEADS):
        lg_f = jax.nn.log_sigmoid(dec_ref[0, hh])
        lg_b = jax.nn.log_sigmoid(dec_ref[1, hh])
        kd_f = jnp.exp(lg_f * (c - 1.0 - row))
        kd_b = jnp.exp(lg_b * row)
        gc_f = jnp.exp(lg_f * c)
        gc_b = jnp.exp(lg_b * c)
        cs = slice(hh * RET_DK, (hh + 1) * RET_DK)
        for cc in range(tc):
            rs = slice(cc * RET_CHUNK, (cc + 1) * RET_CHUNK)
            s_prev = sf_scr[hh]
            sfp_ref[cc, hh] = s_prev.astype(bf16)
            kd = (kf_ref[rs, cs].astype(f32) * kd_f).astype(bf16)
            kv = lax.dot_general(kd, vf_ref[rs, cs], (((0,), (0,)), ((), ())), preferred_element_type=f32)
            sf_scr[hh] = gc_f * s_prev + kv
        for cc in reversed(range(tc)):
            rs = slice(cc * RET_CHUNK, (cc + 1) * RET_CHUNK)
            s_next = sb_scr[hh]
            sbn_ref[cc, hh] = s_next.astype(bf16)
            kd = (kb_ref[rs, cs].astype(f32) * kd_b).astype(bf16)
            kv = lax.dot_general(kd, vb_ref[rs, cs], (((0,), (0,)), ((), ())), preferred_element_type=f32)
            sb_scr[hh] = gc_b * s_next + kv

    @pl.when(i == pl.num_programs(1) - 1)
    def _():
        sff_ref[...] = sf_scr[...]
        sbf_ref[...] = sb_scr[...]


def _retstate_call(qkv, dec, s0f, s0b, tc):
    bs, ls, _ = qkv.shape
    nc = ls // RET_CHUNK
    t = tc * RET_CHUNK
    ns = nc // tc
    hw = RET_HEADS * RET_DK
    st = (RET_HEADS, RET_DK, RET_DV)
    zero_init = s0f is None
    in_specs = [
        pl.BlockSpec((2, RET_HEADS, 1, LANES), lambda b, i: (0, 0, 0, 0)),
        pl.BlockSpec((None, t, hw), lambda b, i: (b, i, 1)),
        pl.BlockSpec((None, t, hw), lambda b, i: (b, i, 2)),
        pl.BlockSpec((None, t, hw), lambda b, i: (b, ns - 1 - i, 1)),
        pl.BlockSpec((None, t, hw), lambda b, i: (b, ns - 1 - i, 2)),
    ]
    args = [dec, qkv, qkv, qkv, qkv]
    if not zero_init:
        in_specs += [pl.BlockSpec((None,) + st, lambda b, i: (b, 0, 0, 0))] * 2
        args += [s0f, s0b]
    return pl.pallas_call(
        functools.partial(_retstate_kernel, tc=tc, zero_init=zero_init),
        grid=(bs, ns),
        in_specs=in_specs,
        out_specs=[
            pl.BlockSpec((None, tc) + st, lambda b, i: (b, i, 0, 0, 0)),
            pl.BlockSpec((None, tc) + st, lambda b, i: (b, ns - 1 - i, 0, 0, 0)),
            pl.BlockSpec((None,) + st, lambda b, i: (b, 0, 0, 0)),
            pl.BlockSpec((None,) + st, lambda b, i: (b, 0, 0, 0)),
        ],
        out_shape=[
            jax.ShapeDtypeStruct((bs, nc) + st, bf16),
            jax.ShapeDtypeStruct((bs, nc) + st, bf16),
            jax.ShapeDtypeStruct((bs,) + st, f32),
            jax.ShapeDtypeStruct((bs,) + st, f32),
        ],
        scratch_shapes=[pltpu.VMEM(st, f32), pltpu.VMEM(st, f32)],
        compiler_params=_cparams(("parallel", "arbitrary"), 32),
        name="retstate",
    )(*args)


def _mix0_kernel(dec_ref, q_ref, k_ref, v_ref, g_ref, u_ref, vv_ref, sfp_ref, sbn_ref, ws_ref, bs_ref,
                 o_ref, *, tc):
    row = _row_iota()
    col = lax.broadcasted_iota(jnp.int32, (RET_CHUNK, LANES), 1).astype(f32)
    c = float(RET_CHUNK)
    nt = (((1,), (1,)), ((), ()))
    for hh in range(RET_HEADS):
        lg_f = jax.nn.log_sigmoid(dec_ref[0, hh])
        lg_b = jax.nn.log_sigmoid(dec_ref[1, hh])
        decay = jnp.where(row >= col,
                          jnp.exp(lg_f * jnp.maximum(row - col, 0.0)),
                          jnp.exp(lg_b * jnp.maximum(col - row - 1.0, 0.0)))
        qd_f = jnp.exp(lg_f * (row + 1.0))
        qd_b = jnp.exp(lg_b * (c - 1.0 - row))
        cs = slice(hh * RET_DK, (hh + 1) * RET_DK)
        for cc in range(tc):
            rs = slice(cc * RET_CHUNK, (cc + 1) * RET_CHUNK)
            q = q_ref[rs, cs]
            qf = q.astype(f32)
            s = lax.dot_general(q, k_ref[rs, cs], nt, preferred_element_type=f32)
            o = jnp.dot((s * decay).astype(bf16), v_ref[rs, cs], preferred_element_type=f32)
            o = o + jnp.dot((qf * qd_f).astype(bf16), sfp_ref[cc, hh], preferred_element_type=f32)
            o = o + jnp.dot((qf * qd_b).astype(bf16), sbn_ref[cc, hh], preferred_element_type=f32)
            mu = jnp.mean(o, axis=-1, keepdims=True)
            var = jnp.mean(jnp.square(o - mu), axis=-1, keepdims=True)
            on = (o - mu) * lax.rsqrt(var + EPS)
            gate = g_ref[rs, cs].astype(f32)
            o_ref[rs, cs] = (gate * jax.nn.sigmoid(gate) * on).astype(bf16)
    off = RET_HEADS * RET_DV
    for gg in range(GMLP_GROUPS):
        cs = slice(gg * GMLP_CH, (gg + 1) * GMLP_CH)
        w = ws_ref[gg]
        bias = bs_ref[gg]
        for cc in range(tc):
            rs = slice(cc * RET_CHUNK, (cc + 1) * RET_CHUNK)
            gu = jax.nn.gelu(u_ref[rs, cs].astype(f32))
            gv = jax.nn.gelu(vv_ref[rs, cs].astype(f32))
            gv = gv * lax.rsqrt(jnp.mean(gv * gv, axis=-1, keepdims=True) + EPS)
            sp = jnp.dot(w, gv.astype(bf16), preferred_element_type=f32) + bias
            o_ref[rs, off + gg * GMLP_CH: off + (gg + 1) * GMLP_CH] = (gu * sp).astype(bf16)


def _mix0_call(qkv, dec, sfp, sbn, ws, bs, tc):
    bsz, ls, _ = qkv.shape
    t = tc * RET_CHUNK
    hw = RET_HEADS * RET_DK
    st = (RET_HEADS, RET_DK, RET_DV)
    col_spec = lambda j: pl.BlockSpec((None, t, hw), lambda b, i, j=j: (b, i, j))
    return pl.pallas_call(
        functools.partial(_mix0_kernel, tc=tc),
        grid=(bsz, ls // t),
        in_specs=[
            pl.BlockSpec((2, RET_HEADS, 1, LANES), lambda b, i: (0, 0, 0, 0)),
            col_spec(0), col_spec(1), col_spec(2), col_spec(3), col_spec(4), col_spec(5),
            pl.BlockSpec((None, tc) + st, lambda b, i: (b, i, 0, 0, 0)),
            pl.BlockSpec((None, tc) + st, lambda b, i: (b, i, 0, 0, 0)),
            pl.BlockSpec((GMLP_GROUPS, RET_CHUNK, RET_CHUNK), lambda b, i: (0, 0, 0)),
            pl.BlockSpec((GMLP_GROUPS, RET_CHUNK, 1), lambda b, i: (0, 0, 0)),
        ],
        out_specs=pl.BlockSpec((None, t, AB_OUT), lambda b, i: (b, i, 0)),
        out_shape=jax.ShapeDtypeStruct((bsz, ls, AB_OUT), bf16),
        compiler_params=_cparams(("parallel", "parallel"), 32),
        name="mix0",
    )(dec, qkv, qkv, qkv, qkv, qkv, qkv, sfp, sbn, ws, bs)


def _lru_kernel(gate_ref, xin_ref, cw_ref, cb_ref, wg_ref, bg_ref, lam_ref, h0f_ref, h0b_ref,
                o_ref, hfl_ref, hbf_ref, xs, xt, af, bfs, ab, bbs, *, chained):
    seg, nseg, pitch = LRU_SEG, LRU_NSEG, LRU_PITCH
    blk = seg

    def stage(s, carry):
        r0 = pl.multiple_of(s * seg, seg)
        p0 = pl.multiple_of(s * pitch, SUBLANES)
        xs[pl.ds(p0, seg), :] = xin_ref[pl.ds(r0, seg), :].astype(f32)
        return carry

    lax.fori_loop(0, nseg, stage, 0)

    def tile(j):
        if isinstance(j, int):
            return pl.ds(j * nseg, nseg)
        return pl.ds(pl.multiple_of(j * nseg, nseg), nseg)

    def gather(j, carry):
        xt[tile(j + 2), :] = xs[pl.ds(j, nseg, stride=pitch), :]
        return carry

    lax.fori_loop(0, seg, gather, 0, unroll=8)
    zt = jnp.zeros((nseg, LANES), f32)
    if chained:
        srow = lax.broadcasted_iota(jnp.int32, (nseg, LANES), 0)
        xt[tile(0), :] = jnp.where(srow == 0, 0.0, pltpu.roll(xt[tile(seg), :], 1, axis=0))
        xt[tile(1), :] = jnp.where(srow == 0, 0.0, pltpu.roll(xt[tile(seg + 1), :], 1, axis=0))
        xt[tile(seg + 2), :] = jnp.where(srow == nseg - 1, 0.0, pltpu.roll(xt[tile(2), :], nseg - 1, axis=0))
    else:
        xt[tile(0), :] = zt
        xt[tile(1), :] = zt
        xt[tile(seg + 2), :] = zt

    cw = cw_ref[...]
    cb = cb_ref[...]
    bg = bg_ref[...]
    sp4 = (0.5 * LRU_C) * jax.nn.softplus(-lam_ref[...])

    def gates(jb, carry):
        p0 = pl.multiple_of(jb * blk, blk)
        xm2 = xt[pl.ds(p0, blk), :]
        xm1 = xt[pl.ds(pl.multiple_of(p0 + nseg, nseg), blk), :]
        x0 = xt[pl.ds(pl.multiple_of(p0 + 2 * nseg, nseg), blk), :]
        xp1 = xt[pl.ds(pl.multiple_of(p0 + 3 * nseg, nseg), blk), :]
        xc = cb + xm2 * cw[0:1] + xm1 * cw[1:2] + x0 * cw[2:3] + xp1 * cw[3:4]
        gt = jnp.dot(xc.astype(bf16), wg_ref[...], preferred_element_type=f32) + bg
        xch = 0.5 * xc
        for d, (a_s, b_s) in enumerate(((af, bfs), (ab, bbs))):
            tr = jnp.tanh(gt[:, (2 * d) * LANES:(2 * d + 1) * LANES])
            ti = jnp.tanh(gt[:, (2 * d + 1) * LANES:(2 * d + 2) * LANES])
            nla = (tr + 1.0) * sp4[d:d + 1]
            a = jnp.exp(-nla)
            m2 = jnp.tanh(nla) * (1.0 + a * a)
            mult = m2 * lax.rsqrt(jnp.maximum(m2, 1e-30))
            a_s[pl.ds(p0, seg), :] = a
            b_s[pl.ds(p0, seg), :] = mult * ((ti + 1.0) * xch)
        return carry

    lax.fori_loop(0, nseg, gates, 0, unroll=4)

    if chained:
        def ends(j, carry):
            hf, pf, hb, pb = carry
            a = af[tile(j), :]
            hf = a * hf + bfs[tile(j), :]
            pf = a * pf
            a2 = ab[tile(seg - 1 - j), :]
            hb = a2 * hb + bbs[tile(seg - 1 - j), :]
            pb = a2 * pb
            return hf, pf, hb, pb

        zero = jnp.zeros((nseg, LANES), f32)
        one = jnp.ones((nseg, LANES), f32)
        hf_e, pf_e, hb_e, pb_e = lax.fori_loop(0, seg, ends, (zero, one, zero, one), unroll=4)
        cf = [h0f_ref[0:1, :]]
        for s in range(1, nseg):
            cf.append(hf_e[s - 1:s] + pf_e[s - 1:s] * cf[-1])
        cbk = [h0b_ref[0:1, :]]
        for s in range(nseg - 2, -1, -1):
            cbk.append(hb_e[s + 1:s + 2] + pb_e[s + 1:s + 2] * cbk[-1])
        hf0 = jnp.concatenate(cf, axis=0)
        hb0 = jnp.concatenate(cbk[::-1], axis=0)
    else:
        hf0 = h0f_ref[...]
        hb0 = h0b_ref[...]

    def scan_b(k, hb):
        j = seg - 1 - k
        hb = ab[tile(j), :] * hb + bbs[tile(j), :]
        xt[tile(j), :] = hb
        return hb

    hb_l = lax.fori_loop(0, seg, scan_b, hb0, unroll=8)

    def scan_f(j, hf):
        hf = af[tile(j), :] * hf + bfs[tile(j), :]
        xs[pl.ds(j, nseg, stride=pitch), :] = hf + xt[tile(j), :]
        return hf

    hf_l = lax.fori_loop(0, seg, scan_f, hf0, unroll=8)
    if chained:
        hfl_ref[...] = hf_l[nseg - 1:nseg]
        hbf_ref[...] = hb_l[0:1]
    else:
        hfl_ref[...] = hf_l
        hbf_ref[...] = hb_l

    def combine(s, carry):
        p0 = pl.multiple_of(s * pitch, SUBLANES)
        r0 = pl.multiple_of(s * seg, seg)
        g = jax.nn.gelu(gate_ref[pl.ds(r0, seg), :].astype(f32))
        o_ref[pl.ds(r0, seg), :] = (g * xs[pl.ds(p0, seg), :]).astype(bf16)
        return carry

    lax.fori_loop(0, nseg, combine, 0)


def _lru_call(gx, cw, cb, wg, bg, lam, h0f, h0b, chained):
    ntok = gx.shape[0]
    rows = LRU_SEG * LRU_NSEG
    nt = ntok // rows
    gc = h0f.shape[1]
    nb = LRU_BLOCKS
    scr = pltpu.VMEM((rows, LANES), f32)
    return pl.pallas_call(
        functools.partial(_lru_kernel, chained=chained),
        grid=(nt, nb),
        in_specs=[
            pl.BlockSpec((rows, LANES), lambda t, n: (t, n)),
            pl.BlockSpec((rows, LANES), lambda t, n: (t, nb + n)),
            pl.BlockSpec((4, LANES), lambda t, n: (0, n)),
            pl.BlockSpec((1, LANES), lambda t, n: (0, n)),
            pl.BlockSpec((None, LANES, 4 * LANES), lambda t, n: (n, 0, 0)),
            pl.BlockSpec((None, 1, 4 * LANES), lambda t, n: (n, 0, 0)),
            pl.BlockSpec((2, LANES), lambda t, n: (0, n)),
            pl.BlockSpec((None, gc, LANES), lambda t, n: (t, 0, n)),
            pl.BlockSpec((None, gc, LANES), lambda t, n: (t, 0, n)),
        ],
        out_specs=[
            pl.BlockSpec((rows, LANES), lambda t, n: (t, n)),
            pl.BlockSpec((None, gc, LANES), lambda t, n: (t, 0, n)),
            pl.BlockSpec((None, gc, LANES), lambda t, n: (t, 0, n)),
        ],
        out_shape=[
            jax.ShapeDtypeStruct((ntok, D_RNN), bf16),
            jax.ShapeDtypeStruct((nt, gc, D_RNN), f32),
            jax.ShapeDtypeStruct((nt, gc, D_RNN), f32),
        ],
        scratch_shapes=[pltpu.VMEM((LRU_NSEG * LRU_PITCH, LANES), f32),
                        pltpu.VMEM((rows + 3 * LRU_NSEG, LANES), f32), scr, scr, scr, scr],
        compiler_params=_cparams(("parallel", "parallel"), 40),
        name="rglru",
    )(gx, gx, cw, cb, wg, bg, lam, h0f, h0b)


def _outmlp_kernel(x_ref, mix_ref, g1_ref, sh2_ref, sc2_ref, g2_ref, gain_ref, fgain_ref,
                   wo_ref, w1_ref, w2_ref, o_ref, *, final):
    mix = jnp.dot(mix_ref[...], wo_ref[...], preferred_element_type=f32)
    x1 = x_ref[...] + g1_ref[...] * mix
    h = (_rms(x1, gain_ref[...]) * (1.0 + sc2_ref[...]) + sh2_ref[...]).astype(bf16)
    acc = None
    for j in range(0, D_FF, FF_CHUNK):
        hid = jnp.dot(h, w1_ref[:, j:j + FF_CHUNK], preferred_element_type=f32)
        hid = jnp.square(jnp.maximum(hid, 0.0)).astype(bf16)
        part = jnp.dot(hid, w2_ref[j:j + FF_CHUNK, :], preferred_element_type=f32)
        acc = part if acc is None else acc + part
    x2 = x1 + g2_ref[...] * acc
    if final:
        x2 = _rms(x2, fgain_ref[...])
    o_ref[...] = x2


def _outmlp_call(x3, mix3, modl, gain, fgain, wo, w1, w2, *, layer, final):
    bg, lg, d = x3.shape
    t = TOK_TILE
    dm = mix3.shape[-1]
    mod_spec = lambda k: pl.BlockSpec((None, 1, d), lambda b, i, k=k: (b, 0, k))

    def resident(shape, lead=None):
        if lead is None:
            return pl.BlockSpec(shape, lambda b, i: (0, 0), pipeline_mode=pl.Buffered(1))
        return pl.BlockSpec((None,) + shape, lambda b, i: (lead, 0, 0), pipeline_mode=pl.Buffered(1))

    return pl.pallas_call(
        functools.partial(_outmlp_kernel, final=final),
        grid=(bg, lg // t),
        in_specs=[
            pl.BlockSpec((None, t, d), lambda b, i: (b, i, 0)),
            pl.BlockSpec((None, t, dm), lambda b, i: (b, i, 0)),
            mod_spec(2), mod_spec(3), mod_spec(4), mod_spec(5),
            pl.BlockSpec((1, d), lambda b, i: (0, 0)),
            pl.BlockSpec((1, d), lambda b, i: (0, 0)),
            resident((dm, d)), resident((d, D_FF), layer), resident((D_FF, d), layer),
        ],
        out_specs=pl.BlockSpec((None, t, d), lambda b, i: (b, i, 0)),
        out_shape=jax.ShapeDtypeStruct((bg, lg, d), f32),
        compiler_params=_cparams(("parallel", "parallel"), 56),
        name="outmlp_final" if final else "outmlp",
    )(x3, mix3, modl, modl, modl, modl, gain.reshape(1, d), fgain.reshape(1, d), wo, w1, w2)


@functools.lru_cache(maxsize=None)
def _rope_tables(length):
    pos = np.arange(length)
    rowp = (pos // GRID_W).astype(np.float64)
    colp = (pos % GRID_W).astype(np.float64)
    nf = RET_DK // 4
    inv_freq = ROPE_BASE ** (-np.arange(nf, dtype=np.float64) / nf)
    ar = rowp[:, None] * inv_freq[None, :]
    ac = colp[:, None] * inv_freq[None, :]
    cos = np.concatenate([np.cos(ar), np.cos(ar), np.cos(ac), np.cos(ac)], axis=-1).astype(np.float32)
    sin = np.concatenate([-np.sin(ar), np.sin(ar), -np.sin(ac), np.sin(ac)], axis=-1).astype(np.float32)
    return cos, sin


def kernel(x_prompt, x_sample, c, c_ctx, state_ret_fwd, state_ret_bwd, state_lru_fwd, state_lru_bwd, mod_w, mod_b, norm_mix, norm_mlp, mlp_w1, mlp_w2, ab_w_in, ab_w_out, ret_decay_fwd, ret_decay_bwd, gmlp_ws, gmlp_bs, lru_w_in, lru_conv_w, lru_conv_b, lru_gate_a_w, lru_gate_a_b, lru_gate_x_w, lru_gate_x_b, lru_lambda, lru_w_out, final_norm):
    d = D_MODEL
    n_lat = c.shape[0]
    conds = jnp.concatenate([c_ctx[None, :], c, jnp.zeros((COND_ROWS - 1 - n_lat, d), f32)], axis=0)
    mod = _mod_call(conds, mod_w, mod_b)

    w_in0 = ab_w_in[0].astype(bf16)
    w_out0 = ab_w_out[0].astype(bf16)
    w1 = mlp_w1.astype(bf16)
    w2 = mlp_w2.astype(bf16)
    w_in1 = lru_w_in[0].astype(bf16)
    w_out1 = lru_w_out[0].astype(bf16)
    ws = gmlp_ws[0].astype(bf16)
    bs = gmlp_bs[0][:, :, None]
    dec = jnp.broadcast_to(jnp.stack([ret_decay_fwd[0], ret_decay_bwd[0]])[:, :, None, None],
                           (2, RET_HEADS, 1, LANES))
    wg = (0.5 * jnp.concatenate([lru_gate_a_w[0, 0], lru_gate_x_w[0, 0], lru_gate_a_w[0, 1], lru_gate_x_w[0, 1]],
                                axis=-1)).astype(bf16)
    bgate = 0.5 * jnp.stack([lru_gate_a_b[0, 0], lru_gate_x_b[0, 0], lru_gate_a_b[0, 1], lru_gate_x_b[0, 1]], axis=0)
    bgate = bgate.reshape(4, LRU_BLOCKS, LRU_BS).transpose(1, 0, 2).reshape(LRU_BLOCKS, 1, 4 * LRU_BS)

    def trunk(x, mod_rows, ret_f0, ret_b0, lru_f0, lru_b0, latent):
        bsz, ls, _ = x.shape
        x3 = x if latent else x.reshape(1, bsz * ls, d)
        modl = [mod[l, mod_rows][:, None, :] for l in range(DEPTH)]
        tc = min(4, ls // RET_CHUNK)

        qkv = _inproj_call(x3, modl[0], norm_mix[0], w_in0, _rope_tables(ls) if latent else None,
                           retention=True, sh_col=0, sc_col=1)
        qkv = qkv.reshape(bsz, ls, AB_IN)
        sfp, sbn, sff, sbf = _retstate_call(qkv, dec, ret_f0, ret_b0, tc)
        mix = _mix0_call(qkv, dec, sfp, sbn, ws, bs, tc)
        x3 = _outmlp_call(x3, mix.reshape(x3.shape[0], x3.shape[1], AB_OUT), modl[0], norm_mlp[0], final_norm,
                          w_out0, w1, w2, layer=0, final=False)

        gx = _inproj_call(x3, modl[1], norm_mix[1], w_in1, None, retention=False, sh_col=0, sc_col=1)
        rows = LRU_SEG * LRU_NSEG
        nt = bsz * ls // rows
        gc = 1 if latent else LRU_NSEG
        pre, hfl, hbf = _lru_call(gx.reshape(bsz * ls, 2 * D_RNN), lru_conv_w[0], lru_conv_b[0].reshape(1, D_RNN),
                                  wg, bgate, lru_lambda[0], lru_f0.reshape(nt, gc, D_RNN),
                                  lru_b0.reshape(nt, gc, D_RNN), chained=latent)
        y = _outmlp_call(x3, pre.reshape(x3.shape[0], x3.shape[1], D_RNN), modl[1], norm_mlp[1], final_norm,
                         w_out1, w1, w2, layer=1, final=True)
        return (y.reshape(bsz, ls, d), sff[:, None], sbf[:, None],
                hfl.reshape(bsz, 1, D_RNN), hbf.reshape(bsz, 1, D_RNN))

    bp = x_prompt.shape[0]
    zero_lru = jnp.zeros((bp, D_RNN), f32)
    y_prompt, new_ret_fwd, new_ret_bwd, new_lru_fwd, new_lru_bwd = trunk(
        x_prompt, slice(0, 1), None, None, zero_lru, zero_lru, False)
    y_sample = trunk(x_sample, slice(1, 1 + n_lat), state_ret_fwd[:, 0], state_ret_bwd[:, 0],
                     state_lru_fwd[:, 0], state_lru_bwd[:, 0], True)[0]
    return (y_prompt, y_sample, new_ret_fwd, new_ret_bwd, new_lru_fwd, new_lru_bwd)
```

```python
import functools

import jax
import jax.numpy as jnp
import numpy as np
from jax import lax
from jax.experimental import pallas as pl
from jax.experimental.pallas import tpu as pltpu

f32 = jnp.float32
bf16 = jnp.bfloat16

D_MODEL = 1024
DEPTH = 2
GRID_W = 64
EPS = 1e-6
RET_HEADS = 4
RET_DK = 128
RET_DV = 128
RET_CHUNK = 128
ROPE_BASE = 10000.0
GMLP_GROUPS = 4
GMLP_CH = 128
AB_IN = 3072
AB_OUT = 1024
D_RNN = D_MODEL
LRU_BLOCKS = 8
LRU_BS = 128
LRU_C = 8.0
D_FF = 4 * D_MODEL

LANES = 128
SUBLANES = 8
MIB = 1024 * 1024

TOK_TILE = 512
COND_ROWS = 8
MOD_TN = 1536
LRU_SEG = 256
LRU_PITCH = LRU_SEG + SUBLANES
FF_CHUNK = 1024


def _cparams(sem, vmem_mib):
    return pltpu.CompilerParams(dimension_semantics=sem, vmem_limit_bytes=vmem_mib * MIB)


def _rms(x, gain):
    ms = jnp.mean(x * x, axis=-1, keepdims=True)
    return x * lax.rsqrt(ms + EPS) * gain


def _mod_kernel(cond_ref, w_ref, b_ref, o_ref):
    c = cond_ref[...]
    s = c * jax.nn.sigmoid(c)
    o_ref[...] = jnp.dot(s.astype(bf16), w_ref[...].astype(bf16), preferred_element_type=f32) + b_ref[...]


def _mod_call(conds, mod_w, mod_b):
    depth, d, n = mod_w.shape
    return pl.pallas_call(
        _mod_kernel,
        grid=(depth, n // MOD_TN),
        in_specs=[
            pl.BlockSpec((COND_ROWS, d), lambda l, j: (0, 0)),
            pl.BlockSpec((None, d, MOD_TN), lambda l, j: (l, 0, j)),
            pl.BlockSpec((None, 1, MOD_TN), lambda l, j: (l, 0, j)),
        ],
        out_specs=pl.BlockSpec((None, COND_ROWS, MOD_TN), lambda l, j: (l, 0, j)),
        out_shape=jax.ShapeDtypeStruct((depth, COND_ROWS, n), f32),
        compiler_params=_cparams(("parallel", "parallel"), 32),
        name="mod",
    )(conds, mod_w, mod_b.reshape(depth, 1, n))


def _rope(t, cos, sin_signed, first_half):
    up = pltpu.roll(t, LANES - 32, axis=1)
    dn = pltpu.roll(t, 32, axis=1)
    return t * cos + jnp.where(first_half, up, dn) * sin_signed


def _inproj_kernel(*refs, retention, rope):
    if rope:
        x_ref, sh_ref, sc_ref, gain_ref, w_ref, cos_ref, sin_ref, o_ref = refs
    else:
        x_ref, sh_ref, sc_ref, gain_ref, w_ref, o_ref = refs
    h = _rms(x_ref[...], gain_ref[...]) * (1.0 + sc_ref[...]) + sh_ref[...]
    hb = h.astype(bf16)
    n = w_ref.shape[1]
    if not retention:
        for j in range(0, n, 512):
            o_ref[:, j:j + 512] = jnp.dot(hb, w_ref[:, j:j + 512], preferred_element_type=f32).astype(bf16)
        return
    qk = RET_HEADS * RET_DK
    if rope:
        cos = cos_ref[...]
        sin = sin_ref[...]
        lane = lax.broadcasted_iota(jnp.int32, cos.shape, 1)
        first_half = (lane % 64) < 32
    for part in range(2):
        y = jnp.dot(hb, w_ref[:, part * qk:(part + 1) * qk], preferred_element_type=f32)
        for hh in range(RET_HEADS):
            t = y[:, hh * RET_DK:(hh + 1) * RET_DK]
            if part == 1:
                t = t * (RET_DK ** -0.5)
            if rope:
                t = _rope(t, cos, sin, first_half)
            c0 = part * qk + hh * RET_DK
            o_ref[:, c0:c0 + RET_DK] = t.astype(bf16)
    for j in range(2 * qk, n, 512):
        o_ref[:, j:j + 512] = jnp.dot(hb, w_ref[:, j:j + 512], preferred_element_type=f32).astype(bf16)


def _inproj_call(x3, modl, gain, w, rope_tabs, *, retention, sh_col, sc_col):
    bg, lg, d = x3.shape
    n = w.shape[1]
    t = TOK_TILE
    rope = rope_tabs is not None
    in_specs = [
        pl.BlockSpec((None, t, d), lambda b, i: (b, i, 0)),
        pl.BlockSpec((None, 1, d), lambda b, i: (b, 0, sh_col)),
        pl.BlockSpec((None, 1, d), lambda b, i: (b, 0, sc_col)),
        pl.BlockSpec((1, d), lambda b, i: (0, 0)),
        pl.BlockSpec((d, n), lambda b, i: (0, 0)),
    ]
    args = [x3, modl, modl, gain.reshape(1, d), w]
    if rope:
        in_specs += [pl.BlockSpec((t, LANES), lambda b, i: (i, 0))] * 2
        args += list(rope_tabs)
    return pl.pallas_call(
        functools.partial(_inproj_kernel, retention=retention, rope=rope),
        grid=(bg, lg // t),
        in_specs=in_specs,
        out_specs=pl.BlockSpec((None, t, n), lambda b, i: (b, i, 0)),
        out_shape=jax.ShapeDtypeStruct((bg, lg, n), bf16),
        compiler_params=_cparams(("parallel", "parallel"), 48),
        name="inproj_ret" if retention else "inproj_lru",
    )(*args)


def _row_iota():
    return lax.broadcasted_iota(jnp.int32, (RET_CHUNK, LANES), 0).astype(f32)


def _retstate_kernel(*refs, tc, zero_init):
    if zero_init:
        dec_ref, kf_ref, vf_ref, kb_ref, vb_ref, sfp_ref, sbn_ref, sff_ref, sbf_ref, sf_scr, sb_scr = refs
    else:
        (dec_ref, kf_ref, vf_ref, kb_ref, vb_ref, s0f_ref, s0b_ref,
         sfp_ref, sbn_ref, sff_ref, sbf_ref, sf_scr, sb_scr) = refs
    i = pl.program_id(1)

    @pl.when(i == 0)
    def _():
        if zero_init:
            sf_scr[...] = jnp.zeros(sf_scr.shape, f32)
            sb_scr[...] = jnp.zeros(sb_scr.shape, f32)
        else:
            sf_scr[...] = s0f_ref[...]
            sb_scr[...] = s0b_ref[...]

    row = _row_iota()
    c = float(RET_CHUNK)
    for hh in range(RET_HEADS):
        lg_f = jax.nn.log_sigmoid(dec_ref[0, hh])
        lg_b = jax.nn.log_sigmoid(dec_ref[1, hh])
        kd_f = jnp.exp(lg_f * (c - 1.0 - row))
        kd_b = jnp.exp(lg_b * row)
        gc_f = jnp.exp(lg_f * c)
        gc_b = jnp.exp(lg_b * c)
        cs = slice(hh * RET_DK, (hh + 1) * RET_DK)
        for cc in range(tc):
            rs = slice(cc * RET_CHUNK, (cc + 1) * RET_CHUNK)
            s_prev = sf_scr[hh]
            sfp_ref[cc, hh] = s_prev.astype(bf16)
            kd = (kf_ref[rs, cs].astype(f32) * kd_f).astype(bf16)
            kv = lax.dot_general(kd, vf_ref[rs, cs], (((0,), (0,)), ((), ())), preferred_element_type=f32)
            sf_scr[hh] = gc_f * s_prev + kv
        for cc in reversed(range(tc)):
            rs = slice(cc * RET_CHUNK, (cc + 1) * RET_CHUNK)
            s_next = sb_scr[hh]
            sbn_ref[cc, hh] = s_next.astype(bf16)
            kd = (kb_ref[rs, cs].astype(f32) * kd_b).astype(bf16)
            kv = lax.dot_general(kd, vb_ref[rs, cs], (((0,), (0,)), ((), ())), preferred_element_type=f32)
            sb_scr[hh] = gc_b * s_next + kv

    @pl.when(i == pl.num_programs(1) - 1)
    def _():
        sff_ref[...] = sf_scr[...]
        sbf_ref[...] = sb_scr[...]


def _retstate_call(qkv, dec, s0f, s0b, tc):
    bs, ls, _ = qkv.shape
    nc = ls // RET_CHUNK
    t = tc * RET_CHUNK
    ns = nc // tc
    hw = RET_HEADS * RET_DK
    st = (RET_HEADS, RET_DK, RET_DV)
    zero_init = s0f is None
    in_specs = [
        pl.BlockSpec((2, RET_HEADS, 1, LANES), lambda b, i: (0, 0, 0, 0)),
        pl.BlockSpec((None, t, hw), lambda b, i: (b, i, 1)),
        pl.BlockSpec((None, t, hw), lambda b, i: (b, i, 2)),
        pl.BlockSpec((None, t, hw), lambda b, i: (b, ns - 1 - i, 1)),
        pl.BlockSpec((None, t, hw), lambda b, i: (b, ns - 1 - i, 2)),
    ]
    args = [dec, qkv, qkv, qkv, qkv]
    if not zero_init:
        in_specs += [pl.BlockSpec((None,) + st, lambda b, i: (b, 0, 0, 0))] * 2
        args += [s0f, s0b]
    return pl.pallas_call(
        functools.partial(_retstate_kernel, tc=tc, zero_init=zero_init),
        grid=(bs, ns),
        in_specs=in_specs,
        out_specs=[
            pl.BlockSpec((None, tc) + st, lambda b, i: (b, i, 0, 0, 0)),
            pl.BlockSpec((None, tc) + st, lambda b, i: (b, ns - 1 - i, 0, 0, 0)),
            pl.BlockSpec((None,) + st, lambda b, i: (b, 0, 0, 0)),
            pl.BlockSpec((None,) + st, lambda b, i: (b, 0, 0, 0)),
        ],
        out_shape=[
            jax.ShapeDtypeStruct((bs, nc) + st, bf16),
            jax.ShapeDtypeStruct((bs, nc) + st, bf16),
            jax.ShapeDtypeStruct((bs,) + st, f32),
            jax.ShapeDtypeStruct((bs,) + st, f32),
        ],
        scratch_shapes=[pltpu.VMEM(st, f32), pltpu.VMEM(st, f32)],
        compiler_params=_cparams(("parallel", "arbitrary"), 32),
        name="retstate",
    )(*args)


def _mix0_kernel(dec_ref, q_ref, k_ref, v_ref, g_ref, u_ref, vv_ref, sfp_ref, sbn_ref, ws_ref, bs_ref,
                 o_ref, *, tc):
    row = _row_iota()
    col = lax.broadcasted_iota(jnp.int32, (RET_CHUNK, LANES), 1).astype(f32)
    c = float(RET_CHUNK)
    nt = (((1,), (1,)), ((), ()))
    for hh in range(RET_HEADS):
        lg_f = jax.nn.log_sigmoid(dec_ref[0, hh])
        lg_b = jax.nn.log_sigmoid(dec_ref[1, hh])
        decay = jnp.where(row >= col,
                          jnp.exp(lg_f * jnp.maximum(row - col, 0.0)),
                          jnp.exp(lg_b * jnp.maximum(col - row - 1.0, 0.0)))
        qd_f = jnp.exp(lg_f * (row + 1.0))
        qd_b = jnp.exp(lg_b * (c - 1.0 - row))
        cs = slice(hh * RET_DK, (hh + 1) * RET_DK)
        for cc in range(tc):
            rs = slice(cc * RET_CHUNK, (cc + 1) * RET_CHUNK)
            q = q_ref[rs, cs]
            qf = q.astype(f32)
            s = lax.dot_general(q, k_ref[rs, cs], nt, preferred_element_type=f32)
            o = jnp.dot((s * decay).astype(bf16), v_ref[rs, cs], preferred_element_type=f32)
            o = o + jnp.dot((qf * qd_f).astype(bf16), sfp_ref[cc, hh], preferred_element_type=f32)
            o = o + jnp.dot((qf * qd_b).astype(bf16), sbn_ref[cc, hh], preferred_element_type=f32)
            mu = jnp.mean(o, axis=-1, keepdims=True)
            var = jnp.mean(jnp.square(o - mu), axis=-1, keepdims=True)
            on = (o - mu) * lax.rsqrt(var + EPS)
            gate = g_ref[rs, cs].astype(f32)
            o_ref[rs, cs] = (gate * jax.nn.sigmoid(gate) * on).astype(bf16)
    off = RET_HEADS * RET_DV
    for gg in range(GMLP_GROUPS):
        cs = slice(gg * GMLP_CH, (gg + 1) * GMLP_CH)
        w = ws_ref[gg]
        bias = bs_ref[gg]
        for cc in range(tc):
            rs = slice(cc * RET_CHUNK, (cc + 1) * RET_CHUNK)
            gu = jax.nn.gelu(u_ref[rs, cs].astype(f32))
            gv = jax.nn.gelu(vv_ref[rs, cs].astype(f32))
            gv = gv * lax.rsqrt(jnp.mean(gv * gv, axis=-1, keepdims=True) + EPS)
            sp = jnp.dot(w, gv.astype(bf16), preferred_element_type=f32) + bias
            o_ref[rs, off + gg * GMLP_CH: off + (gg + 1) * GMLP_CH] = (gu * sp).astype(bf16)


def _mix0_call(qkv, dec, sfp, sbn, ws, bs, tc):
    bsz, ls, _ = qkv.shape
    t = tc * RET_CHUNK
    hw = RET_HEADS * RET_DK
    st = (RET_HEADS, RET_DK, RET_DV)
    col_spec = lambda j: pl.BlockSpec((None, t, hw), lambda b, i, j=j: (b, i, j))
    return pl.pallas_call(
        functools.partial(_mix0_kernel, tc=tc),
        grid=(bsz, ls // t),
        in_specs=[
            pl.BlockSpec((2, RET_HEADS, 1, LANES), lambda b, i: (0, 0, 0, 0)),
            col_spec(0), col_spec(1), col_spec(2), col_spec(3), col_spec(4), col_spec(5),
            pl.BlockSpec((None, tc) + st, lambda b, i: (b, i, 0, 0, 0)),
            pl.BlockSpec((None, tc) + st, lambda b, i: (b, i, 0, 0, 0)),
            pl.BlockSpec((GMLP_GROUPS, RET_CHUNK, RET_CHUNK), lambda b, i: (0, 0, 0)),
            pl.BlockSpec((GMLP_GROUPS, RET_CHUNK, 1), lambda b, i: (0, 0, 0)),
        ],
        out_specs=pl.BlockSpec((None, t, AB_OUT), lambda b, i: (b, i, 0)),
        out_shape=jax.ShapeDtypeStruct((bsz, ls, AB_OUT), bf16),
        compiler_params=_cparams(("parallel", "parallel"), 32),
        name="mix0",
    )(dec, qkv, qkv, qkv, qkv, qkv, qkv, sfp, sbn, ws, bs)


def _loop(n, body, carry):
    for i in range(n):
        carry = body(i, carry)
    return carry


def _lru_body(gate_ref, xin_ref, cw_ref, cb_ref, wg_ref, bg_ref, lam_ref, h0f_ref, h0b_ref,
              o_ref, hfl_ref, hbf_ref, xs, xt, af, bfs, ab, bbs, *, nseg, chained):
    seg, pitch = LRU_SEG, LRU_PITCH
    blk = seg

    def stage(s, carry):
        xs[pl.ds(s * pitch, seg), :] = xin_ref[pl.ds(s * seg, seg), :].astype(f32)
        return carry

    _loop(nseg, stage, 0)

    def tile(j):
        return pl.ds(j * nseg, nseg)

    def gather(j, carry):
        xt[tile(j + 2), :] = xs[pl.ds(j, nseg, stride=pitch), :]
        return carry

    _loop(seg, gather, 0)
    if chained:
        srow = lax.broadcasted_iota(jnp.int32, (nseg, LANES), 0)
        xt[tile(0), :] = jnp.where(srow == 0, 0.0, pltpu.roll(xt[tile(seg), :], 1, axis=0))
        xt[tile(1), :] = jnp.where(srow == 0, 0.0, pltpu.roll(xt[tile(seg + 1), :], 1, axis=0))
        xt[tile(seg + 2), :] = jnp.where(srow == nseg - 1, 0.0, pltpu.roll(xt[tile(2), :], nseg - 1, axis=0))
    else:
        zt = jnp.zeros((nseg, LANES), f32)
        xt[tile(0), :] = zt
        xt[tile(1), :] = zt
        xt[tile(seg + 2), :] = zt

    cw = cw_ref[...]
    cb = cb_ref[...]
    bg = bg_ref[...]
    sp4 = (0.5 * LRU_C) * jax.nn.softplus(-lam_ref[...])

    def gates(jb, carry):
        p0 = jb * blk
        xm2 = xt[pl.ds(p0, blk), :]
        xm1 = xt[pl.ds(p0 + nseg, blk), :]
        x0 = xt[pl.ds(p0 + 2 * nseg, blk), :]
        xp1 = xt[pl.ds(p0 + 3 * nseg, blk), :]
        xc = cb + xm2 * cw[0:1] + xm1 * cw[1:2] + x0 * cw[2:3] + xp1 * cw[3:4]
        gt = jnp.dot(xc.astype(bf16), wg_ref[...], preferred_element_type=f32) + bg
        xch = 0.5 * xc
        for d, (a_s, b_s) in enumerate(((af, bfs), (ab, bbs))):
            tr = jnp.tanh(gt[:, (2 * d) * LANES:(2 * d + 1) * LANES])
            ti = jnp.tanh(gt[:, (2 * d + 1) * LANES:(2 * d + 2) * LANES])
            nla = (tr + 1.0) * sp4[d:d + 1]
            a = jnp.exp(-nla)
            m2 = jnp.tanh(nla) * (1.0 + a * a)
            mult = m2 * lax.rsqrt(jnp.maximum(m2, 1e-30))
            a_s[pl.ds(p0, blk), :] = a
            b_s[pl.ds(p0, blk), :] = mult * ((ti + 1.0) * xch)
        return carry

    _loop(nseg, gates, 0)

    if chained:
        def ends(j, carry):
            hf, pf, hb, pb = carry
            a = af[tile(j), :]
            hf = a * hf + bfs[tile(j), :]
            pf = a * pf
            a2 = ab[tile(seg - 1 - j), :]
            hb = a2 * hb + bbs[tile(seg - 1 - j), :]
            pb = a2 * pb
            return hf, pf, hb, pb

        zero = jnp.zeros((nseg, LANES), f32)
        one = jnp.ones((nseg, LANES), f32)
        hf_e, pf_e, hb_e, pb_e = _loop(seg, ends, (zero, one, zero, one))
        cf = [h0f_ref[0:1, :]]
        for s in range(1, nseg):
            cf.append(hf_e[s - 1:s] + pf_e[s - 1:s] * cf[-1])
        cbk = [h0b_ref[0:1, :]]
        for s in range(nseg - 2, -1, -1):
            cbk.append(hb_e[s + 1:s + 2] + pb_e[s + 1:s + 2] * cbk[-1])
        hf0 = jnp.concatenate(cf, axis=0)
        hb0 = jnp.concatenate(cbk[::-1], axis=0)
    else:
        hf0 = h0f_ref[...]
        hb0 = h0b_ref[...]

    def scan_b(k, hb):
        j = seg - 1 - k
        hb = ab[tile(j), :] * hb + bbs[tile(j), :]
        xt[tile(j), :] = hb
        return hb

    hb_l = _loop(seg, scan_b, hb0)

    def scan_f(j, hf):
        hf = af[tile(j), :] * hf + bfs[tile(j), :]
        xs[pl.ds(j, nseg, stride=pitch), :] = hf + xt[tile(j), :]
        return hf

    hf_l = _loop(seg, scan_f, hf0)
    if chained:
        hfl_ref[...] = hf_l[nseg - 1:nseg]
        hbf_ref[...] = hb_l[0:1]
    else:
        hfl_ref[...] = hf_l
        hbf_ref[...] = hb_l

    def combine(s, carry):
        g = jax.nn.gelu(gate_ref[pl.ds(s * seg, seg), :].astype(f32))
        o_ref[pl.ds(s * seg, seg), :] = (g * xs[pl.ds(s * pitch, seg), :]).astype(bf16)
        return carry

    _loop(nseg, combine, 0)


def _lru_specs(ntok, nseg, gc, pair):
    rows = LRU_SEG * nseg
    nt = ntok // rows
    nb = LRU_BLOCKS
    in_specs = [
        pl.BlockSpec((rows, LANES), lambda *g: pair(*g)),
        pl.BlockSpec((rows, LANES), lambda *g: (pair(*g)[0], nb + pair(*g)[1])),
        pl.BlockSpec((4, LANES), lambda *g: (0, pair(*g)[1])),
        pl.BlockSpec((1, LANES), lambda *g: (0, pair(*g)[1])),
        pl.BlockSpec((None, LANES, 4 * LANES), lambda *g: (pair(*g)[1], 0, 0)),
        pl.BlockSpec((None, 1, 4 * LANES), lambda *g: (pair(*g)[1], 0, 0)),
        pl.BlockSpec((2, LANES), lambda *g: (0, pair(*g)[1])),
        pl.BlockSpec((None, gc, LANES), lambda *g: (pair(*g)[0], 0, pair(*g)[1])),
        pl.BlockSpec((None, gc, LANES), lambda *g: (pair(*g)[0], 0, pair(*g)[1])),
    ]
    out_specs = [
        pl.BlockSpec((rows, LANES), lambda *g: pair(*g)),
        pl.BlockSpec((None, gc, LANES), lambda *g: (pair(*g)[0], 0, pair(*g)[1])),
        pl.BlockSpec((None, gc, LANES), lambda *g: (pair(*g)[0], 0, pair(*g)[1])),
    ]
    out_shape = [
        jax.ShapeDtypeStruct((ntok, D_RNN), bf16),
        jax.ShapeDtypeStruct((nt, gc, D_RNN), f32),
        jax.ShapeDtypeStruct((nt, gc, D_RNN), f32),
    ]
    scr = pltpu.VMEM((rows, LANES), f32)
    scratch = [pltpu.VMEM((nseg * LRU_PITCH, LANES), f32), pltpu.VMEM((rows + 3 * nseg, LANES), f32),
               scr, scr, scr, scr]
    return in_specs, out_specs, out_shape, scratch


def _outmlp_body(x_ref, mix_ref, g1_ref, sh2_ref, sc2_ref, g2_ref, gain_ref, fgain_ref,
                 wo_ref, w1_ref, w2_ref, o_ref, *, final):
    mix = jnp.dot(mix_ref[...], wo_ref[...], preferred_element_type=f32)
    x1 = x_ref[...] + g1_ref[...] * mix
    h = (_rms(x1, gain_ref[...]) * (1.0 + sc2_ref[...]) + sh2_ref[...]).astype(bf16)
    acc = None
    for j in range(0, D_FF, FF_CHUNK):
        hid = jnp.dot(h, w1_ref[:, j:j + FF_CHUNK], preferred_element_type=f32)
        hid = jnp.square(jnp.maximum(hid, 0.0)).astype(bf16)
        part = jnp.dot(hid, w2_ref[j:j + FF_CHUNK, :], preferred_element_type=f32)
        acc = part if acc is None else acc + part
    x2 = x1 + g2_ref[...] * acc
    if final:
        x2 = _rms(x2, fgain_ref[...])
    o_ref[...] = x2


N_MLP_IN = 11
N_LRU_IN = 9
N_LRU_OUT = 3


def _outmlp_kernel(*refs, final, lru):
    if lru is None:
        _outmlp_body(*refs, final=final)
        return
    mlp_in = refs[:N_MLP_IN]
    lru_in = refs[N_MLP_IN:N_MLP_IN + N_LRU_IN]
    outs = refs[N_MLP_IN + N_LRU_IN:]
    _lru_body(*lru_in, *outs[1:1 + N_LRU_OUT], *outs[1 + N_LRU_OUT:], nseg=lru[0], chained=lru[1])
    _outmlp_body(*mlp_in, outs[0], final=final)


def _outmlp_call(x3, mix3, modl, gain, fgain, wo, w1, w2, *, layer, final, t=TOK_TILE, lru_args=None):
    bg, lg, d = x3.shape
    steps = lg // t
    dm = mix3.shape[-1]
    mod_spec = lambda k: pl.BlockSpec((None, 1, d), lambda b, i, k=k: (b, 0, k))

    def resident(shape, lead=None):
        if lead is None:
            return pl.BlockSpec(shape, lambda b, i: (0, 0), pipeline_mode=pl.Buffered(1))
        return pl.BlockSpec((None,) + shape, lambda b, i: (lead, 0, 0), pipeline_mode=pl.Buffered(1))

    in_specs = [
        pl.BlockSpec((None, t, d), lambda b, i: (b, i, 0)),
        pl.BlockSpec((None, t, dm), lambda b, i: (b, i, 0)),
        mod_spec(2), mod_spec(3), mod_spec(4), mod_spec(5),
        pl.BlockSpec((1, d), lambda b, i: (0, 0)),
        pl.BlockSpec((1, d), lambda b, i: (0, 0)),
        resident((dm, d)), resident((d, D_FF), layer), resident((D_FF, d), layer),
    ]
    args = [x3, mix3, modl, modl, modl, modl, gain.reshape(1, d), fgain.reshape(1, d), wo, w1, w2]
    out_specs = [pl.BlockSpec((None, t, d), lambda b, i: (b, i, 0))]
    out_shape = [jax.ShapeDtypeStruct((bg, lg, d), f32)]
    scratch = []
    lru = None
    if lru_args is not None:
        gx, cw, cb, wg, bgate, lam, h0f, h0b, nseg, chained = lru_args
        ntok = gx.shape[0]
        assert (ntok // (LRU_SEG * nseg)) * LRU_BLOCKS == bg * steps, "one RG-LRU (tile, block) pair per MLP step"

        def pair(b, i):
            p = b * steps + i
            return p // LRU_BLOCKS, p % LRU_BLOCKS

        li, lo, ls, scratch = _lru_specs(ntok, nseg, h0f.shape[1], pair)
        in_specs += li
        out_specs += lo
        out_shape += ls
        args += [gx, gx, cw, cb, wg, bgate, lam, h0f, h0b]
        lru = (nseg, chained)
    res = pl.pallas_call(
        functools.partial(_outmlp_kernel, final=final, lru=lru),
        grid=(bg, steps),
        in_specs=in_specs,
        out_specs=out_specs,
        out_shape=out_shape,
        scratch_shapes=scratch,
        compiler_params=_cparams(("parallel", "parallel"), 56),
        name=("outmlp_final" if final else "outmlp") + ("_rglru" if lru else ""),
    )(*args)
    return res if lru else res[0]


@functools.lru_cache(maxsize=None)
def _rope_tables(length):
    pos = np.arange(length)
    rowp = (pos // GRID_W).astype(np.float64)
    colp = (pos % GRID_W).astype(np.float64)
    nf = RET_DK // 4
    inv_freq = ROPE_BASE ** (-np.arange(nf, dtype=np.float64) / nf)
    ar = rowp[:, None] * inv_freq[None, :]
    ac = colp[:, None] * inv_freq[None, :]
    cos = np.concatenate([np.cos(ar), np.cos(ar), np.cos(ac), np.cos(ac)], axis=-1).astype(np.float32)
    sin = np.concatenate([-np.sin(ar), np.sin(ar), -np.sin(ac), np.sin(ac)], axis=-1).astype(np.float32)
    return cos, sin


def kernel(x_prompt, x_sample, c, c_ctx, state_ret_fwd, state_ret_bwd, state_lru_fwd, state_lru_bwd, mod_w, mod_b, norm_mix, norm_mlp, mlp_w1, mlp_w2, ab_w_in, ab_w_out, ret_decay_fwd, ret_decay_bwd, gmlp_ws, gmlp_bs, lru_w_in, lru_conv_w, lru_conv_b, lru_gate_a_w, lru_gate_a_b, lru_gate_x_w, lru_gate_x_b, lru_lambda, lru_w_out, final_norm):
    d = D_MODEL
    n_lat = c.shape[0]
    conds = jnp.concatenate([c_ctx[None, :], c, jnp.zeros((COND_ROWS - 1 - n_lat, d), f32)], axis=0)
    mod = _mod_call(conds, mod_w, mod_b)

    w_in0 = ab_w_in[0].astype(bf16)
    w_out0 = ab_w_out[0].astype(bf16)
    w1 = mlp_w1.astype(bf16)
    w2 = mlp_w2.astype(bf16)
    w_in1 = lru_w_in[0].astype(bf16)
    w_out1 = lru_w_out[0].astype(bf16)
    ws = gmlp_ws[0].astype(bf16)
    bs = gmlp_bs[0][:, :, None]
    dec = jnp.broadcast_to(jnp.stack([ret_decay_fwd[0], ret_decay_bwd[0]])[:, :, None, None],
                           (2, RET_HEADS, 1, LANES))
    wg = (0.5 * jnp.concatenate([lru_gate_a_w[0, 0], lru_gate_x_w[0, 0], lru_gate_a_w[0, 1], lru_gate_x_w[0, 1]],
                                axis=-1)).astype(bf16)
    bgate = 0.5 * jnp.stack([lru_gate_a_b[0, 0], lru_gate_x_b[0, 0], lru_gate_a_b[0, 1], lru_gate_x_b[0, 1]], axis=0)
    bgate = bgate.reshape(4, LRU_BLOCKS, LRU_BS).transpose(1, 0, 2).reshape(LRU_BLOCKS, 1, 4 * LRU_BS)
    lru_w = (lru_conv_w[0], lru_conv_b[0].reshape(1, D_RNN), wg, bgate, lru_lambda[0])

    def mixer0(x3, modl, bsz, ls, ret_f0, ret_b0, latent):
        tc = min(4, ls // RET_CHUNK)
        qkv = _inproj_call(x3, modl, norm_mix[0], w_in0, _rope_tables(ls) if latent else None,
                           retention=True, sh_col=0, sc_col=1)
        qkv = qkv.reshape(bsz, ls, AB_IN)
        sfp, sbn, sff, sbf = _retstate_call(qkv, dec, ret_f0, ret_b0, tc)
        mix = _mix0_call(qkv, dec, sfp, sbn, ws, bs, tc)
        return mix.reshape(x3.shape[0], x3.shape[1], AB_OUT), sff, sbf

    def lru_inproj(x3, modl):
        gx = _inproj_call(x3, modl, norm_mix[1], w_in1, None, retention=False, sh_col=0, sc_col=1)
        return gx.reshape(-1, 2 * D_RNN)

    bp, lp, _ = x_prompt.shape
    bl, ll, _ = x_sample.shape
    assert lp == LRU_SEG and ll % LRU_SEG == 0
    xc = x_prompt.reshape(1, bp * lp, d)
    xl = x_sample
    mod_c = [mod[l, 0:1][:, None, :] for l in range(DEPTH)]
    mod_l = [mod[l, 1:1 + n_lat][:, None, :] for l in range(DEPTH)]

    mix_c, sff, sbf = mixer0(xc, mod_c[0], bp, lp, None, None, False)
    xc = _outmlp_call(xc, mix_c, mod_c[0], norm_mlp[0], final_norm, w_out0, w1, w2, layer=0, final=False)
    gx_c = lru_inproj(xc, mod_c[1])
    mix_l, _, _ = mixer0(xl, mod_l[0], bl, ll, state_ret_fwd[:, 0], state_ret_bwd[:, 0], True)

    steps_l = bl * (ll // TOK_TILE)
    nseg_c = (bp * lp // LRU_SEG) * LRU_BLOCKS // steps_l
    zero_lru = jnp.zeros((bp // nseg_c, nseg_c, D_RNN), f32)
    xl, pre_c, hfl, hbf = _outmlp_call(
        xl, mix_l, mod_l[0], norm_mlp[0], final_norm, w_out0, w1, w2, layer=0, final=False,
        lru_args=(gx_c,) + lru_w + (zero_lru, zero_lru, nseg_c, False))
    gx_l = lru_inproj(xl, mod_l[1])

    nseg_l = ll // LRU_SEG
    t_c = bp * lp // (bl * LRU_BLOCKS)
    y_c, pre_l, _, _ = _outmlp_call(
        xc, pre_c.reshape(1, bp * lp, D_RNN), mod_c[1], norm_mlp[1], final_norm, w_out1, w1, w2,
        layer=1, final=True, t=t_c,
        lru_args=(gx_l,) + lru_w + (state_lru_fwd[:, 0].reshape(bl, 1, D_RNN),
                                    state_lru_bwd[:, 0].reshape(bl, 1, D_RNN), nseg_l, True))
    y_l = _outmlp_call(xl, pre_l.reshape(bl, ll, D_RNN), mod_l[1], norm_mlp[1], final_norm, w_out1, w1, w2,
                       layer=1, final=True)
    return (y_c.reshape(bp, lp, d), y_l, sff[:, None], sbf[:, None],
            hfl.reshape(bp, 1, D_RNN), hbf.reshape(bp, 1, D_RNN))
```

```python
import functools

import jax
import jax.numpy as jnp
import numpy as np
from jax import lax
from jax.experimental import pallas as pl
from jax.experimental.pallas import tpu as pltpu

f32 = jnp.float32
bf16 = jnp.bfloat16

D_MODEL = 1024
DEPTH = 2
GRID_W = 64
EPS = 1e-6
RET_HEADS = 4
RET_DK = 128
RET_DV = 128
RET_CHUNK = 128
ROPE_BASE = 10000.0
GMLP_GROUPS = 4
GMLP_CH = 128
AB_IN = 3072
AB_OUT = 1024
D_RNN = D_MODEL
LRU_BLOCKS = 8
LRU_BS = 128
LRU_C = 8.0
D_FF = 4 * D_MODEL

LANES = 128
SUBLANES = 8
MIB = 1024 * 1024

TOK_TILE = 512
INPROJ_TILE = 1024
COND_ROWS = 8
MOD_TN = 1536
LRU_SEG = 256
LRU_PITCH = LRU_SEG + SUBLANES
FF_CHUNK = 1024


def _cparams(sem, vmem_mib):
    return pltpu.CompilerParams(dimension_semantics=sem, vmem_limit_bytes=vmem_mib * MIB)


def _rms(x, gain):
    ms = jnp.mean(x * x, axis=-1, keepdims=True)
    return x * lax.rsqrt(ms + EPS) * gain


def _mod_kernel(cond_ref, w_ref, b_ref, o_ref):
    c = cond_ref[...]
    s = c * jax.nn.sigmoid(c)
    o_ref[...] = jnp.dot(s.astype(bf16), w_ref[...].astype(bf16), preferred_element_type=f32) + b_ref[...]


def _mod_call(conds, mod_w, mod_b):
    depth, d, n = mod_w.shape
    return pl.pallas_call(
        _mod_kernel,
        grid=(depth, n // MOD_TN),
        in_specs=[
            pl.BlockSpec((COND_ROWS, d), lambda l, j: (0, 0)),
            pl.BlockSpec((None, d, MOD_TN), lambda l, j: (l, 0, j)),
            pl.BlockSpec((None, 1, MOD_TN), lambda l, j: (l, 0, j)),
        ],
        out_specs=pl.BlockSpec((None, COND_ROWS, MOD_TN), lambda l, j: (l, 0, j)),
        out_shape=jax.ShapeDtypeStruct((depth, COND_ROWS, n), f32),
        compiler_params=_cparams(("parallel", "parallel"), 32),
        name="mod",
    )(conds, mod_w, mod_b.reshape(depth, 1, n))


def _rope(t, cos, sin_signed, first_half):
    up = pltpu.roll(t, LANES - 32, axis=1)
    dn = pltpu.roll(t, 32, axis=1)
    return t * cos + jnp.where(first_half, up, dn) * sin_signed


def _inproj_kernel(*refs, retention, rope):
    if rope:
        x_ref, sh_ref, sc_ref, gain_ref, w_ref, cos_ref, sin_ref, o_ref = refs
    else:
        x_ref, sh_ref, sc_ref, gain_ref, w_ref, o_ref = refs
    h = _rms(x_ref[...], gain_ref[...]) * (1.0 + sc_ref[...]) + sh_ref[...]
    hb = h.astype(bf16)
    n = w_ref.shape[1]
    if not retention:
        for j in range(0, n, 512):
            o_ref[:, j:j + 512] = jnp.dot(hb, w_ref[:, j:j + 512], preferred_element_type=f32).astype(bf16)
        return
    qk = RET_HEADS * RET_DK
    if rope:
        cos = cos_ref[...]
        sin = sin_ref[...]
        lane = lax.broadcasted_iota(jnp.int32, cos.shape, 1)
        first_half = (lane % 64) < 32
    for part in range(2):
        y = jnp.dot(hb, w_ref[:, part * qk:(part + 1) * qk], preferred_element_type=f32)
        for hh in range(RET_HEADS):
            t = y[:, hh * RET_DK:(hh + 1) * RET_DK]
            if part == 1:
                t = t * (RET_DK ** -0.5)
            if rope:
                t = _rope(t, cos, sin, first_half)
            c0 = part * qk + hh * RET_DK
            o_ref[:, c0:c0 + RET_DK] = t.astype(bf16)
    for j in range(2 * qk, n, 512):
        o_ref[:, j:j + 512] = jnp.dot(hb, w_ref[:, j:j + 512], preferred_element_type=f32).astype(bf16)


def _inproj_call(x3, modl, gain, w, rope_tabs, *, retention, sh_col, sc_col):
    bg, lg, d = x3.shape
    n = w.shape[1]
    t = INPROJ_TILE
    rope = rope_tabs is not None
    in_specs = [
        pl.BlockSpec((None, t, d), lambda b, i: (b, i, 0)),
        pl.BlockSpec((None, 1, d), lambda b, i: (b, 0, sh_col)),
        pl.BlockSpec((None, 1, d), lambda b, i: (b, 0, sc_col)),
        pl.BlockSpec((1, d), lambda b, i: (0, 0)),
        pl.BlockSpec((d, n), lambda b, i: (0, 0), pipeline_mode=pl.Buffered(1)),
    ]
    args = [x3, modl, modl, gain.reshape(1, d), w]
    if rope:
        in_specs += [pl.BlockSpec((t, LANES), lambda b, i: (i, 0))] * 2
        args += list(rope_tabs)
    return pl.pallas_call(
        functools.partial(_inproj_kernel, retention=retention, rope=rope),
        grid=(bg, lg // t),
        in_specs=in_specs,
        out_specs=pl.BlockSpec((None, t, n), lambda b, i: (b, i, 0)),
        out_shape=jax.ShapeDtypeStruct((bg, lg, n), bf16),
        compiler_params=_cparams(("parallel", "parallel"), 48),
        name="inproj_ret" if retention else "inproj_lru",
    )(*args)


def _row_iota():
    return lax.broadcasted_iota(jnp.int32, (RET_CHUNK, LANES), 0).astype(f32)


def _retstate_kernel(dec_ref, k_ref, v_ref, s0b_ref, sbn_ref, sb_scr, *, tc):
    @pl.when(pl.program_id(1) == 0)
    def _():
        sb_scr[...] = s0b_ref[...]

    row = _row_iota()
    c = float(RET_CHUNK)
    for hh in range(RET_HEADS):
        lg_b = jax.nn.log_sigmoid(dec_ref[1, hh])
        kd_b = jnp.exp(lg_b * row)
        gc_b = jnp.exp(lg_b * c)
        cs = slice(hh * RET_DK, (hh + 1) * RET_DK)
        for cc in reversed(range(tc)):
            rs = slice(cc * RET_CHUNK, (cc + 1) * RET_CHUNK)
            s_next = sb_scr[hh]
            sbn_ref[cc, hh] = s_next.astype(bf16)
            kd = (k_ref[rs, cs].astype(f32) * kd_b).astype(bf16)
            kv = lax.dot_general(kd, v_ref[rs, cs], (((0,), (0,)), ((), ())), preferred_element_type=f32)
            sb_scr[hh] = gc_b * s_next + kv


def _retstate_call(qkv, dec, s0b, tc):
    bs, ls, _ = qkv.shape
    nc = ls // RET_CHUNK
    t = tc * RET_CHUNK
    ns = nc // tc
    hw = RET_HEADS * RET_DK
    st = (RET_HEADS, RET_DK, RET_DV)
    return pl.pallas_call(
        functools.partial(_retstate_kernel, tc=tc),
        grid=(bs, ns),
        in_specs=[
            pl.BlockSpec((2, RET_HEADS, 1, LANES), lambda b, i: (0, 0, 0, 0)),
            pl.BlockSpec((None, t, hw), lambda b, i: (b, ns - 1 - i, 1)),
            pl.BlockSpec((None, t, hw), lambda b, i: (b, ns - 1 - i, 2)),
            pl.BlockSpec((None,) + st, lambda b, i: (b, 0, 0, 0)),
        ],
        out_specs=pl.BlockSpec((None, tc) + st, lambda b, i: (b, ns - 1 - i, 0, 0, 0)),
        out_shape=jax.ShapeDtypeStruct((bs, nc) + st, bf16),
        scratch_shapes=[pltpu.VMEM(st, f32)],
        compiler_params=_cparams(("parallel", "arbitrary"), 32),
        name="retstate",
    )(dec, qkv, qkv, s0b)


def _mix0_kernel(*refs, tc, whole_seq):
    if whole_seq:
        dec_ref, q_ref, k_ref, v_ref, g_ref, u_ref, vv_ref, ws_ref, bs_ref, o_ref, sff_ref, sbf_ref = refs
    else:
        (dec_ref, q_ref, k_ref, v_ref, g_ref, u_ref, vv_ref, s0f_ref, sbn_ref, ws_ref, bs_ref,
         o_ref, sf_scr) = refs

        @pl.when(pl.program_id(1) == 0)
        def _():
            sf_scr[...] = s0f_ref[...]

    row = _row_iota()
    col = lax.broadcasted_iota(jnp.int32, (RET_CHUNK, LANES), 1).astype(f32)
    c = float(RET_CHUNK)
    nt = (((1,), (1,)), ((), ()))
    tn = (((0,), (0,)), ((), ()))
    for hh in range(RET_HEADS):
        lg_f = jax.nn.log_sigmoid(dec_ref[0, hh])
        lg_b = jax.nn.log_sigmoid(dec_ref[1, hh])
        decay = jnp.where(row >= col,
                          jnp.exp(lg_f * jnp.maximum(row - col, 0.0)),
                          jnp.exp(lg_b * jnp.maximum(col - row - 1.0, 0.0)))
        qd_f = jnp.exp(lg_f * (row + 1.0))
        qd_b = jnp.exp(lg_b * (c - 1.0 - row))
        cs = slice(hh * RET_DK, (hh + 1) * RET_DK)
        kd_f = jnp.exp(lg_f * (c - 1.0 - row))
        kd_b = jnp.exp(lg_b * row)
        gc_f = jnp.exp(lg_f * c)
        gc_b = jnp.exp(lg_b * c)
        s_prev, s_next = [None] * tc, [None] * tc
        sf = None if whole_seq else sf_scr[hh]
        sb = None
        for cc in range(tc):
            rs = slice(cc * RET_CHUNK, (cc + 1) * RET_CHUNK)
            kf = k_ref[rs, cs].astype(f32)
            kv = lax.dot_general((kf * kd_f).astype(bf16), v_ref[rs, cs], tn, preferred_element_type=f32)
            s_prev[cc] = sf
            sf = kv if sf is None else gc_f * sf + kv
        if whole_seq:
            for cc in reversed(range(tc)):
                rs = slice(cc * RET_CHUNK, (cc + 1) * RET_CHUNK)
                kf = k_ref[rs, cs].astype(f32)
                kv = lax.dot_general((kf * kd_b).astype(bf16), v_ref[rs, cs], tn, preferred_element_type=f32)
                s_next[cc] = sb
                sb = kv if sb is None else gc_b * sb + kv
            sff_ref[hh] = sf
            sbf_ref[hh] = sb
        else:
            sf_scr[hh] = sf
        for cc in range(tc):
            rs = slice(cc * RET_CHUNK, (cc + 1) * RET_CHUNK)
            q = q_ref[rs, cs]
            qf = q.astype(f32)
            s = lax.dot_general(q, k_ref[rs, cs], nt, preferred_element_type=f32)
            o = jnp.dot((s * decay).astype(bf16), v_ref[rs, cs], preferred_element_type=f32)
            sp_c = None if s_prev[cc] is None else s_prev[cc].astype(bf16)
            if whole_seq:
                sn_c = None if s_next[cc] is None else s_next[cc].astype(bf16)
            else:
                sn_c = sbn_ref[cc, hh]
            if sp_c is not None:
                o = o + jnp.dot((qf * qd_f).astype(bf16), sp_c, preferred_element_type=f32)
            if sn_c is not None:
                o = o + jnp.dot((qf * qd_b).astype(bf16), sn_c, preferred_element_type=f32)
            mu = jnp.mean(o, axis=-1, keepdims=True)
            var = jnp.mean(jnp.square(o - mu), axis=-1, keepdims=True)
            on = (o - mu) * lax.rsqrt(var + EPS)
            gate = g_ref[rs, cs].astype(f32)
            o_ref[rs, cs] = (gate * jax.nn.sigmoid(gate) * on).astype(bf16)
    off = RET_HEADS * RET_DV
    for gg in range(GMLP_GROUPS):
        cs = slice(gg * GMLP_CH, (gg + 1) * GMLP_CH)
        w = ws_ref[gg]
        bias = bs_ref[gg]
        for cc in range(tc):
            rs = slice(cc * RET_CHUNK, (cc + 1) * RET_CHUNK)
            gu = jax.nn.gelu(u_ref[rs, cs].astype(f32))
            gv = jax.nn.gelu(vv_ref[rs, cs].astype(f32))
            gv = gv * lax.rsqrt(jnp.mean(gv * gv, axis=-1, keepdims=True) + EPS)
            sp = jnp.dot(w, gv.astype(bf16), preferred_element_type=f32) + bias
            o_ref[rs, off + gg * GMLP_CH: off + (gg + 1) * GMLP_CH] = (gu * sp).astype(bf16)


def _mix0_call(qkv, dec, s0f, sbn, ws, bs, tc):
    bsz, ls, _ = qkv.shape
    t = tc * RET_CHUNK
    hw = RET_HEADS * RET_DK
    st = (RET_HEADS, RET_DK, RET_DV)
    col_spec = lambda j: pl.BlockSpec((None, t, hw), lambda b, i, j=j: (b, i, j))
    whole_seq = s0f is None
    in_specs = [
        pl.BlockSpec((2, RET_HEADS, 1, LANES), lambda b, i: (0, 0, 0, 0)),
        col_spec(0), col_spec(1), col_spec(2), col_spec(3), col_spec(4), col_spec(5),
    ]
    args = [dec, qkv, qkv, qkv, qkv, qkv, qkv]
    out_specs = [pl.BlockSpec((None, t, AB_OUT), lambda b, i: (b, i, 0))]
    out_shape = [jax.ShapeDtypeStruct((bsz, ls, AB_OUT), bf16)]
    if whole_seq:
        assert ls == t
        out_specs += [pl.BlockSpec((None,) + st, lambda b, i: (b, 0, 0, 0))] * 2
        out_shape += [jax.ShapeDtypeStruct((bsz,) + st, f32)] * 2
    else:
        in_specs += [pl.BlockSpec((None,) + st, lambda b, i: (b, 0, 0, 0)),
                     pl.BlockSpec((None, tc) + st, lambda b, i: (b, i, 0, 0, 0))]
        args += [s0f, sbn]
    in_specs += [
        pl.BlockSpec((GMLP_GROUPS, RET_CHUNK, RET_CHUNK), lambda b, i: (0, 0, 0)),
        pl.BlockSpec((GMLP_GROUPS, RET_CHUNK, 1), lambda b, i: (0, 0, 0)),
    ]
    args += [ws, bs]
    return pl.pallas_call(
        functools.partial(_mix0_kernel, tc=tc, whole_seq=whole_seq),
        grid=(bsz, ls // t),
        in_specs=in_specs,
        out_specs=out_specs,
        out_shape=out_shape,
        scratch_shapes=[] if whole_seq else [pltpu.VMEM(st, f32)],
        compiler_params=_cparams(("parallel", "parallel" if whole_seq else "arbitrary"), 32),
        name="mix0_seq" if whole_seq else "mix0",
    )(*args)


def _loop(n, body, carry):
    for i in range(n):
        carry = body(i, carry)
    return carry


def _lru_body(gate_ref, xin_ref, cw_ref, cb_ref, wg_ref, bg_ref, lam_ref, h0f_ref, h0b_ref,
              o_ref, hfl_ref, hbf_ref, xs, xt, af, bfs, ab, bbs, *, nseg, chained):
    seg, pitch = LRU_SEG, LRU_PITCH
    blk = seg

    def stage(s, carry):
        xs[pl.ds(s * pitch, seg), :] = xin_ref[pl.ds(s * seg, seg), :].astype(f32)
        return carry

    _loop(nseg, stage, 0)

    def tile(j):
        return pl.ds(j * nseg, nseg)

    def gather(j, carry):
        xt[tile(j + 2), :] = xs[pl.ds(j, nseg, stride=pitch), :]
        return carry

    _loop(seg, gather, 0)
    if chained:
        srow = lax.broadcasted_iota(jnp.int32, (nseg, LANES), 0)
        xt[tile(0), :] = jnp.where(srow == 0, 0.0, pltpu.roll(xt[tile(seg), :], 1, axis=0))
        xt[tile(1), :] = jnp.where(srow == 0, 0.0, pltpu.roll(xt[tile(seg + 1), :], 1, axis=0))
        xt[tile(seg + 2), :] = jnp.where(srow == nseg - 1, 0.0, pltpu.roll(xt[tile(2), :], nseg - 1, axis=0))
    else:
        zt = jnp.zeros((nseg, LANES), f32)
        xt[tile(0), :] = zt
        xt[tile(1), :] = zt
        xt[tile(seg + 2), :] = zt

    cw = cw_ref[...]
    cb = cb_ref[...]
    bg = bg_ref[...]
    sp4 = (0.5 * LRU_C) * jax.nn.softplus(-lam_ref[...])

    def gates(jb, carry):
        p0 = jb * blk
        xm2 = xt[pl.ds(p0, blk), :]
        xm1 = xt[pl.ds(p0 + nseg, blk), :]
        x0 = xt[pl.ds(p0 + 2 * nseg, blk), :]
        xp1 = xt[pl.ds(p0 + 3 * nseg, blk), :]
        xc = cb + xm2 * cw[0:1] + xm1 * cw[1:2] + x0 * cw[2:3] + xp1 * cw[3:4]
        gt = jnp.dot(xc.astype(bf16), wg_ref[...], preferred_element_type=f32) + bg
        xch = 0.5 * xc
        for d, (a_s, b_s) in enumerate(((af, bfs), (ab, bbs))):
            tr = jnp.tanh(gt[:, (2 * d) * LANES:(2 * d + 1) * LANES])
            ti = jnp.tanh(gt[:, (2 * d + 1) * LANES:(2 * d + 2) * LANES])
            nla = (tr + 1.0) * sp4[d:d + 1]
            a = jnp.exp(-nla)
            m2 = jnp.tanh(nla) * (1.0 + a * a)
            mult = m2 * lax.rsqrt(jnp.maximum(m2, 1e-30))
            a_s[pl.ds(p0, blk), :] = a
            b_s[pl.ds(p0, blk), :] = mult * ((ti + 1.0) * xch)
        return carry

    _loop(nseg, gates, 0)

    if chained:
        def ends(j, carry):
            hf, pf, hb, pb = carry
            a = af[tile(j), :]
            hf = a * hf + bfs[tile(j), :]
            pf = a * pf
            a2 = ab[tile(seg - 1 - j), :]
            hb = a2 * hb + bbs[tile(seg - 1 - j), :]
            pb = a2 * pb
            return hf, pf, hb, pb

        zero = jnp.zeros((nseg, LANES), f32)
        one = jnp.ones((nseg, LANES), f32)
        hf_e, pf_e, hb_e, pb_e = _loop(seg, ends, (zero, one, zero, one))
        cf = [h0f_ref[0:1, :]]
        for s in range(1, nseg):
            cf.append(hf_e[s - 1:s] + pf_e[s - 1:s] * cf[-1])
        cbk = [h0b_ref[0:1, :]]
        for s in range(nseg - 2, -1, -1):
            cbk.append(hb_e[s + 1:s + 2] + pb_e[s + 1:s + 2] * cbk[-1])
        hf0 = jnp.concatenate(cf, axis=0)
        hb0 = jnp.concatenate(cbk[::-1], axis=0)
    else:
        hf0 = h0f_ref[...]
        hb0 = h0b_ref[...]

    def scan_b(k, hb):
        j = seg - 1 - k
        hb = ab[tile(j), :] * hb + bbs[tile(j), :]
        xt[tile(j), :] = hb
        return hb

    hb_l = _loop(seg, scan_b, hb0)

    def scan_f(j, hf):
        hf = af[tile(j), :] * hf + bfs[tile(j), :]
        xs[pl.ds(j, nseg, stride=pitch), :] = hf + xt[tile(j), :]
        return hf

    hf_l = _loop(seg, scan_f, hf0)
    if chained:
        hfl_ref[...] = hf_l[nseg - 1:nseg]
        hbf_ref[...] = hb_l[0:1]
    else:
        hfl_ref[...] = hf_l
        hbf_ref[...] = hb_l

    def combine(s, carry):
        g = jax.nn.gelu(gate_ref[pl.ds(s * seg, seg), :].astype(f32))
        o_ref[pl.ds(s * seg, seg), :] = (g * xs[pl.ds(s * pitch, seg), :]).astype(bf16)
        return carry

    _loop(nseg, combine, 0)


def _lru_specs(ntok, nseg, gc, pair):
    rows = LRU_SEG * nseg
    nt = ntok // rows
    nb = LRU_BLOCKS
    in_specs = [
        pl.BlockSpec((rows, LANES), lambda *g: pair(*g)),
        pl.BlockSpec((rows, LANES), lambda *g: (pair(*g)[0], nb + pair(*g)[1])),
        pl.BlockSpec((4, LANES), lambda *g: (0, pair(*g)[1])),
        pl.BlockSpec((1, LANES), lambda *g: (0, pair(*g)[1])),
        pl.BlockSpec((None, LANES, 4 * LANES), lambda *g: (pair(*g)[1], 0, 0)),
        pl.BlockSpec((None, 1, 4 * LANES), lambda *g: (pair(*g)[1], 0, 0)),
        pl.BlockSpec((2, LANES), lambda *g: (0, pair(*g)[1])),
        pl.BlockSpec((None, gc, LANES), lambda *g: (pair(*g)[0], 0, pair(*g)[1])),
        pl.BlockSpec((None, gc, LANES), lambda *g: (pair(*g)[0], 0, pair(*g)[1])),
    ]
    out_specs = [
        pl.BlockSpec((rows, LANES), lambda *g: pair(*g)),
        pl.BlockSpec((None, gc, LANES), lambda *g: (pair(*g)[0], 0, pair(*g)[1])),
        pl.BlockSpec((None, gc, LANES), lambda *g: (pair(*g)[0], 0, pair(*g)[1])),
    ]
    out_shape = [
        jax.ShapeDtypeStruct((ntok, D_RNN), bf16),
        jax.ShapeDtypeStruct((nt, gc, D_RNN), f32),
        jax.ShapeDtypeStruct((nt, gc, D_RNN), f32),
    ]
    scr = pltpu.VMEM((rows, LANES), f32)
    scratch = [pltpu.VMEM((nseg * LRU_PITCH, LANES), f32), pltpu.VMEM((rows + 3 * nseg, LANES), f32),
               scr, scr, scr, scr]
    return in_specs, out_specs, out_shape, scratch


def _outmlp_body(x_ref, mix_ref, g1_ref, sh2_ref, sc2_ref, g2_ref, gain_ref, fgain_ref,
                 wo_ref, w1_ref, w2_ref, o_ref, *, final):
    mix = jnp.dot(mix_ref[...], wo_ref[...], preferred_element_type=f32)
    x1 = x_ref[...] + g1_ref[...] * mix
    h = (_rms(x1, gain_ref[...]) * (1.0 + sc2_ref[...]) + sh2_ref[...]).astype(bf16)
    acc = None
    for j in range(0, D_FF, FF_CHUNK):
        hid = jnp.dot(h, w1_ref[:, j:j + FF_CHUNK], preferred_element_type=f32)
        hid = jnp.square(jnp.maximum(hid, 0.0)).astype(bf16)
        part = jnp.dot(hid, w2_ref[j:j + FF_CHUNK, :], preferred_element_type=f32)
        acc = part if acc is None else acc + part
    x2 = x1 + g2_ref[...] * acc
    if final:
        x2 = _rms(x2, fgain_ref[...])
    o_ref[...] = x2


N_MLP_IN = 11
N_LRU_IN = 9
N_LRU_OUT = 3


def _outmlp_kernel(*refs, final, lru):
    if lru is None:
        _outmlp_body(*refs, final=final)
        return
    mlp_in = refs[:N_MLP_IN]
    lru_in = refs[N_MLP_IN:N_MLP_IN + N_LRU_IN]
    outs = refs[N_MLP_IN + N_LRU_IN:]
    _lru_body(*lru_in, *outs[1:1 + N_LRU_OUT], *outs[1 + N_LRU_OUT:], nseg=lru[0], chained=lru[1])
    _outmlp_body(*mlp_in, outs[0], final=final)


def _outmlp_call(x3, mix3, modl, gain, fgain, wo, w1, w2, *, layer, final, t=TOK_TILE, lru_args=None):
    bg, lg, d = x3.shape
    steps = lg // t
    dm = mix3.shape[-1]
    mod_spec = lambda k: pl.BlockSpec((None, 1, d), lambda b, i, k=k: (b, 0, k))

    def resident(shape, lead=None):
        if lead is None:
            return pl.BlockSpec(shape, lambda b, i: (0, 0), pipeline_mode=pl.Buffered(1))
        return pl.BlockSpec((None,) + shape, lambda b, i: (lead, 0, 0), pipeline_mode=pl.Buffered(1))

    in_specs = [
        pl.BlockSpec((None, t, d), lambda b, i: (b, i, 0)),
        pl.BlockSpec((None, t, dm), lambda b, i: (b, i, 0)),
        mod_spec(2), mod_spec(3), mod_spec(4), mod_spec(5),
        pl.BlockSpec((1, d), lambda b, i: (0, 0)),
        pl.BlockSpec((1, d), lambda b, i: (0, 0)),
        resident((dm, d)), resident((d, D_FF), layer), resident((D_FF, d), layer),
    ]
    args = [x3, mix3, modl, modl, modl, modl, gain.reshape(1, d), fgain.reshape(1, d), wo, w1, w2]
    out_specs = [pl.BlockSpec((None, t, d), lambda b, i: (b, i, 0))]
    out_shape = [jax.ShapeDtypeStruct((bg, lg, d), f32)]
    scratch = []
    lru = None
    if lru_args is not None:
        gx, cw, cb, wg, bgate, lam, h0f, h0b, nseg, chained = lru_args
        ntok = gx.shape[0]
        assert (ntok // (LRU_SEG * nseg)) * LRU_BLOCKS == bg * steps, "one RG-LRU (tile, block) pair per MLP step"

        def pair(b, i):
            p = b * steps + i
            return p // LRU_BLOCKS, p % LRU_BLOCKS

        li, lo, ls, scratch = _lru_specs(ntok, nseg, h0f.shape[1], pair)
        in_specs += li
        out_specs += lo
        out_shape += ls
        args += [gx, gx, cw, cb, wg, bgate, lam, h0f, h0b]
        lru = (nseg, chained)
    res = pl.pallas_call(
        functools.partial(_outmlp_kernel, final=final, lru=lru),
        grid=(bg, steps),
        in_specs=in_specs,
        out_specs=out_specs,
        out_shape=out_shape,
        scratch_shapes=scratch,
        compiler_params=_cparams(("parallel", "parallel"), 56),
        name=("outmlp_final" if final else "outmlp") + ("_rglru" if lru else ""),
    )(*args)
    return res if lru else res[0]


@functools.lru_cache(maxsize=None)
def _rope_tables(length):
    pos = np.arange(length)
    rowp = (pos // GRID_W).astype(np.float64)
    colp = (pos % GRID_W).astype(np.float64)
    nf = RET_DK // 4
    inv_freq = ROPE_BASE ** (-np.arange(nf, dtype=np.float64) / nf)
    ar = rowp[:, None] * inv_freq[None, :]
    ac = colp[:, None] * inv_freq[None, :]
    cos = np.concatenate([np.cos(ar), np.cos(ar), np.cos(ac), np.cos(ac)], axis=-1).astype(np.float32)
    sin = np.concatenate([-np.sin(ar), np.sin(ar), -np.sin(ac), np.sin(ac)], axis=-1).astype(np.float32)
    return cos, sin


def kernel(x_prompt, x_sample, c, c_ctx, state_ret_fwd, state_ret_bwd, state_lru_fwd, state_lru_bwd, mod_w, mod_b, norm_mix, norm_mlp, mlp_w1, mlp_w2, ab_w_in, ab_w_out, ret_decay_fwd, ret_decay_bwd, gmlp_ws, gmlp_bs, lru_w_in, lru_conv_w, lru_conv_b, lru_gate_a_w, lru_gate_a_b, lru_gate_x_w, lru_gate_x_b, lru_lambda, lru_w_out, final_norm):
    d = D_MODEL
    n_lat = c.shape[0]
    conds = jnp.concatenate([c_ctx[None, :], c, jnp.zeros((COND_ROWS - 1 - n_lat, d), f32)], axis=0)
    mod = _mod_call(conds, mod_w, mod_b)

    w_in0 = ab_w_in[0].astype(bf16)
    w_out0 = ab_w_out[0].astype(bf16)
    w1 = mlp_w1.astype(bf16)
    w2 = mlp_w2.astype(bf16)
    w_in1 = lru_w_in[0].astype(bf16)
    w_out1 = lru_w_out[0].astype(bf16)
    ws = gmlp_ws[0].astype(bf16)
    bs = gmlp_bs[0][:, :, None]
    dec = jnp.broadcast_to(jnp.stack([ret_decay_fwd[0], ret_decay_bwd[0]])[:, :, None, None],
                           (2, RET_HEADS, 1, LANES))
    wg = (0.5 * jnp.concatenate([lru_gate_a_w[0, 0], lru_gate_x_w[0, 0], lru_gate_a_w[0, 1], lru_gate_x_w[0, 1]],
                                axis=-1)).astype(bf16)
    bgate = 0.5 * jnp.stack([lru_gate_a_b[0, 0], lru_gate_x_b[0, 0], lru_gate_a_b[0, 1], lru_gate_x_b[0, 1]], axis=0)
    bgate = bgate.reshape(4, LRU_BLOCKS, LRU_BS).transpose(1, 0, 2).reshape(LRU_BLOCKS, 1, 4 * LRU_BS)
    lru_w = (lru_conv_w[0], lru_conv_b[0].reshape(1, D_RNN), wg, bgate, lru_lambda[0])

    def mixer0(x3, modl, bsz, ls, ret_f0, ret_b0, latent):
        tc = min(4, ls // RET_CHUNK)
        qkv = _inproj_call(x3, modl, norm_mix[0], w_in0, _rope_tables(ls) if latent else None,
                           retention=True, sh_col=0, sc_col=1)
        qkv = qkv.reshape(bsz, ls, AB_IN)
        if ret_f0 is None:
            mix, sff, sbf = _mix0_call(qkv, dec, None, None, ws, bs, tc)
        else:
            sbn = _retstate_call(qkv, dec, ret_b0, tc)
            mix = _mix0_call(qkv, dec, ret_f0, sbn, ws, bs, tc)[0]
            sff = sbf = None
        return mix.reshape(x3.shape[0], x3.shape[1], AB_OUT), sff, sbf

    def lru_inproj(x3, modl):
        gx = _inproj_call(x3, modl, norm_mix[1], w_in1, None, retention=False, sh_col=0, sc_col=1)
        return gx.reshape(-1, 2 * D_RNN)

    bp, lp, _ = x_prompt.shape
    bl, ll, _ = x_sample.shape
    assert lp == LRU_SEG and ll % LRU_SEG == 0
    xc = x_prompt.reshape(1, bp * lp, d)
    xl = x_sample
    mod_c = [mod[l, 0:1][:, None, :] for l in range(DEPTH)]
    mod_l = [mod[l, 1:1 + n_lat][:, None, :] for l in range(DEPTH)]

    mix_c, sff, sbf = mixer0(xc, mod_c[0], bp, lp, None, None, False)
    xc = _outmlp_call(xc, mix_c, mod_c[0], norm_mlp[0], final_norm, w_out0, w1, w2, layer=0, final=False)
    gx_c = lru_inproj(xc, mod_c[1])
    mix_l, _, _ = mixer0(xl, mod_l[0], bl, ll, state_ret_fwd[:, 0], state_ret_bwd[:, 0], True)

    steps_l = bl * (ll // TOK_TILE)
    nseg_c = (bp * lp // LRU_SEG) * LRU_BLOCKS // steps_l
    zero_lru = jnp.zeros((bp // nseg_c, nseg_c, D_RNN), f32)
    xl, pre_c, hfl, hbf = _outmlp_call(
        xl, mix_l, mod_l[0], norm_mlp[0], final_norm, w_out0, w1, w2, layer=0, final=False,
        lru_args=(gx_c,) + lru_w + (zero_lru, zero_lru, nseg_c, False))
    gx_l = lru_inproj(xl, mod_l[1])

    nseg_l = ll // LRU_SEG
    t_c = bp * lp // (bl * LRU_BLOCKS)
    y_c, pre_l, _, _ = _outmlp_call(
        xc, pre_c.reshape(1, bp * lp, D_RNN), mod_c[1], norm_mlp[1], final_norm, w_out1, w1, w2,
        layer=1, final=True, t=t_c,
        lru_args=(gx_l,) + lru_w + (state_lru_fwd[:, 0].reshape(bl, 1, D_RNN),
                                    state_lru_bwd[:, 0].reshape(bl, 1, D_RNN), nseg_l, True))
    y_l = _outmlp_call(xl, pre_l.reshape(bl, ll, D_RNN), mod_l[1], norm_mlp[1], final_norm, w_out1, w1, w2,
                       layer=1, final=True)
    return (y_c.reshape(bp, lp, d), y_l, sff[:, None], sbf[:, None],
            hfl.reshape(bp, 1, D_RNN), hbf.reshape(bp, 1, D_RNN))
```

```python
import functools

import jax
import jax.numpy as jnp
import numpy as np
from jax import lax
from jax.experimental import pallas as pl
from jax.experimental.pallas import tpu as pltpu

f32 = jnp.float32
bf16 = jnp.bfloat16

D_MODEL = 1024
DEPTH = 2
GRID_W = 64
EPS = 1e-6
RET_HEADS = 4
RET_DK = 128
RET_DV = 128
RET_CHUNK = 128
ROPE_BASE = 10000.0
GMLP_GROUPS = 4
GMLP_CH = 128
AB_IN = 3072
AB_OUT = 1024
D_RNN = D_MODEL
LRU_BLOCKS = 8
LRU_BS = 128
LRU_C = 8.0
D_FF = 4 * D_MODEL

LANES = 128
SUBLANES = 8
MIB = 1024 * 1024

TOK_TILE = 512
INPROJ_TILE = 1024
COND_ROWS = 8
MOD_TN = 1536
LRU_SEG = 256
LRU_PITCH = LRU_SEG + SUBLANES
FF_CHUNK = 1024


def _cparams(sem, vmem_mib):
    return pltpu.CompilerParams(dimension_semantics=sem, vmem_limit_bytes=vmem_mib * MIB)


def _rms(x, gain):
    ms = jnp.mean(x * x, axis=-1, keepdims=True)
    return x * lax.rsqrt(ms + EPS) * gain


def _mod_kernel(cond_ref, w_ref, b_ref, o_ref):
    c = cond_ref[...]
    s = c * jax.nn.sigmoid(c)
    o_ref[...] = jnp.dot(s.astype(bf16), w_ref[...].astype(bf16), preferred_element_type=f32) + b_ref[...]


def _mod_call(conds, mod_w, mod_b):
    depth, d, n = mod_w.shape
    return pl.pallas_call(
        _mod_kernel,
        grid=(depth, n // MOD_TN),
        in_specs=[
            pl.BlockSpec((COND_ROWS, d), lambda l, j: (0, 0)),
            pl.BlockSpec((None, d, MOD_TN), lambda l, j: (l, 0, j)),
            pl.BlockSpec((None, 1, MOD_TN), lambda l, j: (l, 0, j)),
        ],
        out_specs=pl.BlockSpec((None, COND_ROWS, MOD_TN), lambda l, j: (l, 0, j)),
        out_shape=jax.ShapeDtypeStruct((depth, COND_ROWS, n), f32),
        compiler_params=_cparams(("parallel", "parallel"), 32),
        name="mod",
    )(conds, mod_w, mod_b.reshape(depth, 1, n))


def _rope(t, cos, sin_signed, first_half):
    up = pltpu.roll(t, LANES - 32, axis=1)
    dn = pltpu.roll(t, 32, axis=1)
    return t * cos + jnp.where(first_half, up, dn) * sin_signed


def _inproj_kernel(*refs, retention, rope, bwd_states):
    if bwd_states:
        (x_ref, sh_ref, sc_ref, gain_ref, w_ref, cos_ref, sin_ref, dec_ref, s0b_ref,
         o_ref, sbn_ref, sb_scr) = refs

        @pl.when(pl.program_id(1) == 0)
        def _():
            sb_scr[...] = s0b_ref[...]
    elif rope:
        x_ref, sh_ref, sc_ref, gain_ref, w_ref, cos_ref, sin_ref, o_ref = refs
    else:
        x_ref, sh_ref, sc_ref, gain_ref, w_ref, o_ref = refs
    h = _rms(x_ref[...], gain_ref[...]) * (1.0 + sc_ref[...]) + sh_ref[...]
    hb = h.astype(bf16)
    n = w_ref.shape[1]
    if not retention:
        for j in range(0, n, 512):
            o_ref[:, j:j + 512] = jnp.dot(hb, w_ref[:, j:j + 512], preferred_element_type=f32).astype(bf16)
        return
    qk = RET_HEADS * RET_DK
    if rope:
        cos = cos_ref[...]
        sin = sin_ref[...]
        lane = lax.broadcasted_iota(jnp.int32, cos.shape, 1)
        first_half = (lane % 64) < 32
    k_heads = []
    for part in range(2):
        y = jnp.dot(hb, w_ref[:, part * qk:(part + 1) * qk], preferred_element_type=f32)
        for hh in range(RET_HEADS):
            t = y[:, hh * RET_DK:(hh + 1) * RET_DK]
            if part == 1:
                t = t * (RET_DK ** -0.5)
            if rope:
                t = _rope(t, cos, sin, first_half)
            c0 = part * qk + hh * RET_DK
            tb = t.astype(bf16)
            o_ref[:, c0:c0 + RET_DK] = tb
            if part == 1:
                k_heads.append(tb)
    for j in range(2 * qk, n, 512):
        yb = jnp.dot(hb, w_ref[:, j:j + 512], preferred_element_type=f32).astype(bf16)
        o_ref[:, j:j + 512] = yb
        if bwd_states and j == 2 * qk:
            vb = yb
    if bwd_states:
        row = lax.broadcasted_iota(jnp.int32, (RET_CHUNK, LANES), 0).astype(f32)
        nchunk = x_ref.shape[0] // RET_CHUNK
        for hh in range(RET_HEADS):
            lg_b = jax.nn.log_sigmoid(dec_ref[1, hh])
            kd_b = jnp.exp(lg_b * row)
            gc_b = jnp.exp(lg_b * float(RET_CHUNK))
            cs = slice(hh * RET_DK, (hh + 1) * RET_DK)
            for cc in reversed(range(nchunk)):
                rs = slice(cc * RET_CHUNK, (cc + 1) * RET_CHUNK)
                s_next = sb_scr[hh]
                sbn_ref[cc, hh] = s_next.astype(bf16)
                kd = (k_heads[hh][rs].astype(f32) * kd_b).astype(bf16)
                kv = lax.dot_general(kd, vb[rs, cs], (((0,), (0,)), ((), ())), preferred_element_type=f32)
                sb_scr[hh] = gc_b * s_next + kv


def _inproj_call(x3, modl, gain, w, rope_tabs, *, retention, sh_col, sc_col, bwd=None):
    bg, lg, d = x3.shape
    n = w.shape[1]
    t = INPROJ_TILE
    ns = lg // t
    rope = rope_tabs is not None
    pos = (lambda i: ns - 1 - i) if bwd is not None else (lambda i: i)
    in_specs = [
        pl.BlockSpec((None, t, d), lambda b, i: (b, pos(i), 0)),
        pl.BlockSpec((None, 1, d), lambda b, i: (b, 0, sh_col)),
        pl.BlockSpec((None, 1, d), lambda b, i: (b, 0, sc_col)),
        pl.BlockSpec((1, d), lambda b, i: (0, 0)),
        pl.BlockSpec((d, n), lambda b, i: (0, 0), pipeline_mode=pl.Buffered(1)),
    ]
    args = [x3, modl, modl, gain.reshape(1, d), w]
    if rope:
        in_specs += [pl.BlockSpec((t, LANES), lambda b, i: (pos(i), 0))] * 2
        args += list(rope_tabs)
    out_specs = [pl.BlockSpec((None, t, n), lambda b, i: (b, pos(i), 0))]
    out_shape = [jax.ShapeDtypeStruct((bg, lg, n), bf16)]
    scratch = []
    if bwd is not None:
        st = (RET_HEADS, RET_DK, RET_DV)
        tc = t // RET_CHUNK
        in_specs += [pl.BlockSpec((2, RET_HEADS, 1, LANES), lambda b, i: (0, 0, 0, 0)),
                     pl.BlockSpec((None,) + st, lambda b, i: (b, 0, 0, 0))]
        args += list(bwd)
        out_specs.append(pl.BlockSpec((None, tc) + st, lambda b, i: (b, pos(i), 0, 0, 0)))
        out_shape.append(jax.ShapeDtypeStruct((bg, lg // RET_CHUNK) + st, bf16))
        scratch.append(pltpu.VMEM(st, f32))
    res = pl.pallas_call(
        functools.partial(_inproj_kernel, retention=retention, rope=rope, bwd_states=bwd is not None),
        grid=(bg, ns),
        in_specs=in_specs,
        out_specs=out_specs,
        out_shape=out_shape,
        scratch_shapes=scratch,
        compiler_params=_cparams(("parallel", "arbitrary" if bwd is not None else "parallel"), 48),
        name="inproj_ret" if retention else "inproj_lru",
    )(*args)
    return res if bwd is not None else res[0]


def _row_iota():
    return lax.broadcasted_iota(jnp.int32, (RET_CHUNK, LANES), 0).astype(f32)


def _mix0_kernel(*refs, tc, whole_seq):
    if whole_seq:
        dec_ref, q_ref, k_ref, v_ref, g_ref, u_ref, vv_ref, ws_ref, bs_ref, o_ref, sff_ref, sbf_ref = refs
    else:
        (dec_ref, q_ref, k_ref, v_ref, g_ref, u_ref, vv_ref, s0f_ref, sbn_ref, ws_ref, bs_ref,
         o_ref, sf_scr) = refs

        @pl.when(pl.program_id(1) == 0)
        def _():
            sf_scr[...] = s0f_ref[...]

    row = _row_iota()
    col = lax.broadcasted_iota(jnp.int32, (RET_CHUNK, LANES), 1).astype(f32)
    c = float(RET_CHUNK)
    nt = (((1,), (1,)), ((), ()))
    tn = (((0,), (0,)), ((), ()))
    for hh in range(RET_HEADS):
        lg_f = jax.nn.log_sigmoid(dec_ref[0, hh])
        lg_b = jax.nn.log_sigmoid(dec_ref[1, hh])
        decay = jnp.where(row >= col,
                          jnp.exp(lg_f * jnp.maximum(row - col, 0.0)),
                          jnp.exp(lg_b * jnp.maximum(col - row - 1.0, 0.0)))
        qd_f = jnp.exp(lg_f * (row + 1.0))
        qd_b = jnp.exp(lg_b * (c - 1.0 - row))
        cs = slice(hh * RET_DK, (hh + 1) * RET_DK)
        kd_f = jnp.exp(lg_f * (c - 1.0 - row))
        kd_b = jnp.exp(lg_b * row)
        gc_f = jnp.exp(lg_f * c)
        gc_b = jnp.exp(lg_b * c)
        s_prev, s_next = [None] * tc, [None] * tc
        sf = None if whole_seq else sf_scr[hh]
        sb = None
        for cc in range(tc):
            rs = slice(cc * RET_CHUNK, (cc + 1) * RET_CHUNK)
            kf = k_ref[rs, cs].astype(f32)
            kv = lax.dot_general((kf * kd_f).astype(bf16), v_ref[rs, cs], tn, preferred_element_type=f32)
            s_prev[cc] = sf
            sf = kv if sf is None else gc_f * sf + kv
        if whole_seq:
            for cc in reversed(range(tc)):
                rs = slice(cc * RET_CHUNK, (cc + 1) * RET_CHUNK)
                kf = k_ref[rs, cs].astype(f32)
                kv = lax.dot_general((kf * kd_b).astype(bf16), v_ref[rs, cs], tn, preferred_element_type=f32)
                s_next[cc] = sb
                sb = kv if sb is None else gc_b * sb + kv
            sff_ref[hh] = sf
            sbf_ref[hh] = sb
        else:
            sf_scr[hh] = sf
        for cc in range(tc):
            rs = slice(cc * RET_CHUNK, (cc + 1) * RET_CHUNK)
            q = q_ref[rs, cs]
            qf = q.astype(f32)
            s = lax.dot_general(q, k_ref[rs, cs], nt, preferred_element_type=f32)
            o = jnp.dot((s * decay).astype(bf16), v_ref[rs, cs], preferred_element_type=f32)
            sp_c = None if s_prev[cc] is None else s_prev[cc].astype(bf16)
            if whole_seq:
                sn_c = None if s_next[cc] is None else s_next[cc].astype(bf16)
            else:
                sn_c = sbn_ref[cc, hh]
            if sp_c is not None:
                o = o + jnp.dot((qf * qd_f).astype(bf16), sp_c, preferred_element_type=f32)
            if sn_c is not None:
                o = o + jnp.dot((qf * qd_b).astype(bf16), sn_c, preferred_element_type=f32)
            mu = jnp.mean(o, axis=-1, keepdims=True)
            var = jnp.mean(jnp.square(o - mu), axis=-1, keepdims=True)
            on = (o - mu) * lax.rsqrt(var + EPS)
            gate = g_ref[rs, cs].astype(f32)
            o_ref[rs, cs] = (gate * jax.nn.sigmoid(gate) * on).astype(bf16)
    off = RET_HEADS * RET_DV
    for gg in range(GMLP_GROUPS):
        cs = slice(gg * GMLP_CH, (gg + 1) * GMLP_CH)
        w = ws_ref[gg]
        bias = bs_ref[gg]
        for cc in range(tc):
            rs = slice(cc * RET_CHUNK, (cc + 1) * RET_CHUNK)
            gu = jax.nn.gelu(u_ref[rs, cs].astype(f32))
            gv = jax.nn.gelu(vv_ref[rs, cs].astype(f32))
            gv = gv * lax.rsqrt(jnp.mean(gv * gv, axis=-1, keepdims=True) + EPS)
            sp = jnp.dot(w, gv.astype(bf16), preferred_element_type=f32) + bias
            o_ref[rs, off + gg * GMLP_CH: off + (gg + 1) * GMLP_CH] = (gu * sp).astype(bf16)


def _mix0_call(qkv, dec, s0f, sbn, ws, bs, tc):
    bsz, ls, _ = qkv.shape
    t = tc * RET_CHUNK
    hw = RET_HEADS * RET_DK
    st = (RET_HEADS, RET_DK, RET_DV)
    col_spec = lambda j: pl.BlockSpec((None, t, hw), lambda b, i, j=j: (b, i, j))
    whole_seq = s0f is None
    in_specs = [
        pl.BlockSpec((2, RET_HEADS, 1, LANES), lambda b, i: (0, 0, 0, 0)),
        col_spec(0), col_spec(1), col_spec(2), col_spec(3), col_spec(4), col_spec(5),
    ]
    args = [dec, qkv, qkv, qkv, qkv, qkv, qkv]
    out_specs = [pl.BlockSpec((None, t, AB_OUT), lambda b, i: (b, i, 0))]
    out_shape = [jax.ShapeDtypeStruct((bsz, ls, AB_OUT), bf16)]
    if whole_seq:
        assert ls == t
        out_specs += [pl.BlockSpec((None,) + st, lambda b, i: (b, 0, 0, 0))] * 2
        out_shape += [jax.ShapeDtypeStruct((bsz,) + st, f32)] * 2
    else:
        in_specs += [pl.BlockSpec((None,) + st, lambda b, i: (b, 0, 0, 0)),
                     pl.BlockSpec((None, tc) + st, lambda b, i: (b, i, 0, 0, 0))]
        args += [s0f, sbn]
    in_specs += [
        pl.BlockSpec((GMLP_GROUPS, RET_CHUNK, RET_CHUNK), lambda b, i: (0, 0, 0)),
        pl.BlockSpec((GMLP_GROUPS, RET_CHUNK, 1), lambda b, i: (0, 0, 0)),
    ]
    args += [ws, bs]
    return pl.pallas_call(
        functools.partial(_mix0_kernel, tc=tc, whole_seq=whole_seq),
        grid=(bsz, ls // t),
        in_specs=in_specs,
        out_specs=out_specs,
        out_shape=out_shape,
        scratch_shapes=[] if whole_seq else [pltpu.VMEM(st, f32)],
        compiler_params=_cparams(("parallel", "parallel" if whole_seq else "arbitrary"), 32),
        name="mix0_seq" if whole_seq else "mix0",
    )(*args)


def _loop(n, body, carry):
    for i in range(n):
        carry = body(i, carry)
    return carry


def _lru_body(gate_ref, xin_ref, cw_ref, cb_ref, wg_ref, bg_ref, lam_ref, h0f_ref, h0b_ref,
              o_ref, hfl_ref, hbf_ref, xs, xt, af, bfs, ab, bbs, *, nseg, chained):
    seg, pitch = LRU_SEG, LRU_PITCH
    blk = seg

    def stage(s, carry):
        xs[pl.ds(s * pitch, seg), :] = xin_ref[pl.ds(s * seg, seg), :].astype(f32)
        return carry

    _loop(nseg, stage, 0)

    def tile(j):
        return pl.ds(j * nseg, nseg)

    def gather(j, carry):
        xt[tile(j + 2), :] = xs[pl.ds(j, nseg, stride=pitch), :]
        return carry

    _loop(seg, gather, 0)
    if chained:
        srow = lax.broadcasted_iota(jnp.int32, (nseg, LANES), 0)
        xt[tile(0), :] = jnp.where(srow == 0, 0.0, pltpu.roll(xt[tile(seg), :], 1, axis=0))
        xt[tile(1), :] = jnp.where(srow == 0, 0.0, pltpu.roll(xt[tile(seg + 1), :], 1, axis=0))
        xt[tile(seg + 2), :] = jnp.where(srow == nseg - 1, 0.0, pltpu.roll(xt[tile(2), :], nseg - 1, axis=0))
    else:
        zt = jnp.zeros((nseg, LANES), f32)
        xt[tile(0), :] = zt
        xt[tile(1), :] = zt
        xt[tile(seg + 2), :] = zt

    cw = cw_ref[...]
    cb = cb_ref[...]
    bg = bg_ref[...]
    sp4 = (0.5 * LRU_C) * jax.nn.softplus(-lam_ref[...])

    def gates(jb, carry):
        p0 = jb * blk
        xm2 = xt[pl.ds(p0, blk), :]
        xm1 = xt[pl.ds(p0 + nseg, blk), :]
        x0 = xt[pl.ds(p0 + 2 * nseg, blk), :]
        xp1 = xt[pl.ds(p0 + 3 * nseg, blk), :]
        xc = cb + xm2 * cw[0:1] + xm1 * cw[1:2] + x0 * cw[2:3] + xp1 * cw[3:4]
        gt = jnp.dot(xc.astype(bf16), wg_ref[...], preferred_element_type=f32) + bg
        xch = 0.5 * xc
        for d, (a_s, b_s) in enumerate(((af, bfs), (ab, bbs))):
            tr = jnp.tanh(gt[:, (2 * d) * LANES:(2 * d + 1) * LANES])
            ti = jnp.tanh(gt[:, (2 * d + 1) * LANES:(2 * d + 2) * LANES])
            nla = (tr + 1.0) * sp4[d:d + 1]
            a = jnp.exp(-nla)
            m2 = jnp.tanh(nla) * (1.0 + a * a)
            mult = m2 * lax.rsqrt(jnp.maximum(m2, 1e-30))
            a_s[pl.ds(p0, blk), :] = a
            b_s[pl.ds(p0, blk), :] = mult * ((ti + 1.0) * xch)
        return carry

    _loop(nseg, gates, 0)

    if chained:
        def ends(j, carry):
            hf, pf, hb, pb = carry
            a = af[tile(j), :]
            hf = a * hf + bfs[tile(j), :]
            pf = a * pf
            a2 = ab[tile(seg - 1 - j), :]
            hb = a2 * hb + bbs[tile(seg - 1 - j), :]
            pb = a2 * pb
            return hf, pf, hb, pb

        zero = jnp.zeros((nseg, LANES), f32)
        one = jnp.ones((nseg, LANES), f32)
        hf_e, pf_e, hb_e, pb_e = _loop(seg, ends, (zero, one, zero, one))
        cf = [h0f_ref[0:1, :]]
        for s in range(1, nseg):
            cf.append(hf_e[s - 1:s] + pf_e[s - 1:s] * cf[-1])
        cbk = [h0b_ref[0:1, :]]
        for s in range(nseg - 2, -1, -1):
            cbk.append(hb_e[s + 1:s + 2] + pb_e[s + 1:s + 2] * cbk[-1])
        hf0 = jnp.concatenate(cf, axis=0)
        hb0 = jnp.concatenate(cbk[::-1], axis=0)
    else:
        hf0 = h0f_ref[...]
        hb0 = h0b_ref[...]

    def scan_b(k, hb):
        j = seg - 1 - k
        hb = ab[tile(j), :] * hb + bbs[tile(j), :]
        xt[tile(j), :] = hb
        return hb

    hb_l = _loop(seg, scan_b, hb0)

    def scan_f(j, hf):
        hf = af[tile(j), :] * hf + bfs[tile(j), :]
        xs[pl.ds(j, nseg, stride=pitch), :] = hf + xt[tile(j), :]
        return hf

    hf_l = _loop(seg, scan_f, hf0)
    if chained:
        hfl_ref[...] = hf_l[nseg - 1:nseg]
        hbf_ref[...] = hb_l[0:1]
    else:
        hfl_ref[...] = hf_l
        hbf_ref[...] = hb_l

    def combine(s, carry):
        g = jax.nn.gelu(gate_ref[pl.ds(s * seg, seg), :].astype(f32))
        o_ref[pl.ds(s * seg, seg), :] = (g * xs[pl.ds(s * pitch, seg), :]).astype(bf16)
        return carry

    _loop(nseg, combine, 0)


def _lru_specs(nt, tile0, nseg, gc, pair):
    rows = LRU_SEG * nseg
    ntok = nt * rows
    nb = LRU_BLOCKS
    in_specs = [
        pl.BlockSpec((rows, LANES), lambda *g: (tile0 + pair(*g)[0], pair(*g)[1])),
        pl.BlockSpec((rows, LANES), lambda *g: (tile0 + pair(*g)[0], nb + pair(*g)[1])),
        pl.BlockSpec((4, LANES), lambda *g: (0, pair(*g)[1])),
        pl.BlockSpec((1, LANES), lambda *g: (0, pair(*g)[1])),
        pl.BlockSpec((None, LANES, 4 * LANES), lambda *g: (pair(*g)[1], 0, 0)),
        pl.BlockSpec((None, 1, 4 * LANES), lambda *g: (pair(*g)[1], 0, 0)),
        pl.BlockSpec((2, LANES), lambda *g: (0, pair(*g)[1])),
        pl.BlockSpec((None, gc, LANES), lambda *g: (tile0 + pair(*g)[0], 0, pair(*g)[1])),
        pl.BlockSpec((None, gc, LANES), lambda *g: (tile0 + pair(*g)[0], 0, pair(*g)[1])),
    ]
    out_specs = [
        pl.BlockSpec((rows, LANES), lambda *g: pair(*g)),
        pl.BlockSpec((None, gc, LANES), lambda *g: (pair(*g)[0], 0, pair(*g)[1])),
        pl.BlockSpec((None, gc, LANES), lambda *g: (pair(*g)[0], 0, pair(*g)[1])),
    ]
    out_shape = [
        jax.ShapeDtypeStruct((ntok, D_RNN), bf16),
        jax.ShapeDtypeStruct((nt, gc, D_RNN), f32),
        jax.ShapeDtypeStruct((nt, gc, D_RNN), f32),
    ]
    scr = pltpu.VMEM((rows, LANES), f32)
    scratch = [pltpu.VMEM((nseg * LRU_PITCH, LANES), f32), pltpu.VMEM((rows + 3 * nseg, LANES), f32),
               scr, scr, scr, scr]
    return in_specs, out_specs, out_shape, scratch


def _outmlp_body(x_ref, mix_ref, g1_ref, sh2_ref, sc2_ref, g2_ref, gain_ref, fgain_ref,
                 wo_ref, w1_ref, w2_ref, *, final):
    mix = jnp.dot(mix_ref[...], wo_ref[...], preferred_element_type=f32)
    x1 = x_ref[...] + g1_ref[...] * mix
    h = (_rms(x1, gain_ref[...]) * (1.0 + sc2_ref[...]) + sh2_ref[...]).astype(bf16)
    acc = None
    for j in range(0, D_FF, FF_CHUNK):
        hid = jnp.dot(h, w1_ref[:, j:j + FF_CHUNK], preferred_element_type=f32)
        hid = jnp.square(jnp.maximum(hid, 0.0)).astype(bf16)
        part = jnp.dot(hid, w2_ref[j:j + FF_CHUNK, :], preferred_element_type=f32)
        acc = part if acc is None else acc + part
    x2 = x1 + g2_ref[...] * acc
    if final:
        x2 = _rms(x2, fgain_ref[...])
    return x2


N_MLP_IN = 11
N_LRU_IN = 9


def _outmlp_kernel(*refs, final, grp, carry, lru):
    n_in = N_MLP_IN + (1 if carry else 0) + (N_LRU_IN if lru else 0)
    o_ref = refs[n_in]
    if lru is not None:
        _lru_body(*refs[n_in - N_LRU_IN:n_in], *refs[n_in + 1:], nseg=lru[0], chained=lru[1])
    x2 = _outmlp_body(*refs[:N_MLP_IN], final=final)
    if carry:
        o_ref[grp] = x2
        o_ref[1 - grp] = refs[N_MLP_IN][...]
    else:
        o_ref[...] = x2


def _outmlp_call(x4, mix3, mod4, gain, fgain, wo, w1, w2, *, layer, final, grp=0, carry=None, lru_args=None):
    ng, nb, lg, d = x4.shape
    t = TOK_TILE
    steps = lg // t
    dm = mix3.shape[-1]
    mod_spec = lambda k: pl.BlockSpec((None, None, 1, d), lambda b, i, k=k: (grp, b, 0, k))

    def resident(shape, lead=None):
        if lead is None:
            return pl.BlockSpec(shape, lambda b, i: (0, 0), pipeline_mode=pl.Buffered(1))
        return pl.BlockSpec((None,) + shape, lambda b, i: (lead, 0, 0), pipeline_mode=pl.Buffered(1))

    in_specs = [
        pl.BlockSpec((None, None, t, d), lambda b, i: (grp, b, i, 0)),
        pl.BlockSpec((None, t, dm), lambda b, i: (b, i, 0)),
        mod_spec(2), mod_spec(3), mod_spec(4), mod_spec(5),
        pl.BlockSpec((1, d), lambda b, i: (0, 0)),
        pl.BlockSpec((1, d), lambda b, i: (0, 0)),
        resident((dm, d)), resident((d, D_FF), layer), resident((D_FF, d), layer),
    ]
    args = [x4, mix3, mod4, mod4, mod4, mod4, gain.reshape(1, d), fgain.reshape(1, d), wo, w1, w2]
    if carry is None:
        out_specs = [pl.BlockSpec((None, t, d), lambda b, i: (b, i, 0))]
        out_shape = [jax.ShapeDtypeStruct((nb, lg, d), f32)]
    else:
        assert ng == 2
        in_specs.append(pl.BlockSpec((None, t, d), lambda b, i: (b, i, 0)))
        args.append(carry)
        out_specs = [pl.BlockSpec((ng, None, t, d), lambda b, i: (0, b, i, 0))]
        out_shape = [jax.ShapeDtypeStruct((ng, nb, lg, d), f32)]
    scratch = []
    lru = None
    if lru_args is not None:
        gx, cw, cb, wg, bgate, lam, h0f, h0b, nseg, chained, tile0 = lru_args
        assert nb * steps % LRU_BLOCKS == 0, "one RG-LRU (row tile, channel block) pair per MLP step"

        def pair(b, i):
            p = b * steps + i
            return p // LRU_BLOCKS, p % LRU_BLOCKS

        li, lo, ls, scratch = _lru_specs(nb * steps // LRU_BLOCKS, tile0, nseg, h0f.shape[1], pair)
        in_specs += li
        out_specs += lo
        out_shape += ls
        args += [gx, gx, cw, cb, wg, bgate, lam, h0f, h0b]
        lru = (nseg, chained)
    res = pl.pallas_call(
        functools.partial(_outmlp_kernel, final=final, grp=grp, carry=carry is not None, lru=lru),
        grid=(nb, steps),
        in_specs=in_specs,
        out_specs=out_specs,
        out_shape=out_shape,
        scratch_shapes=scratch,
        compiler_params=_cparams(("parallel", "parallel"), 58),
        name=("outmlp_final" if final else "outmlp") + ("_rglru" if lru else "") + ("_join" if carry is not None else ""),
    )(*args)
    return res if lru else res[0]


@functools.lru_cache(maxsize=None)
def _rope_tables(length):
    pos = np.arange(length)
    rowp = (pos // GRID_W).astype(np.float64)
    colp = (pos % GRID_W).astype(np.float64)
    nf = RET_DK // 4
    inv_freq = ROPE_BASE ** (-np.arange(nf, dtype=np.float64) / nf)
    ar = rowp[:, None] * inv_freq[None, :]
    ac = colp[:, None] * inv_freq[None, :]
    cos = np.concatenate([np.cos(ar), np.cos(ar), np.cos(ac), np.cos(ac)], axis=-1).astype(np.float32)
    sin = np.concatenate([-np.sin(ar), np.sin(ar), -np.sin(ac), np.sin(ac)], axis=-1).astype(np.float32)
    return cos, sin


def kernel(x_prompt, x_sample, c, c_ctx, state_ret_fwd, state_ret_bwd, state_lru_fwd, state_lru_bwd, mod_w, mod_b, norm_mix, norm_mlp, mlp_w1, mlp_w2, ab_w_in, ab_w_out, ret_decay_fwd, ret_decay_bwd, gmlp_ws, gmlp_bs, lru_w_in, lru_conv_w, lru_conv_b, lru_gate_a_w, lru_gate_a_b, lru_gate_x_w, lru_gate_x_b, lru_lambda, lru_w_out, final_norm):
    d = D_MODEL
    n_lat = c.shape[0]
    conds = jnp.concatenate([c_ctx[None, :], c, jnp.zeros((COND_ROWS - 1 - n_lat, d), f32)], axis=0)
    mod = _mod_call(conds, mod_w, mod_b)

    w_in0 = ab_w_in[0].astype(bf16)
    w_out0 = ab_w_out[0].astype(bf16)
    w1 = mlp_w1.astype(bf16)
    w2 = mlp_w2.astype(bf16)
    w_in1 = lru_w_in[0].astype(bf16)
    w_out1 = lru_w_out[0].astype(bf16)
    ws = gmlp_ws[0].astype(bf16)
    bs = gmlp_bs[0][:, :, None]
    dec = jnp.broadcast_to(jnp.stack([ret_decay_fwd[0], ret_decay_bwd[0]])[:, :, None, None],
                           (2, RET_HEADS, 1, LANES))
    wg = (0.5 * jnp.concatenate([lru_gate_a_w[0, 0], lru_gate_x_w[0, 0], lru_gate_a_w[0, 1], lru_gate_x_w[0, 1]],
                                axis=-1)).astype(bf16)
    bgate = 0.5 * jnp.stack([lru_gate_a_b[0, 0], lru_gate_x_b[0, 0], lru_gate_a_b[0, 1], lru_gate_x_b[0, 1]], axis=0)
    bgate = bgate.reshape(4, LRU_BLOCKS, LRU_BS).transpose(1, 0, 2).reshape(LRU_BLOCKS, 1, 4 * LRU_BS)
    lru_w = (lru_conv_w[0], lru_conv_b[0].reshape(1, D_RNN), wg, bgate, lru_lambda[0])

    def mixer0(x3, modl, bsz, ls, ret_f0, ret_b0, latent):
        tc = min(4, ls // RET_CHUNK)
        if latent:
            qkv, sbn = _inproj_call(x3, modl, norm_mix[0], w_in0, _rope_tables(ls), retention=True,
                                    sh_col=0, sc_col=1, bwd=(dec, ret_b0))
            mix = _mix0_call(qkv, dec, ret_f0, sbn, ws, bs, tc)[0]
            sff = sbf = None
        else:
            qkv = _inproj_call(x3, modl, norm_mix[0], w_in0, None, retention=True, sh_col=0, sc_col=1)
            mix, sff, sbf = _mix0_call(qkv.reshape(bsz, ls, AB_IN), dec, None, None, ws, bs, tc)
        return mix.reshape(x3.shape[0], x3.shape[1], AB_OUT), sff, sbf

    def lru_inproj(x3, modl):
        gx = _inproj_call(x3, modl, norm_mix[1], w_in1, None, retention=False, sh_col=0, sc_col=1)
        return gx.reshape(-1, 2 * D_RNN)

    bp, lp, _ = x_prompt.shape
    bl, ll, _ = x_sample.shape
    assert lp == LRU_SEG and ll % LRU_SEG == 0 and bl % 2 == 0
    xc = x_prompt.reshape(1, bp * lp, d)
    xl = x_sample
    mod_c = [mod[l, 0:1][:, None, :] for l in range(DEPTH)]
    mod_l = [mod[l, 1:1 + n_lat][:, None, :] for l in range(DEPTH)]
    grouped = lambda a, g: a.reshape((g, a.shape[0] // g) + a.shape[1:])

    mix_c, sff, sbf = mixer0(xc, mod_c[0], bp, lp, None, None, False)
    xc = _outmlp_call(grouped(xc, 1), mix_c, grouped(mod_c[0], 1), norm_mlp[0], final_norm, w_out0, w1, w2,
                      layer=0, final=False)
    gx_c = lru_inproj(xc, mod_c[1])
    mix_l, _, _ = mixer0(xl, mod_l[0], bl, ll, state_ret_fwd[:, 0], state_ret_bwd[:, 0], True)

    steps_l = bl * (ll // TOK_TILE)
    nseg_c = (bp * lp // LRU_SEG) * LRU_BLOCKS // steps_l
    zero_lru = jnp.zeros((bp // nseg_c, nseg_c, D_RNN), f32)
    xl, pre_c, hfl, hbf = _outmlp_call(
        grouped(xl, 1), mix_l, grouped(mod_l[0], 1), norm_mlp[0], final_norm, w_out0, w1, w2, layer=0, final=False,
        lru_args=(gx_c,) + lru_w + (zero_lru, zero_lru, nseg_c, False, 0))
    gx_l = lru_inproj(xl, mod_l[1])

    nseg_l = ll // LRU_SEG
    half = bl // 2
    h0f = state_lru_fwd[:, 0].reshape(bl, 1, D_RNN)
    h0b = state_lru_bwd[:, 0].reshape(bl, 1, D_RNN)
    assert bp * lp // TOK_TILE == half * LRU_BLOCKS == half * (ll // TOK_TILE)
    y_c, pre_a, _, _ = _outmlp_call(
        grouped(xc, 1), pre_c.reshape(1, bp * lp, D_RNN), grouped(mod_c[1], 1), norm_mlp[1], final_norm,
        w_out1, w1, w2, layer=1, final=True,
        lru_args=(gx_l,) + lru_w + (h0f, h0b, nseg_l, True, 0))
    xl2 = grouped(xl, 2)
    mod_l2 = grouped(mod_l[1], 2)
    y_a, pre_b, _, _ = _outmlp_call(
        xl2, pre_a.reshape(half, ll, D_RNN), mod_l2, norm_mlp[1], final_norm, w_out1, w1, w2,
        layer=1, final=True, grp=0,
        lru_args=(gx_l,) + lru_w + (h0f, h0b, nseg_l, True, half))
    y_l = _outmlp_call(xl2, pre_b.reshape(half, ll, D_RNN), mod_l2, norm_mlp[1], final_norm, w_out1, w1, w2,
                       layer=1, final=True, grp=1, carry=y_a)
    return (y_c.reshape(bp, lp, d), y_l.reshape(bl, ll, d), sff[:, None], sbf[:, None],
            hfl.reshape(bp, 1, D_RNN), hbf.reshape(bp, 1, D_RNN))
```

```python
import functools

import jax
import jax.numpy as jnp
import numpy as np
from jax import lax
from jax.experimental import pallas as pl
from jax.experimental.pallas import tpu as pltpu

f32 = jnp.float32
bf16 = jnp.bfloat16

D_MODEL = 1024
DEPTH = 2
GRID_W = 64
EPS = 1e-6
RET_HEADS = 4
RET_DK = 128
RET_DV = 128
RET_CHUNK = 128
ROPE_BASE = 10000.0
GMLP_GROUPS = 4
GMLP_CH = 128
AB_IN = 3072
AB_OUT = 1024
D_RNN = D_MODEL
LRU_BLOCKS = 8
LRU_BS = 128
LRU_C = 8.0
D_FF = 4 * D_MODEL

LANES = 128
SUBLANES = 8
MIB = 1024 * 1024

TOK_TILE = 512
PLAIN_MLP_TILE = 1024
INPROJ_TILE = 1024
COND_ROWS = 8
MOD_TN = 1536
LRU_SEG = 256
LRU_PITCH = LRU_SEG + SUBLANES
FF_CHUNK = 1024


def _cparams(sem, vmem_mib):
    return pltpu.CompilerParams(dimension_semantics=sem, vmem_limit_bytes=vmem_mib * MIB)


def _rms(x, gain):
    ms = jnp.mean(x * x, axis=-1, keepdims=True)
    return x * lax.rsqrt(ms + EPS) * gain


def _mod_kernel(cond_ref, w_ref, b_ref, o_ref):
    c = cond_ref[...]
    s = c * jax.nn.sigmoid(c)
    o_ref[...] = jnp.dot(s.astype(bf16), w_ref[...].astype(bf16), preferred_element_type=f32) + b_ref[...]


def _mod_call(conds, mod_w, mod_b):
    depth, d, n = mod_w.shape
    return pl.pallas_call(
        _mod_kernel,
        grid=(depth, n // MOD_TN),
        in_specs=[
            pl.BlockSpec((COND_ROWS, d), lambda l, j: (0, 0)),
            pl.BlockSpec((None, d, MOD_TN), lambda l, j: (l, 0, j)),
            pl.BlockSpec((None, 1, MOD_TN), lambda l, j: (l, 0, j)),
        ],
        out_specs=pl.BlockSpec((None, COND_ROWS, MOD_TN), lambda l, j: (l, 0, j)),
        out_shape=jax.ShapeDtypeStruct((depth, COND_ROWS, n), f32),
        compiler_params=_cparams(("parallel", "parallel"), 32),
        name="mod",
    )(conds, mod_w, mod_b.reshape(depth, 1, n))


def _rope(t, cos, sin_signed, first_half):
    up = pltpu.roll(t, LANES - 32, axis=1)
    dn = pltpu.roll(t, 32, axis=1)
    return t * cos + jnp.where(first_half, up, dn) * sin_signed


def _inproj_kernel(*refs, retention, rope, bwd_states):
    if bwd_states:
        (x_ref, sh_ref, sc_ref, gain_ref, w_ref, cos_ref, sin_ref, dec_ref, s0b_ref,
         o_ref, sbn_ref, sb_scr) = refs

        @pl.when(pl.program_id(1) == 0)
        def _():
            sb_scr[...] = s0b_ref[...]
    elif rope:
        x_ref, sh_ref, sc_ref, gain_ref, w_ref, cos_ref, sin_ref, o_ref = refs
    else:
        x_ref, sh_ref, sc_ref, gain_ref, w_ref, o_ref = refs
    h = _rms(x_ref[...], gain_ref[...]) * (1.0 + sc_ref[...]) + sh_ref[...]
    hb = h.astype(bf16)
    n = w_ref.shape[1]
    if not retention:
        for j in range(0, n, 512):
            o_ref[:, j:j + 512] = jnp.dot(hb, w_ref[:, j:j + 512], preferred_element_type=f32).astype(bf16)
        return
    qk = RET_HEADS * RET_DK
    if rope:
        cos = cos_ref[...]
        sin = sin_ref[...]
        lane = lax.broadcasted_iota(jnp.int32, cos.shape, 1)
        first_half = (lane % 64) < 32
    k_heads = []
    for part in range(2):
        y = jnp.dot(hb, w_ref[:, part * qk:(part + 1) * qk], preferred_element_type=f32)
        for hh in range(RET_HEADS):
            t = y[:, hh * RET_DK:(hh + 1) * RET_DK]
            if part == 1:
                t = t * (RET_DK ** -0.5)
            if rope:
                t = _rope(t, cos, sin, first_half)
            c0 = part * qk + hh * RET_DK
            tb = t.astype(bf16)
            o_ref[:, c0:c0 + RET_DK] = tb
            if part == 1:
                k_heads.append(tb)
    for j in range(2 * qk, n, 512):
        yb = jnp.dot(hb, w_ref[:, j:j + 512], preferred_element_type=f32).astype(bf16)
        o_ref[:, j:j + 512] = yb
        if bwd_states and j == 2 * qk:
            vb = yb
    if bwd_states:
        row = lax.broadcasted_iota(jnp.int32, (RET_CHUNK, LANES), 0).astype(f32)
        nchunk = x_ref.shape[0] // RET_CHUNK
        for hh in range(RET_HEADS):
            lg_b = jax.nn.log_sigmoid(dec_ref[1, hh])
            kd_b = jnp.exp(lg_b * row)
            gc_b = jnp.exp(lg_b * float(RET_CHUNK))
            cs = slice(hh * RET_DK, (hh + 1) * RET_DK)
            for cc in reversed(range(nchunk)):
                rs = slice(cc * RET_CHUNK, (cc + 1) * RET_CHUNK)
                s_next = sb_scr[hh]
                sbn_ref[cc, hh] = s_next.astype(bf16)
                kd = (k_heads[hh][rs].astype(f32) * kd_b).astype(bf16)
                kv = lax.dot_general(kd, vb[rs, cs], (((0,), (0,)), ((), ())), preferred_element_type=f32)
                sb_scr[hh] = gc_b * s_next + kv


def _inproj_call(x3, modl, gain, w, rope_tabs, *, retention, sh_col, sc_col, bwd=None):
    bg, lg, d = x3.shape
    n = w.shape[1]
    t = INPROJ_TILE
    ns = lg // t
    rope = rope_tabs is not None
    pos = (lambda i: ns - 1 - i) if bwd is not None else (lambda i: i)
    in_specs = [
        pl.BlockSpec((None, t, d), lambda b, i: (b, pos(i), 0)),
        pl.BlockSpec((None, 1, d), lambda b, i: (b, 0, sh_col)),
        pl.BlockSpec((None, 1, d), lambda b, i: (b, 0, sc_col)),
        pl.BlockSpec((1, d), lambda b, i: (0, 0)),
        pl.BlockSpec((d, n), lambda b, i: (0, 0), pipeline_mode=pl.Buffered(1)),
    ]
    args = [x3, modl, modl, gain.reshape(1, d), w]
    if rope:
        in_specs += [pl.BlockSpec((t, LANES), lambda b, i: (pos(i), 0))] * 2
        args += list(rope_tabs)
    out_specs = [pl.BlockSpec((None, t, n), lambda b, i: (b, pos(i), 0))]
    out_shape = [jax.ShapeDtypeStruct((bg, lg, n), bf16)]
    scratch = []
    if bwd is not None:
        st = (RET_HEADS, RET_DK, RET_DV)
        tc = t // RET_CHUNK
        in_specs += [pl.BlockSpec((2, RET_HEADS, 1, LANES), lambda b, i: (0, 0, 0, 0)),
                     pl.BlockSpec((None,) + st, lambda b, i: (b, 0, 0, 0))]
        args += list(bwd)
        out_specs.append(pl.BlockSpec((None, tc) + st, lambda b, i: (b, pos(i), 0, 0, 0)))
        out_shape.append(jax.ShapeDtypeStruct((bg, lg // RET_CHUNK) + st, bf16))
        scratch.append(pltpu.VMEM(st, f32))
    res = pl.pallas_call(
        functools.partial(_inproj_kernel, retention=retention, rope=rope, bwd_states=bwd is not None),
        grid=(bg, ns),
        in_specs=in_specs,
        out_specs=out_specs,
        out_shape=out_shape,
        scratch_shapes=scratch,
        compiler_params=_cparams(("parallel", "arbitrary" if bwd is not None else "parallel"), 48),
        name="inproj_ret" if retention else "inproj_lru",
    )(*args)
    return res if bwd is not None else res[0]


def _row_iota():
    return lax.broadcasted_iota(jnp.int32, (RET_CHUNK, LANES), 0).astype(f32)


def _mix0_kernel(*refs, tc, whole_seq):
    if whole_seq:
        dec_ref, q_ref, k_ref, v_ref, g_ref, u_ref, vv_ref, ws_ref, bs_ref, o_ref, sff_ref, sbf_ref = refs
    else:
        (dec_ref, q_ref, k_ref, v_ref, g_ref, u_ref, vv_ref, s0f_ref, sbn_ref, ws_ref, bs_ref,
         o_ref, sf_scr) = refs

        @pl.when(pl.program_id(1) == 0)
        def _():
            sf_scr[...] = s0f_ref[...]

    row = _row_iota()
    col = lax.broadcasted_iota(jnp.int32, (RET_CHUNK, LANES), 1).astype(f32)
    c = float(RET_CHUNK)
    nt = (((1,), (1,)), ((), ()))
    tn = (((0,), (0,)), ((), ()))
    for hh in range(RET_HEADS):
        lg_f = jax.nn.log_sigmoid(dec_ref[0, hh])
        lg_b = jax.nn.log_sigmoid(dec_ref[1, hh])
        decay = jnp.where(row >= col,
                          jnp.exp(lg_f * jnp.maximum(row - col, 0.0)),
                          jnp.exp(lg_b * jnp.maximum(col - row - 1.0, 0.0)))
        qd_f = jnp.exp(lg_f * (row + 1.0))
        qd_b = jnp.exp(lg_b * (c - 1.0 - row))
        cs = slice(hh * RET_DK, (hh + 1) * RET_DK)
        kd_f = jnp.exp(lg_f * (c - 1.0 - row))
        kd_b = jnp.exp(lg_b * row)
        gc_f = jnp.exp(lg_f * c)
        gc_b = jnp.exp(lg_b * c)
        s_prev, s_next = [None] * tc, [None] * tc
        sf = None if whole_seq else sf_scr[hh]
        sb = None
        for cc in range(tc):
            rs = slice(cc * RET_CHUNK, (cc + 1) * RET_CHUNK)
            kf = k_ref[rs, cs].astype(f32)
            kv = lax.dot_general((kf * kd_f).astype(bf16), v_ref[rs, cs], tn, preferred_element_type=f32)
            s_prev[cc] = sf
            sf = kv if sf is None else gc_f * sf + kv
        if whole_seq:
            for cc in reversed(range(tc)):
                rs = slice(cc * RET_CHUNK, (cc + 1) * RET_CHUNK)
                kf = k_ref[rs, cs].astype(f32)
                kv = lax.dot_general((kf * kd_b).astype(bf16), v_ref[rs, cs], tn, preferred_element_type=f32)
                s_next[cc] = sb
                sb = kv if sb is None else gc_b * sb + kv
            sff_ref[hh] = sf
            sbf_ref[hh] = sb
        else:
            sf_scr[hh] = sf
        for cc in range(tc):
            rs = slice(cc * RET_CHUNK, (cc + 1) * RET_CHUNK)
            q = q_ref[rs, cs]
            qf = q.astype(f32)
            s = lax.dot_general(q, k_ref[rs, cs], nt, preferred_element_type=f32)
            o = jnp.dot((s * decay).astype(bf16), v_ref[rs, cs], preferred_element_type=f32)
            sp_c = None if s_prev[cc] is None else s_prev[cc].astype(bf16)
            if whole_seq:
                sn_c = None if s_next[cc] is None else s_next[cc].astype(bf16)
            else:
                sn_c = sbn_ref[cc, hh]
            if sp_c is not None:
                o = o + jnp.dot((qf * qd_f).astype(bf16), sp_c, preferred_element_type=f32)
            if sn_c is not None:
                o = o + jnp.dot((qf * qd_b).astype(bf16), sn_c, preferred_element_type=f32)
            mu = jnp.mean(o, axis=-1, keepdims=True)
            var = jnp.mean(jnp.square(o - mu), axis=-1, keepdims=True)
            on = (o - mu) * lax.rsqrt(var + EPS)
            gate = g_ref[rs, cs].astype(f32)
            o_ref[rs, cs] = (gate * jax.nn.sigmoid(gate) * on).astype(bf16)
    off = RET_HEADS * RET_DV
    for gg in range(GMLP_GROUPS):
        cs = slice(gg * GMLP_CH, (gg + 1) * GMLP_CH)
        w = ws_ref[gg]
        bias = bs_ref[gg]
        for cc in range(tc):
            rs = slice(cc * RET_CHUNK, (cc + 1) * RET_CHUNK)
            gu = jax.nn.gelu(u_ref[rs, cs].astype(f32))
            gv = jax.nn.gelu(vv_ref[rs, cs].astype(f32))
            gv = gv * lax.rsqrt(jnp.mean(gv * gv, axis=-1, keepdims=True) + EPS)
            sp = jnp.dot(w, gv.astype(bf16), preferred_element_type=f32) + bias
            o_ref[rs, off + gg * GMLP_CH: off + (gg + 1) * GMLP_CH] = (gu * sp).astype(bf16)


def _mix0_call(qkv, dec, s0f, sbn, ws, bs, tc):
    bsz, ls, _ = qkv.shape
    t = tc * RET_CHUNK
    hw = RET_HEADS * RET_DK
    st = (RET_HEADS, RET_DK, RET_DV)
    col_spec = lambda j: pl.BlockSpec((None, t, hw), lambda b, i, j=j: (b, i, j))
    whole_seq = s0f is None
    in_specs = [
        pl.BlockSpec((2, RET_HEADS, 1, LANES), lambda b, i: (0, 0, 0, 0)),
        col_spec(0), col_spec(1), col_spec(2), col_spec(3), col_spec(4), col_spec(5),
    ]
    args = [dec, qkv, qkv, qkv, qkv, qkv, qkv]
    out_specs = [pl.BlockSpec((None, t, AB_OUT), lambda b, i: (b, i, 0))]
    out_shape = [jax.ShapeDtypeStruct((bsz, ls, AB_OUT), bf16)]
    if whole_seq:
        assert ls == t
        out_specs += [pl.BlockSpec((None,) + st, lambda b, i: (b, 0, 0, 0))] * 2
        out_shape += [jax.ShapeDtypeStruct((bsz,) + st, f32)] * 2
    else:
        in_specs += [pl.BlockSpec((None,) + st, lambda b, i: (b, 0, 0, 0)),
                     pl.BlockSpec((None, tc) + st, lambda b, i: (b, i, 0, 0, 0))]
        args += [s0f, sbn]
    in_specs += [
        pl.BlockSpec((GMLP_GROUPS, RET_CHUNK, RET_CHUNK), lambda b, i: (0, 0, 0)),
        pl.BlockSpec((GMLP_GROUPS, RET_CHUNK, 1), lambda b, i: (0, 0, 0)),
    ]
    args += [ws, bs]
    return pl.pallas_call(
        functools.partial(_mix0_kernel, tc=tc, whole_seq=whole_seq),
        grid=(bsz, ls // t),
        in_specs=in_specs,
        out_specs=out_specs,
        out_shape=out_shape,
        scratch_shapes=[] if whole_seq else [pltpu.VMEM(st, f32)],
        compiler_params=_cparams(("parallel", "parallel" if whole_seq else "arbitrary"), 32),
        name="mix0_seq" if whole_seq else "mix0",
    )(*args)


def _loop(n, body, carry):
    for i in range(n):
        carry = body(i, carry)
    return carry


def _lru_body(gate_ref, xin_ref, cw_ref, cb_ref, wg_ref, bg_ref, lam_ref, h0f_ref, h0b_ref,
              o_ref, hfl_ref, hbf_ref, xs, xt, af, bfs, ab, bbs, *, nseg, chained):
    seg, pitch = LRU_SEG, LRU_PITCH
    blk = seg

    def stage(s, carry):
        xs[pl.ds(s * pitch, seg), :] = xin_ref[pl.ds(s * seg, seg), :].astype(f32)
        return carry

    _loop(nseg, stage, 0)

    def tile(j):
        return pl.ds(j * nseg, nseg)

    def gather(j, carry):
        xt[tile(j + 2), :] = xs[pl.ds(j, nseg, stride=pitch), :]
        return carry

    _loop(seg, gather, 0)
    if chained:
        srow = lax.broadcasted_iota(jnp.int32, (nseg, LANES), 0)
        xt[tile(0), :] = jnp.where(srow == 0, 0.0, pltpu.roll(xt[tile(seg), :], 1, axis=0))
        xt[tile(1), :] = jnp.where(srow == 0, 0.0, pltpu.roll(xt[tile(seg + 1), :], 1, axis=0))
        xt[tile(seg + 2), :] = jnp.where(srow == nseg - 1, 0.0, pltpu.roll(xt[tile(2), :], nseg - 1, axis=0))
    else:
        zt = jnp.zeros((nseg, LANES), f32)
        xt[tile(0), :] = zt
        xt[tile(1), :] = zt
        xt[tile(seg + 2), :] = zt

    cw = cw_ref[...]
    cb = cb_ref[...]
    bg = bg_ref[...]
    sp4 = (0.5 * LRU_C) * jax.nn.softplus(-lam_ref[...])

    def gates(jb, carry):
        p0 = jb * blk
        xm2 = xt[pl.ds(p0, blk), :]
        xm1 = xt[pl.ds(p0 + nseg, blk), :]
        x0 = xt[pl.ds(p0 + 2 * nseg, blk), :]
        xp1 = xt[pl.ds(p0 + 3 * nseg, blk), :]
        xc = cb + xm2 * cw[0:1] + xm1 * cw[1:2] + x0 * cw[2:3] + xp1 * cw[3:4]
        gt = jnp.dot(xc.astype(bf16), wg_ref[...], preferred_element_type=f32) + bg
        xch = 0.5 * xc
        for d, (a_s, b_s) in enumerate(((af, bfs), (ab, bbs))):
            tr = jnp.tanh(gt[:, (2 * d) * LANES:(2 * d + 1) * LANES])
            ti = jnp.tanh(gt[:, (2 * d + 1) * LANES:(2 * d + 2) * LANES])
            nla = (tr + 1.0) * sp4[d:d + 1]
            a = jnp.exp(-nla)
            m2 = jnp.tanh(nla) * (1.0 + a * a)
            mult = m2 * lax.rsqrt(jnp.maximum(m2, 1e-30))
            a_s[pl.ds(p0, blk), :] = a
            b_s[pl.ds(p0, blk), :] = mult * ((ti + 1.0) * xch)
        return carry

    _loop(nseg, gates, 0)

    if chained:
        def ends(j, carry):
            hf, pf, hb, pb = carry
            a = af[tile(j), :]
            hf = a * hf + bfs[tile(j), :]
            pf = a * pf
            a2 = ab[tile(seg - 1 - j), :]
            hb = a2 * hb + bbs[tile(seg - 1 - j), :]
            pb = a2 * pb
            return hf, pf, hb, pb

        zero = jnp.zeros((nseg, LANES), f32)
        one = jnp.ones((nseg, LANES), f32)
        hf_e, pf_e, hb_e, pb_e = _loop(seg, ends, (zero, one, zero, one))
        cf = [h0f_ref[0:1, :]]
        for s in range(1, nseg):
            cf.append(hf_e[s - 1:s] + pf_e[s - 1:s] * cf[-1])
        cbk = [h0b_ref[0:1, :]]
        for s in range(nseg - 2, -1, -1):
            cbk.append(hb_e[s + 1:s + 2] + pb_e[s + 1:s + 2] * cbk[-1])
        hf0 = jnp.concatenate(cf, axis=0)
        hb0 = jnp.concatenate(cbk[::-1], axis=0)
    else:
        hf0 = h0f_ref[...]
        hb0 = h0b_ref[...]

    def scan_b(k, hb):
        j = seg - 1 - k
        hb = ab[tile(j), :] * hb + bbs[tile(j), :]
        xt[tile(j), :] = hb
        return hb

    hb_l = _loop(seg, scan_b, hb0)

    def scan_f(j, hf):
        hf = af[tile(j), :] * hf + bfs[tile(j), :]
        xs[pl.ds(j, nseg, stride=pitch), :] = hf + xt[tile(j), :]
        return hf

    hf_l = _loop(seg, scan_f, hf0)
    if chained:
        hfl_ref[...] = hf_l[nseg - 1:nseg]
        hbf_ref[...] = hb_l[0:1]
    else:
        hfl_ref[...] = hf_l
        hbf_ref[...] = hb_l

    def combine(s, carry):
        g = jax.nn.gelu(gate_ref[pl.ds(s * seg, seg), :].astype(f32))
        o_ref[pl.ds(s * seg, seg), :] = (g * xs[pl.ds(s * pitch, seg), :]).astype(bf16)
        return carry

    _loop(nseg, combine, 0)


def _lru_specs(ntok, nseg, gc, pair):
    rows = LRU_SEG * nseg
    nt = ntok // rows
    nb = LRU_BLOCKS
    in_specs = [
        pl.BlockSpec((rows, LANES), lambda *g: pair(*g)),
        pl.BlockSpec((rows, LANES), lambda *g: (pair(*g)[0], nb + pair(*g)[1])),
        pl.BlockSpec((4, LANES), lambda *g: (0, pair(*g)[1])),
        pl.BlockSpec((1, LANES), lambda *g: (0, pair(*g)[1])),
        pl.BlockSpec((None, LANES, 4 * LANES), lambda *g: (pair(*g)[1], 0, 0)),
        pl.BlockSpec((None, 1, 4 * LANES), lambda *g: (pair(*g)[1], 0, 0)),
        pl.BlockSpec((2, LANES), lambda *g: (0, pair(*g)[1])),
        pl.BlockSpec((None, gc, LANES), lambda *g: (pair(*g)[0], 0, pair(*g)[1])),
        pl.BlockSpec((None, gc, LANES), lambda *g: (pair(*g)[0], 0, pair(*g)[1])),
    ]
    out_specs = [
        pl.BlockSpec((rows, LANES), lambda *g: pair(*g)),
        pl.BlockSpec((None, gc, LANES), lambda *g: (pair(*g)[0], 0, pair(*g)[1])),
        pl.BlockSpec((None, gc, LANES), lambda *g: (pair(*g)[0], 0, pair(*g)[1])),
    ]
    out_shape = [
        jax.ShapeDtypeStruct((ntok, D_RNN), bf16),
        jax.ShapeDtypeStruct((nt, gc, D_RNN), f32),
        jax.ShapeDtypeStruct((nt, gc, D_RNN), f32),
    ]
    scr = pltpu.VMEM((rows, LANES), f32)
    scratch = [pltpu.VMEM((nseg * LRU_PITCH, LANES), f32), pltpu.VMEM((rows + 3 * nseg, LANES), f32),
               scr, scr, scr, scr]
    return in_specs, out_specs, out_shape, scratch


def _outmlp_body(x_ref, mix_ref, g1_ref, sh2_ref, sc2_ref, g2_ref, gain_ref, fgain_ref,
                 wo_ref, w1_ref, w2_ref, *, final):
    mix = jnp.dot(mix_ref[...], wo_ref[...], preferred_element_type=f32)
    x1 = x_ref[...] + g1_ref[...] * mix
    h = (_rms(x1, gain_ref[...]) * (1.0 + sc2_ref[...]) + sh2_ref[...]).astype(bf16)
    acc = None
    for j in range(0, D_FF, FF_CHUNK):
        hid = jnp.dot(h, w1_ref[:, j:j + FF_CHUNK], preferred_element_type=f32)
        hid = jnp.square(jnp.maximum(hid, 0.0)).astype(bf16)
        part = jnp.dot(hid, w2_ref[j:j + FF_CHUNK, :], preferred_element_type=f32)
        acc = part if acc is None else acc + part
    x2 = x1 + g2_ref[...] * acc
    if final:
        x2 = _rms(x2, fgain_ref[...])
    return x2


N_MLP_IN = 11
N_LRU_IN = 9


def _outmlp_kernel(*refs, final, lru):
    n_in = N_MLP_IN + (N_LRU_IN if lru else 0)
    if lru is not None:
        _lru_body(*refs[N_MLP_IN:n_in], *refs[n_in + 1:], nseg=lru[0], chained=lru[1])
    refs[n_in][...] = _outmlp_body(*refs[:N_MLP_IN], final=final)


def _outmlp_call(x3, mix3, modl, gain, fgain, wo, w1, w2, *, layer, final, t, lru_args=None):
    bg, lg, d = x3.shape
    steps = lg // t
    dm = mix3.shape[-1]
    mod_spec = lambda k: pl.BlockSpec((None, 1, d), lambda b, i, k=k: (b, 0, k))

    def resident(shape, lead=None):
        if lead is None:
            return pl.BlockSpec(shape, lambda b, i: (0, 0), pipeline_mode=pl.Buffered(1))
        return pl.BlockSpec((None,) + shape, lambda b, i: (lead, 0, 0), pipeline_mode=pl.Buffered(1))

    in_specs = [
        pl.BlockSpec((None, t, d), lambda b, i: (b, i, 0)),
        pl.BlockSpec((None, t, dm), lambda b, i: (b, i, 0)),
        mod_spec(2), mod_spec(3), mod_spec(4), mod_spec(5),
        pl.BlockSpec((1, d), lambda b, i: (0, 0)),
        pl.BlockSpec((1, d), lambda b, i: (0, 0)),
        resident((dm, d)), resident((d, D_FF), layer), resident((D_FF, d), layer),
    ]
    args = [x3, mix3, modl, modl, modl, modl, gain.reshape(1, d), fgain.reshape(1, d), wo, w1, w2]
    out_specs = [pl.BlockSpec((None, t, d), lambda b, i: (b, i, 0))]
    out_shape = [jax.ShapeDtypeStruct((bg, lg, d), f32)]
    scratch = []
    lru = None
    if lru_args is not None:
        gx, cw, cb, wg, bgate, lam, h0f, h0b, nseg, chained = lru_args
        ntok = gx.shape[0]
        assert (ntok // (LRU_SEG * nseg)) * LRU_BLOCKS == bg * steps, "one RG-LRU (tile, block) pair per MLP step"

        def pair(b, i):
            p = b * steps + i
            return p // LRU_BLOCKS, p % LRU_BLOCKS

        li, lo, ls, scratch = _lru_specs(ntok, nseg, h0f.shape[1], pair)
        in_specs += li
        out_specs += lo
        out_shape += ls
        args += [gx, gx, cw, cb, wg, bgate, lam, h0f, h0b]
        lru = (nseg, chained)
    res = pl.pallas_call(
        functools.partial(_outmlp_kernel, final=final, lru=lru),
        grid=(bg, steps),
        in_specs=in_specs,
        out_specs=out_specs,
        out_shape=out_shape,
        scratch_shapes=scratch,
        compiler_params=_cparams(("parallel", "parallel"), 58),
        name=("outmlp_final" if final else "outmlp") + ("_rglru" if lru else ""),
    )(*args)
    return res if lru else res[0]


@functools.lru_cache(maxsize=None)
def _rope_tables(length):
    pos = np.arange(length)
    rowp = (pos // GRID_W).astype(np.float64)
    colp = (pos % GRID_W).astype(np.float64)
    nf = RET_DK // 4
    inv_freq = ROPE_BASE ** (-np.arange(nf, dtype=np.float64) / nf)
    ar = rowp[:, None] * inv_freq[None, :]
    ac = colp[:, None] * inv_freq[None, :]
    cos = np.concatenate([np.cos(ar), np.cos(ar), np.cos(ac), np.cos(ac)], axis=-1).astype(np.float32)
    sin = np.concatenate([-np.sin(ar), np.sin(ar), -np.sin(ac), np.sin(ac)], axis=-1).astype(np.float32)
    return cos, sin


def kernel(x_prompt, x_sample, c, c_ctx, state_ret_fwd, state_ret_bwd, state_lru_fwd, state_lru_bwd, mod_w, mod_b, norm_mix, norm_mlp, mlp_w1, mlp_w2, ab_w_in, ab_w_out, ret_decay_fwd, ret_decay_bwd, gmlp_ws, gmlp_bs, lru_w_in, lru_conv_w, lru_conv_b, lru_gate_a_w, lru_gate_a_b, lru_gate_x_w, lru_gate_x_b, lru_lambda, lru_w_out, final_norm):
    d = D_MODEL
    n_lat = c.shape[0]
    conds = jnp.concatenate([c_ctx[None, :], c, jnp.zeros((COND_ROWS - 1 - n_lat, d), f32)], axis=0)
    mod = _mod_call(conds, mod_w, mod_b)

    w_in0 = ab_w_in[0].astype(bf16)
    w_out0 = ab_w_out[0].astype(bf16)
    w1 = mlp_w1.astype(bf16)
    w2 = mlp_w2.astype(bf16)
    w_in1 = lru_w_in[0].astype(bf16)
    w_out1 = lru_w_out[0].astype(bf16)
    ws = gmlp_ws[0].astype(bf16)
    bs = gmlp_bs[0][:, :, None]
    dec = jnp.broadcast_to(jnp.stack([ret_decay_fwd[0], ret_decay_bwd[0]])[:, :, None, None],
                           (2, RET_HEADS, 1, LANES))
    wg = (0.5 * jnp.concatenate([lru_gate_a_w[0, 0], lru_gate_x_w[0, 0], lru_gate_a_w[0, 1], lru_gate_x_w[0, 1]],
                                axis=-1)).astype(bf16)
    bgate = 0.5 * jnp.stack([lru_gate_a_b[0, 0], lru_gate_x_b[0, 0], lru_gate_a_b[0, 1], lru_gate_x_b[0, 1]], axis=0)
    bgate = bgate.reshape(4, LRU_BLOCKS, LRU_BS).transpose(1, 0, 2).reshape(LRU_BLOCKS, 1, 4 * LRU_BS)
    lru_w = (lru_conv_w[0], lru_conv_b[0].reshape(1, D_RNN), wg, bgate, lru_lambda[0])

    def mixer0(x3, modl, bsz, ls, ret_f0, ret_b0, latent):
        tc = min(4, ls // RET_CHUNK)
        if latent:
            qkv, sbn = _inproj_call(x3, modl, norm_mix[0], w_in0, _rope_tables(ls), retention=True,
                                    sh_col=0, sc_col=1, bwd=(dec, ret_b0))
            mix = _mix0_call(qkv, dec, ret_f0, sbn, ws, bs, tc)[0]
            sff = sbf = None
        else:
            qkv = _inproj_call(x3, modl, norm_mix[0], w_in0, None, retention=True, sh_col=0, sc_col=1)
            mix, sff, sbf = _mix0_call(qkv.reshape(bsz, ls, AB_IN), dec, None, None, ws, bs, tc)
        return mix.reshape(x3.shape[0], x3.shape[1], AB_OUT), sff, sbf

    def lru_inproj(x3, modl):
        gx = _inproj_call(x3, modl, norm_mix[1], w_in1, None, retention=False, sh_col=0, sc_col=1)
        return gx.reshape(-1, 2 * D_RNN)

    bp, lp, _ = x_prompt.shape
    bl, ll, _ = x_sample.shape
    assert lp == LRU_SEG and ll % LRU_SEG == 0
    xc = x_prompt.reshape(1, bp * lp, d)
    xl = x_sample
    mod_c = [mod[l, 0:1][:, None, :] for l in range(DEPTH)]
    mod_l = [mod[l, 1:1 + n_lat][:, None, :] for l in range(DEPTH)]

    mix_c, sff, sbf = mixer0(xc, mod_c[0], bp, lp, None, None, False)
    xc = _outmlp_call(xc, mix_c, mod_c[0], norm_mlp[0], final_norm, w_out0, w1, w2, layer=0, final=False,
                      t=PLAIN_MLP_TILE)
    gx_c = lru_inproj(xc, mod_c[1])
    mix_l, _, _ = mixer0(xl, mod_l[0], bl, ll, state_ret_fwd[:, 0], state_ret_bwd[:, 0], True)

    steps_l = bl * (ll // TOK_TILE)
    nseg_c = (bp * lp // LRU_SEG) * LRU_BLOCKS // steps_l
    zero_lru = jnp.zeros((bp // nseg_c, nseg_c, D_RNN), f32)
    xl, pre_c, hfl, hbf = _outmlp_call(
        xl, mix_l, mod_l[0], norm_mlp[0], final_norm, w_out0, w1, w2, layer=0, final=False, t=TOK_TILE,
        lru_args=(gx_c,) + lru_w + (zero_lru, zero_lru, nseg_c, False))
    gx_l = lru_inproj(xl, mod_l[1])

    nseg_l = ll // LRU_SEG
    t_c = bp * lp // (bl * LRU_BLOCKS)
    y_c, pre_l, _, _ = _outmlp_call(
        xc, pre_c.reshape(1, bp * lp, D_RNN), mod_c[1], norm_mlp[1], final_norm, w_out1, w1, w2,
        layer=1, final=True, t=t_c,
        lru_args=(gx_l,) + lru_w + (state_lru_fwd[:, 0].reshape(bl, 1, D_RNN),
                                    state_lru_bwd[:, 0].reshape(bl, 1, D_RNN), nseg_l, True))
    y_l = _outmlp_call(xl, pre_l.reshape(bl, ll, D_RNN), mod_l[1], norm_mlp[1], final_norm, w_out1, w1, w2,
                       layer=1, final=True, t=PLAIN_MLP_TILE)
    return (y_c.reshape(bp, lp, d), y_l, sff[:, None], sbf[:, None],
            hfl.reshape(bp, 1, D_RNN), hbf.reshape(bp, 1, D_RNN))
```

```python
import functools

import jax
import jax.numpy as jnp
import numpy as np
from jax import lax
from jax.experimental import pallas as pl
from jax.experimental.pallas import tpu as pltpu

f32 = jnp.float32
bf16 = jnp.bfloat16

D_MODEL = 1024
DEPTH = 2
GRID_W = 64
EPS = 1e-6
RET_HEADS = 4
RET_DK = 128
RET_DV = 128
RET_CHUNK = 128
ROPE_BASE = 10000.0
GMLP_GROUPS = 4
GMLP_CH = 128
AB_IN = 3072
AB_OUT = 1024
D_RNN = D_MODEL
LRU_BLOCKS = 8
LRU_BS = 128
LRU_C = 8.0
D_FF = 4 * D_MODEL

LANES = 128
SUBLANES = 8
MIB = 1024 * 1024

TOK_TILE = 512
PLAIN_MLP_TILE = 1024
INPROJ_TILE = 1024
COND_ROWS = 8
MOD_TN = 1536
LRU_SEG = 256
LRU_PITCH = LRU_SEG + SUBLANES
FF_CHUNK = 1024
N_BIAS_TERMS = 3
LOG2_E = 1.4426950408889634
GELU_C1 = 0.7978845608028654
GELU_C3 = 0.044715 * GELU_C1


def _cparams(sem, vmem_mib):
    return pltpu.CompilerParams(dimension_semantics=sem, vmem_limit_bytes=vmem_mib * MIB)


def _gelu(x):
    hx = 0.5 * x
    return hx + hx * jnp.tanh(x * (GELU_C1 + GELU_C3 * (x * x)))


def _rms(x, gain):
    ms = jnp.mean(x * x, axis=-1, keepdims=True)
    return x * lax.rsqrt(ms + EPS) * gain


def _mod_kernel(cond_ref, w_ref, b_ref, o_ref):
    c = cond_ref[...]
    s = c * jax.nn.sigmoid(c)
    o_ref[...] = jnp.dot(s.astype(bf16), w_ref[...].astype(bf16), preferred_element_type=f32) + b_ref[...]


def _mod_call(conds, mod_w, mod_b):
    depth, d, n = mod_w.shape
    return pl.pallas_call(
        _mod_kernel,
        grid=(depth, n // MOD_TN),
        in_specs=[
            pl.BlockSpec((COND_ROWS, d), lambda l, j: (0, 0)),
            pl.BlockSpec((None, d, MOD_TN), lambda l, j: (l, 0, j)),
            pl.BlockSpec((None, 1, MOD_TN), lambda l, j: (l, 0, j)),
        ],
        out_specs=pl.BlockSpec((None, COND_ROWS, MOD_TN), lambda l, j: (l, 0, j)),
        out_shape=jax.ShapeDtypeStruct((depth, COND_ROWS, n), f32),
        compiler_params=_cparams(("parallel", "parallel"), 32),
        name="mod",
    )(conds, mod_w, mod_b.reshape(depth, 1, n))


def _rope(t, cos, sin_signed, first_half):
    up = pltpu.roll(t, LANES - 32, axis=1)
    dn = pltpu.roll(t, 32, axis=1)
    return t * cos + jnp.where(first_half, up, dn) * sin_signed


def _inproj_kernel(*refs, retention, rope, bwd_states):
    if bwd_states:
        (x_ref, sh_ref, sc_ref, gain_ref, w_ref, cos_ref, sin_ref, dec_ref, s0b_ref,
         o_ref, sbn_ref, sb_scr) = refs

        @pl.when(pl.program_id(1) == 0)
        def _():
            sb_scr[...] = s0b_ref[...]
    elif rope:
        x_ref, sh_ref, sc_ref, gain_ref, w_ref, cos_ref, sin_ref, o_ref = refs
    else:
        x_ref, sh_ref, sc_ref, gain_ref, w_ref, o_ref = refs
    h = _rms(x_ref[...], gain_ref[...]) * (1.0 + sc_ref[...]) + sh_ref[...]
    hb = h.astype(bf16)
    n = w_ref.shape[1]
    if not retention:
        for j in range(0, n, 512):
            o_ref[:, j:j + 512] = jnp.dot(hb, w_ref[:, j:j + 512], preferred_element_type=f32).astype(bf16)
        return
    qk = RET_HEADS * RET_DK
    if rope:
        cos = cos_ref[...]
        sin = sin_ref[...]
        lane = lax.broadcasted_iota(jnp.int32, cos.shape, 1)
        first_half = (lane % 64) < 32
    k_heads = []
    for part in range(2):
        y = jnp.dot(hb, w_ref[:, part * qk:(part + 1) * qk], preferred_element_type=f32)
        for hh in range(RET_HEADS):
            t = y[:, hh * RET_DK:(hh + 1) * RET_DK]
            if part == 1:
                t = t * (RET_DK ** -0.5)
            if rope:
                t = _rope(t, cos, sin, first_half)
            c0 = part * qk + hh * RET_DK
            tb = t.astype(bf16)
            o_ref[:, c0:c0 + RET_DK] = tb
            if part == 1:
                k_heads.append(tb)
    for j in range(2 * qk, n, 512):
        yb = jnp.dot(hb, w_ref[:, j:j + 512], preferred_element_type=f32).astype(bf16)
        o_ref[:, j:j + 512] = yb
        if bwd_states and j == 2 * qk:
            vb = yb
    if bwd_states:
        row = lax.broadcasted_iota(jnp.int32, (RET_CHUNK, LANES), 0).astype(f32)
        nchunk = x_ref.shape[0] // RET_CHUNK
        for hh in range(RET_HEADS):
            lg_b = jax.nn.log_sigmoid(dec_ref[1, hh])
            kd_b = jnp.exp(lg_b * row)
            gc_b = jnp.exp(lg_b * float(RET_CHUNK))
            cs = slice(hh * RET_DK, (hh + 1) * RET_DK)
            for cc in reversed(range(nchunk)):
                rs = slice(cc * RET_CHUNK, (cc + 1) * RET_CHUNK)
                s_next = sb_scr[hh]
                sbn_ref[cc, hh] = s_next.astype(bf16)
                kd = (k_heads[hh][rs].astype(f32) * kd_b).astype(bf16)
                kv = lax.dot_general(kd, vb[rs, cs], (((0,), (0,)), ((), ())), preferred_element_type=f32)
                sb_scr[hh] = gc_b * s_next + kv


def _inproj_call(x3, modl, gain, w, rope_tabs, *, retention, sh_col, sc_col, bwd=None):
    bg, lg, d = x3.shape
    n = w.shape[1]
    t = INPROJ_TILE
    ns = lg // t
    rope = rope_tabs is not None
    pos = (lambda i: ns - 1 - i) if bwd is not None else (lambda i: i)
    in_specs = [
        pl.BlockSpec((None, t, d), lambda b, i: (b, pos(i), 0)),
        pl.BlockSpec((None, 1, d), lambda b, i: (b, 0, sh_col)),
        pl.BlockSpec((None, 1, d), lambda b, i: (b, 0, sc_col)),
        pl.BlockSpec((1, d), lambda b, i: (0, 0)),
        pl.BlockSpec((d, n), lambda b, i: (0, 0), pipeline_mode=pl.Buffered(1)),
    ]
    args = [x3, modl, modl, gain.reshape(1, d), w]
    if rope:
        in_specs += [pl.BlockSpec((t, LANES), lambda b, i: (pos(i), 0))] * 2
        args += list(rope_tabs)
    out_specs = [pl.BlockSpec((None, t, n), lambda b, i: (b, pos(i), 0))]
    out_shape = [jax.ShapeDtypeStruct((bg, lg, n), bf16)]
    scratch = []
    if bwd is not None:
        st = (RET_HEADS, RET_DK, RET_DV)
        tc = t // RET_CHUNK
        in_specs += [pl.BlockSpec((2, RET_HEADS, 1, LANES), lambda b, i: (0, 0, 0, 0)),
                     pl.BlockSpec((None,) + st, lambda b, i: (b, 0, 0, 0))]
        args += list(bwd)
        out_specs.append(pl.BlockSpec((None, tc) + st, lambda b, i: (b, pos(i), 0, 0, 0)))
        out_shape.append(jax.ShapeDtypeStruct((bg, lg // RET_CHUNK) + st, bf16))
        scratch.append(pltpu.VMEM(st, f32))
    res = pl.pallas_call(
        functools.partial(_inproj_kernel, retention=retention, rope=rope, bwd_states=bwd is not None),
        grid=(bg, ns),
        in_specs=in_specs,
        out_specs=out_specs,
        out_shape=out_shape,
        scratch_shapes=scratch,
        compiler_params=_cparams(("parallel", "arbitrary" if bwd is not None else "parallel"), 48),
        name="inproj_ret" if retention else "inproj_lru",
    )(*args)
    return res if bwd is not None else res[0]


def _row_iota():
    return lax.broadcasted_iota(jnp.int32, (RET_CHUNK, LANES), 0).astype(f32)


def _mix0_kernel(*refs, tc, whole_seq):
    if whole_seq:
        dec_ref, q_ref, k_ref, v_ref, g_ref, u_ref, vv_ref, ws_ref, bs_ref, o_ref, sff_ref, sbf_ref = refs
    else:
        (dec_ref, q_ref, k_ref, v_ref, g_ref, u_ref, vv_ref, s0f_ref, sbn_ref, ws_ref, bs_ref,
         o_ref, sf_scr) = refs

        @pl.when(pl.program_id(1) == 0)
        def _():
            sf_scr[...] = s0f_ref[...]

    row = _row_iota()
    col = lax.broadcasted_iota(jnp.int32, (RET_CHUNK, LANES), 1).astype(f32)
    c = float(RET_CHUNK)
    nt = (((1,), (1,)), ((), ()))
    tn = (((0,), (0,)), ((), ()))
    for hh in range(RET_HEADS):
        lg_f = jax.nn.log_sigmoid(dec_ref[0, hh])
        lg_b = jax.nn.log_sigmoid(dec_ref[1, hh])
        decay = jnp.where(row >= col,
                          jnp.exp(lg_f * jnp.maximum(row - col, 0.0)),
                          jnp.exp(lg_b * jnp.maximum(col - row - 1.0, 0.0)))
        qd_f = jnp.exp(lg_f * (row + 1.0))
        qd_b = jnp.exp(lg_b * (c - 1.0 - row))
        cs = slice(hh * RET_DK, (hh + 1) * RET_DK)
        kd_f = jnp.exp(lg_f * (c - 1.0 - row))
        kd_b = jnp.exp(lg_b * row)
        gc_f = jnp.exp(lg_f * c)
        gc_b = jnp.exp(lg_b * c)
        s_prev, s_next = [None] * tc, [None] * tc
        sf = None if whole_seq else sf_scr[hh]
        sb = None
        for cc in range(tc):
            rs = slice(cc * RET_CHUNK, (cc + 1) * RET_CHUNK)
            kf = k_ref[rs, cs].astype(f32)
            kv = lax.dot_general((kf * kd_f).astype(bf16), v_ref[rs, cs], tn, preferred_element_type=f32)
            s_prev[cc] = sf
            sf = kv if sf is None else gc_f * sf + kv
        if whole_seq:
            for cc in reversed(range(tc)):
                rs = slice(cc * RET_CHUNK, (cc + 1) * RET_CHUNK)
                kf = k_ref[rs, cs].astype(f32)
                kv = lax.dot_general((kf * kd_b).astype(bf16), v_ref[rs, cs], tn, preferred_element_type=f32)
                s_next[cc] = sb
                sb = kv if sb is None else gc_b * sb + kv
            sff_ref[hh] = sf
            sbf_ref[hh] = sb
        else:
            sf_scr[hh] = sf
        for cc in range(tc):
            rs = slice(cc * RET_CHUNK, (cc + 1) * RET_CHUNK)
            q = q_ref[rs, cs]
            qf = q.astype(f32)
            s = lax.dot_general(q, k_ref[rs, cs], nt, preferred_element_type=f32)
            o = jnp.dot((s * decay).astype(bf16), v_ref[rs, cs], preferred_element_type=f32)
            sp_c = None if s_prev[cc] is None else s_prev[cc].astype(bf16)
            if whole_seq:
                sn_c = None if s_next[cc] is None else s_next[cc].astype(bf16)
            else:
                sn_c = sbn_ref[cc, hh]
            if sp_c is not None:
                o = o + jnp.dot((qf * qd_f).astype(bf16), sp_c, preferred_element_type=f32)
            if sn_c is not None:
                o = o + jnp.dot((qf * qd_b).astype(bf16), sn_c, preferred_element_type=f32)
            mu = jnp.mean(o, axis=-1, keepdims=True)
            var = jnp.mean(jnp.square(o - mu), axis=-1, keepdims=True)
            on = (o - mu) * lax.rsqrt(var + EPS)
            gate = g_ref[rs, cs].astype(f32)
            o_ref[rs, cs] = (gate * jax.nn.sigmoid(gate) * on).astype(bf16)
    off = RET_HEADS * RET_DV
    for gg in range(GMLP_GROUPS):
        cs = slice(gg * GMLP_CH, (gg + 1) * GMLP_CH)
        w = ws_ref[gg]
        bias = bs_ref[gg]
        for cc in range(tc):
            rs = slice(cc * RET_CHUNK, (cc + 1) * RET_CHUNK)
            gu = _gelu(u_ref[rs, cs].astype(f32))
            gv = _gelu(vv_ref[rs, cs].astype(f32))
            gv = gv * lax.rsqrt(jnp.mean(gv * gv, axis=-1, keepdims=True) + EPS)
            sp = jnp.dot(w, gv.astype(bf16), preferred_element_type=f32) + bias
            o_ref[rs, off + gg * GMLP_CH: off + (gg + 1) * GMLP_CH] = (gu * sp).astype(bf16)


def _mix0_call(qkv, dec, s0f, sbn, ws, bs, tc):
    bsz, ls, _ = qkv.shape
    t = tc * RET_CHUNK
    hw = RET_HEADS * RET_DK
    st = (RET_HEADS, RET_DK, RET_DV)
    col_spec = lambda j: pl.BlockSpec((None, t, hw), lambda b, i, j=j: (b, i, j))
    whole_seq = s0f is None
    in_specs = [
        pl.BlockSpec((2, RET_HEADS, 1, LANES), lambda b, i: (0, 0, 0, 0)),
        col_spec(0), col_spec(1), col_spec(2), col_spec(3), col_spec(4), col_spec(5),
    ]
    args = [dec, qkv, qkv, qkv, qkv, qkv, qkv]
    out_specs = [pl.BlockSpec((None, t, AB_OUT), lambda b, i: (b, i, 0))]
    out_shape = [jax.ShapeDtypeStruct((bsz, ls, AB_OUT), bf16)]
    if whole_seq:
        assert ls == t
        out_specs += [pl.BlockSpec((None,) + st, lambda b, i: (b, 0, 0, 0))] * 2
        out_shape += [jax.ShapeDtypeStruct((bsz,) + st, f32)] * 2
    else:
        in_specs += [pl.BlockSpec((None,) + st, lambda b, i: (b, 0, 0, 0)),
                     pl.BlockSpec((None, tc) + st, lambda b, i: (b, i, 0, 0, 0))]
        args += [s0f, sbn]
    in_specs += [
        pl.BlockSpec((GMLP_GROUPS, RET_CHUNK, RET_CHUNK), lambda b, i: (0, 0, 0)),
        pl.BlockSpec((GMLP_GROUPS, RET_CHUNK, 1), lambda b, i: (0, 0, 0)),
    ]
    args += [ws, bs]
    return pl.pallas_call(
        functools.partial(_mix0_kernel, tc=tc, whole_seq=whole_seq),
        grid=(bsz, ls // t),
        in_specs=in_specs,
        out_specs=out_specs,
        out_shape=out_shape,
        scratch_shapes=[] if whole_seq else [pltpu.VMEM(st, f32)],
        compiler_params=_cparams(("parallel", "parallel" if whole_seq else "arbitrary"), 32),
        name="mix0_seq" if whole_seq else "mix0",
    )(*args)


def _loop(n, body, carry):
    for i in range(n):
        carry = body(i, carry)
    return carry


def _lru_body(gate_ref, xin_ref, cw_ref, cb_ref, wg_ref, lam_ref, h0f_ref, h0b_ref,
              o_ref, hfl_ref, hbf_ref, xs, xt, af, bfs, ab, bbs, *, nseg, chained):
    seg, pitch = LRU_SEG, LRU_PITCH
    blk = seg

    def stage(s, carry):
        xs[pl.ds(s * pitch, seg), :] = xin_ref[pl.ds(s * seg, seg), :].astype(f32)
        return carry

    _loop(nseg, stage, 0)

    def tile(j):
        return pl.ds(j * nseg, nseg)

    def gather(j, carry):
        xt[tile(j + 2), :] = xs[pl.ds(j, nseg, stride=pitch), :]
        return carry

    _loop(seg, gather, 0)
    if chained:
        srow = lax.broadcasted_iota(jnp.int32, (nseg, LANES), 0)
        xt[tile(0), :] = jnp.where(srow == 0, 0.0, pltpu.roll(xt[tile(seg), :], 1, axis=0))
        xt[tile(1), :] = jnp.where(srow == 0, 0.0, pltpu.roll(xt[tile(seg + 1), :], 1, axis=0))
        xt[tile(seg + 2), :] = jnp.where(srow == nseg - 1, 0.0, pltpu.roll(xt[tile(2), :], nseg - 1, axis=0))
    else:
        zt = jnp.zeros((nseg, LANES), f32)
        xt[tile(0), :] = zt
        xt[tile(1), :] = zt
        xt[tile(seg + 2), :] = zt

    cw = cw_ref[...]
    cb = cb_ref[...]
    sp4 = (0.5 * LRU_C) * jax.nn.softplus(-lam_ref[...])
    sp4e = sp4 * (-LOG2_E)
    ones = (lax.broadcasted_iota(jnp.int32, (blk, LANES), 1) < N_BIAS_TERMS).astype(bf16)

    def gates(jb, carry):
        p0 = jb * blk
        xm2 = xt[pl.ds(p0, blk), :]
        xm1 = xt[pl.ds(p0 + nseg, blk), :]
        x0 = xt[pl.ds(p0 + 2 * nseg, blk), :]
        xp1 = xt[pl.ds(p0 + 3 * nseg, blk), :]
        xc = cb + xm2 * cw[0:1] + xm1 * cw[1:2] + x0 * cw[2:3] + xp1 * cw[3:4]
        lhs = jnp.concatenate([xc.astype(bf16), ones], axis=1)
        gt = jnp.dot(lhs, wg_ref[...], preferred_element_type=f32)
        xch = 0.5 * xc
        for d, (a_s, b_s) in enumerate(((af, bfs), (ab, bbs))):
            tr = jnp.tanh(gt[:, (2 * d) * LANES:(2 * d + 1) * LANES])
            ti = jnp.tanh(gt[:, (2 * d + 1) * LANES:(2 * d + 2) * LANES])
            r2 = tr + 1.0
            nla = r2 * sp4[d:d + 1]
            a = jnp.exp2(r2 * sp4e[d:d + 1])
            m2 = jnp.tanh(nla) * (1.0 + a * a)
            mult = m2 * lax.rsqrt(jnp.maximum(m2, 1e-30))
            a_s[pl.ds(p0, blk), :] = a
            b_s[pl.ds(p0, blk), :] = mult * ((ti + 1.0) * xch)
        return carry

    _loop(nseg, gates, 0)

    if chained:
        def ends(j, carry):
            hf, pf, hb, pb = carry
            a = af[tile(j), :]
            hf = a * hf + bfs[tile(j), :]
            pf = a * pf
            a2 = ab[tile(seg - 1 - j), :]
            hb = a2 * hb + bbs[tile(seg - 1 - j), :]
            pb = a2 * pb
            return hf, pf, hb, pb

        zero = jnp.zeros((nseg, LANES), f32)
        one = jnp.ones((nseg, LANES), f32)
        hf_e, pf_e, hb_e, pb_e = _loop(seg, ends, (zero, one, zero, one))
        cf = [h0f_ref[0:1, :]]
        for s in range(1, nseg):
            cf.append(hf_e[s - 1:s] + pf_e[s - 1:s] * cf[-1])
        cbk = [h0b_ref[0:1, :]]
        for s in range(nseg - 2, -1, -1):
            cbk.append(hb_e[s + 1:s + 2] + pb_e[s + 1:s + 2] * cbk[-1])
        hf0 = jnp.concatenate(cf, axis=0)
        hb0 = jnp.concatenate(cbk[::-1], axis=0)
    else:
        hf0 = h0f_ref[...]
        hb0 = h0b_ref[...]

    def scan_b(k, hb):
        j = seg - 1 - k
        hb = ab[tile(j), :] * hb + bbs[tile(j), :]
        xt[tile(j), :] = hb
        return hb

    hb_l = _loop(seg, scan_b, hb0)

    def scan_f(j, hf):
        hf = af[tile(j), :] * hf + bfs[tile(j), :]
        xs[pl.ds(j, nseg, stride=pitch), :] = hf + xt[tile(j), :]
        return hf

    hf_l = _loop(seg, scan_f, hf0)
    if chained:
        hfl_ref[...] = hf_l[nseg - 1:nseg]
        hbf_ref[...] = hb_l[0:1]
    else:
        hfl_ref[...] = hf_l
        hbf_ref[...] = hb_l

    def combine(s, carry):
        g = _gelu(gate_ref[pl.ds(s * seg, seg), :].astype(f32))
        o_ref[pl.ds(s * seg, seg), :] = (g * xs[pl.ds(s * pitch, seg), :]).astype(bf16)
        return carry

    _loop(nseg, combine, 0)


def _lru_specs(ntok, nseg, gc, pair):
    rows = LRU_SEG * nseg
    nt = ntok // rows
    nb = LRU_BLOCKS
    in_specs = [
        pl.BlockSpec((rows, LANES), lambda *g: pair(*g)),
        pl.BlockSpec((rows, LANES), lambda *g: (pair(*g)[0], nb + pair(*g)[1])),
        pl.BlockSpec((4, LANES), lambda *g: (0, pair(*g)[1])),
        pl.BlockSpec((1, LANES), lambda *g: (0, pair(*g)[1])),
        pl.BlockSpec((None, 2 * LANES, 4 * LANES), lambda *g: (pair(*g)[1], 0, 0)),
        pl.BlockSpec((2, LANES), lambda *g: (0, pair(*g)[1])),
        pl.BlockSpec((None, gc, LANES), lambda *g: (pair(*g)[0], 0, pair(*g)[1])),
        pl.BlockSpec((None, gc, LANES), lambda *g: (pair(*g)[0], 0, pair(*g)[1])),
    ]
    out_specs = [
        pl.BlockSpec((rows, LANES), lambda *g: pair(*g)),
        pl.BlockSpec((None, gc, LANES), lambda *g: (pair(*g)[0], 0, pair(*g)[1])),
        pl.BlockSpec((None, gc, LANES), lambda *g: (pair(*g)[0], 0, pair(*g)[1])),
    ]
    out_shape = [
        jax.ShapeDtypeStruct((ntok, D_RNN), bf16),
        jax.ShapeDtypeStruct((nt, gc, D_RNN), f32),
        jax.ShapeDtypeStruct((nt, gc, D_RNN), f32),
    ]
    scr = pltpu.VMEM((rows, LANES), f32)
    scratch = [pltpu.VMEM((nseg * LRU_PITCH, LANES), f32), pltpu.VMEM((rows + 3 * nseg, LANES), f32),
               scr, scr, scr, scr]
    return in_specs, out_specs, out_shape, scratch


def _outmlp_body(x_ref, mix_ref, g1_ref, sh2_ref, sc2_ref, g2_ref, gain_ref, fgain_ref,
                 wo_ref, w1_ref, w2_ref, *, final):
    mix = jnp.dot(mix_ref[...], wo_ref[...], preferred_element_type=f32)
    x1 = x_ref[...] + g1_ref[...] * mix
    h = (_rms(x1, gain_ref[...]) * (1.0 + sc2_ref[...]) + sh2_ref[...]).astype(bf16)
    acc = None
    for j in range(0, D_FF, FF_CHUNK):
        hid = jnp.dot(h, w1_ref[:, j:j + FF_CHUNK], preferred_element_type=f32)
        hid = jnp.square(jnp.maximum(hid, 0.0)).astype(bf16)
        part = jnp.dot(hid, w2_ref[j:j + FF_CHUNK, :], preferred_element_type=f32)
        acc = part if acc is None else acc + part
    x2 = x1 + g2_ref[...] * acc
    if final:
        x2 = _rms(x2, fgain_ref[...])
    return x2


N_MLP_IN = 11
N_LRU_IN = 8


def _outmlp_kernel(*refs, final, lru):
    n_in = N_MLP_IN + (N_LRU_IN if lru else 0)
    if lru is not None:
        _lru_body(*refs[N_MLP_IN:n_in], *refs[n_in + 1:], nseg=lru[0], chained=lru[1])
    refs[n_in][...] = _outmlp_body(*refs[:N_MLP_IN], final=final)


def _outmlp_call(x3, mix3, modl, gain, fgain, wo, w1, w2, *, layer, final, t, lru_args=None):
    bg, lg, d = x3.shape
    steps = lg // t
    dm = mix3.shape[-1]
    mod_spec = lambda k: pl.BlockSpec((None, 1, d), lambda b, i, k=k: (b, 0, k))

    def resident(shape, lead=None):
        if lead is None:
            return pl.BlockSpec(shape, lambda b, i: (0, 0), pipeline_mode=pl.Buffered(1))
        return pl.BlockSpec((None,) + shape, lambda b, i: (lead, 0, 0), pipeline_mode=pl.Buffered(1))

    in_specs = [
        pl.BlockSpec((None, t, d), lambda b, i: (b, i, 0)),
        pl.BlockSpec((None, t, dm), lambda b, i: (b, i, 0)),
        mod_spec(2), mod_spec(3), mod_spec(4), mod_spec(5),
        pl.BlockSpec((1, d), lambda b, i: (0, 0)),
        pl.BlockSpec((1, d), lambda b, i: (0, 0)),
        resident((dm, d)), resident((d, D_FF), layer), resident((D_FF, d), layer),
    ]
    args = [x3, mix3, modl, modl, modl, modl, gain.reshape(1, d), fgain.reshape(1, d), wo, w1, w2]
    out_specs = [pl.BlockSpec((None, t, d), lambda b, i: (b, i, 0))]
    out_shape = [jax.ShapeDtypeStruct((bg, lg, d), f32)]
    scratch = []
    lru = None
    if lru_args is not None:
        gx, cw, cb, wg, lam, h0f, h0b, nseg, chained = lru_args
        ntok = gx.shape[0]
        assert (ntok // (LRU_SEG * nseg)) * LRU_BLOCKS == bg * steps, "one RG-LRU (tile, block) pair per MLP step"

        def pair(b, i):
            p = b * steps + i
            return p // LRU_BLOCKS, p % LRU_BLOCKS

        li, lo, ls, scratch = _lru_specs(ntok, nseg, h0f.shape[1], pair)
        in_specs += li
        out_specs += lo
        out_shape += ls
        args += [gx, gx, cw, cb, wg, lam, h0f, h0b]
        lru = (nseg, chained)
    res = pl.pallas_call(
        functools.partial(_outmlp_kernel, final=final, lru=lru),
        grid=(bg, steps),
        in_specs=in_specs,
        out_specs=out_specs,
        out_shape=out_shape,
        scratch_shapes=scratch,
        compiler_params=_cparams(("parallel", "parallel"), 58),
        name=("outmlp_final" if final else "outmlp") + ("_rglru" if lru else ""),
    )(*args)
    return res if lru else res[0]


@functools.lru_cache(maxsize=None)
def _rope_tables(length):
    pos = np.arange(length)
    rowp = (pos // GRID_W).astype(np.float64)
    colp = (pos % GRID_W).astype(np.float64)
    nf = RET_DK // 4
    inv_freq = ROPE_BASE ** (-np.arange(nf, dtype=np.float64) / nf)
    ar = rowp[:, None] * inv_freq[None, :]
    ac = colp[:, None] * inv_freq[None, :]
    cos = np.concatenate([np.cos(ar), np.cos(ar), np.cos(ac), np.cos(ac)], axis=-1).astype(np.float32)
    sin = np.concatenate([-np.sin(ar), np.sin(ar), -np.sin(ac), np.sin(ac)], axis=-1).astype(np.float32)
    return cos, sin


def kernel(x_prompt, x_sample, c, c_ctx, state_ret_fwd, state_ret_bwd, state_lru_fwd, state_lru_bwd, mod_w, mod_b, norm_mix, norm_mlp, mlp_w1, mlp_w2, ab_w_in, ab_w_out, ret_decay_fwd, ret_decay_bwd, gmlp_ws, gmlp_bs, lru_w_in, lru_conv_w, lru_conv_b, lru_gate_a_w, lru_gate_a_b, lru_gate_x_w, lru_gate_x_b, lru_lambda, lru_w_out, final_norm):
    d = D_MODEL
    n_lat = c.shape[0]
    conds = jnp.concatenate([c_ctx[None, :], c, jnp.zeros((COND_ROWS - 1 - n_lat, d), f32)], axis=0)
    mod = _mod_call(conds, mod_w, mod_b)

    w_in0 = ab_w_in[0].astype(bf16)
    w_out0 = ab_w_out[0].astype(bf16)
    w1 = mlp_w1.astype(bf16)
    w2 = mlp_w2.astype(bf16)
    w_in1 = lru_w_in[0].astype(bf16)
    w_out1 = lru_w_out[0].astype(bf16)
    ws = gmlp_ws[0].astype(bf16)
    bs = gmlp_bs[0][:, :, None]
    dec = jnp.broadcast_to(jnp.stack([ret_decay_fwd[0], ret_decay_bwd[0]])[:, :, None, None],
                           (2, RET_HEADS, 1, LANES))
    wg = (0.5 * jnp.concatenate([lru_gate_a_w[0, 0], lru_gate_x_w[0, 0], lru_gate_a_w[0, 1], lru_gate_x_w[0, 1]],
                                axis=-1)).astype(bf16)
    bgate = 0.5 * jnp.stack([lru_gate_a_b[0, 0], lru_gate_x_b[0, 0], lru_gate_a_b[0, 1], lru_gate_x_b[0, 1]], axis=0)
    bgate = bgate.reshape(4, LRU_BLOCKS, LRU_BS).transpose(1, 0, 2).reshape(LRU_BLOCKS, 1, 4 * LRU_BS)
    terms, rem = [], bgate
    for _ in range(N_BIAS_TERMS):
        terms.append(rem.astype(bf16))
        rem = rem - terms[-1].astype(f32)
    wg = jnp.concatenate([wg] + terms + [jnp.zeros((LRU_BLOCKS, LANES - N_BIAS_TERMS, 4 * LRU_BS), bf16)], axis=1)
    lru_w = (lru_conv_w[0], lru_conv_b[0].reshape(1, D_RNN), wg, lru_lambda[0])

    def mixer0(x3, modl, bsz, ls, ret_f0, ret_b0, latent):
        tc = min(4, ls // RET_CHUNK)
        if latent:
            qkv, sbn = _inproj_call(x3, modl, norm_mix[0], w_in0, _rope_tables(ls), retention=True,
                                    sh_col=0, sc_col=1, bwd=(dec, ret_b0))
            mix = _mix0_call(qkv, dec, ret_f0, sbn, ws, bs, tc)[0]
            sff = sbf = None
        else:
            qkv = _inproj_call(x3, modl, norm_mix[0], w_in0, None, retention=True, sh_col=0, sc_col=1)
            mix, sff, sbf = _mix0_call(qkv.reshape(bsz, ls, AB_IN), dec, None, None, ws, bs, tc)
        return mix.reshape(x3.shape[0], x3.shape[1], AB_OUT), sff, sbf

    def lru_inproj(x3, modl):
        gx = _inproj_call(x3, modl, norm_mix[1], w_in1, None, retention=False, sh_col=0, sc_col=1)
        return gx.reshape(-1, 2 * D_RNN)

    bp, lp, _ = x_prompt.shape
    bl, ll, _ = x_sample.shape
    assert lp == LRU_SEG and ll % LRU_SEG == 0
    xc = x_prompt.reshape(1, bp * lp, d)
    xl = x_sample
    mod_c = [mod[l, 0:1][:, None, :] for l in range(DEPTH)]
    mod_l = [mod[l, 1:1 + n_lat][:, None, :] for l in range(DEPTH)]

    mix_c, sff, sbf = mixer0(xc, mod_c[0], bp, lp, None, None, False)
    xc = _outmlp_call(xc, mix_c, mod_c[0], norm_mlp[0], final_norm, w_out0, w1, w2, layer=0, final=False,
                      t=PLAIN_MLP_TILE)
    gx_c = lru_inproj(xc, mod_c[1])
    mix_l, _, _ = mixer0(xl, mod_l[0], bl, ll, state_ret_fwd[:, 0], state_ret_bwd[:, 0], True)

    steps_l = bl * (ll // TOK_TILE)
    nseg_c = (bp * lp // LRU_SEG) * LRU_BLOCKS // steps_l
    zero_lru = jnp.zeros((bp // nseg_c, nseg_c, D_RNN), f32)
    xl, pre_c, hfl, hbf = _outmlp_call(
        xl, mix_l, mod_l[0], norm_mlp[0], final_norm, w_out0, w1, w2, layer=0, final=False, t=TOK_TILE,
        lru_args=(gx_c,) + lru_w + (zero_lru, zero_lru, nseg_c, False))
    gx_l = lru_inproj(xl, mod_l[1])

    nseg_l = ll // LRU_SEG
    t_c = bp * lp // (bl * LRU_BLOCKS)
    y_c, pre_l, _, _ = _outmlp_call(
        xc, pre_c.reshape(1, bp * lp, D_RNN), mod_c[1], norm_mlp[1], final_norm, w_out1, w1, w2,
        layer=1, final=True, t=t_c,
        lru_args=(gx_l,) + lru_w + (state_lru_fwd[:, 0].reshape(bl, 1, D_RNN),
                                    state_lru_bwd[:, 0].reshape(bl, 1, D_RNN), nseg_l, True))
    y_l = _outmlp_call(xl, pre_l.reshape(bl, ll, D_RNN), mod_l[1], norm_mlp[1], final_norm, w_out1, w1, w2,
                       layer=1, final=True, t=PLAIN_MLP_TILE)
    return (y_c.reshape(bp, lp, d), y_l, sff[:, None], sbf[:, None],
            hfl.reshape(bp, 1, D_RNN), hbf.reshape(bp, 1, D_RNN))
```

```python
import functools

import jax
import jax.numpy as jnp
import numpy as np
from jax import lax
from jax.experimental import pallas as pl
from jax.experimental.pallas import tpu as pltpu

f32 = jnp.float32
bf16 = jnp.bfloat16

D_MODEL = 1024
DEPTH = 2
GRID_W = 64
EPS = 1e-6
RET_HEADS = 4
RET_DK = 128
RET_DV = 128
RET_CHUNK = 128
ROPE_BASE = 10000.0
GMLP_GROUPS = 4
GMLP_CH = 128
AB_IN = 3072
AB_OUT = 1024
D_RNN = D_MODEL
LRU_BLOCKS = 8
LRU_BS = 128
LRU_C = 8.0
D_FF = 4 * D_MODEL

LANES = 128
SUBLANES = 8
MIB = 1024 * 1024

TOK_TILE = 512
PLAIN_MLP_TILE = 1024
INPROJ_TILE = 1024
COND_ROWS = 8
MOD_TN = 1536
LRU_SEG = 256
LRU_PITCH = LRU_SEG + SUBLANES
FF_CHUNK = 1024
N_BIAS_TERMS = 3
LOG2_E = 1.4426950408889634
GELU_C1 = 0.7978845608028654
GELU_C3 = 0.044715 * GELU_C1


def _cparams(sem, vmem_mib):
    return pltpu.CompilerParams(dimension_semantics=sem, vmem_limit_bytes=vmem_mib * MIB)


def _gelu(x):
    hx = 0.5 * x
    return hx + hx * jnp.tanh(x * (GELU_C1 + GELU_C3 * (x * x)))


def _silu(x):
    hx = 0.5 * x
    return hx + hx * jnp.tanh(hx)


def _rms(x, gain):
    ms = jnp.mean(x * x, axis=-1, keepdims=True)
    return x * lax.rsqrt(ms + EPS) * gain


def _mod_kernel(cond_ref, w_ref, b_ref, o_ref):
    c = cond_ref[...]
    s = c * jax.nn.sigmoid(c)
    o_ref[...] = jnp.dot(s.astype(bf16), w_ref[...].astype(bf16), preferred_element_type=f32) + b_ref[...]


def _mod_call(conds, mod_w, mod_b):
    depth, d, n = mod_w.shape
    return pl.pallas_call(
        _mod_kernel,
        grid=(depth, n // MOD_TN),
        in_specs=[
            pl.BlockSpec((COND_ROWS, d), lambda l, j: (0, 0)),
            pl.BlockSpec((None, d, MOD_TN), lambda l, j: (l, 0, j)),
            pl.BlockSpec((None, 1, MOD_TN), lambda l, j: (l, 0, j)),
        ],
        out_specs=pl.BlockSpec((None, COND_ROWS, MOD_TN), lambda l, j: (l, 0, j)),
        out_shape=jax.ShapeDtypeStruct((depth, COND_ROWS, n), f32),
        compiler_params=_cparams(("parallel", "parallel"), 32),
        name="mod",
    )(conds, mod_w, mod_b.reshape(depth, 1, n))


def _rope(t, cos, sin_signed, first_half):
    up = pltpu.roll(t, LANES - 32, axis=1)
    dn = pltpu.roll(t, 32, axis=1)
    return t * cos + jnp.where(first_half, up, dn) * sin_signed


def _inproj_kernel(*refs, retention, rope, bwd_states):
    if bwd_states:
        (x_ref, sh_ref, sc_ref, gain_ref, w_ref, cos_ref, sin_ref, dec_ref, s0b_ref,
         o_ref, sbn_ref, sb_scr) = refs

        @pl.when(pl.program_id(1) == 0)
        def _():
            sb_scr[...] = s0b_ref[...]
    elif rope:
        x_ref, sh_ref, sc_ref, gain_ref, w_ref, cos_ref, sin_ref, o_ref = refs
    else:
        x_ref, sh_ref, sc_ref, gain_ref, w_ref, o_ref = refs
    h = _rms(x_ref[...], gain_ref[...]) * (1.0 + sc_ref[...]) + sh_ref[...]
    hb = h.astype(bf16)
    n = w_ref.shape[1]
    if not retention:
        for j in range(0, n, 512):
            y = jnp.dot(hb, w_ref[:, j:j + 512], preferred_element_type=f32)
            o_ref[:, j:j + 512] = (_gelu(y) if j < n // 2 else y).astype(bf16)
        return
    qk = RET_HEADS * RET_DK
    if rope:
        cos = cos_ref[...]
        sin = sin_ref[...]
        lane = lax.broadcasted_iota(jnp.int32, cos.shape, 1)
        first_half = (lane % 64) < 32
    k_heads = []
    for part in range(2):
        y = jnp.dot(hb, w_ref[:, part * qk:(part + 1) * qk], preferred_element_type=f32)
        for hh in range(RET_HEADS):
            t = y[:, hh * RET_DK:(hh + 1) * RET_DK]
            if part == 1:
                t = t * (RET_DK ** -0.5)
            if rope:
                t = _rope(t, cos, sin, first_half)
            c0 = part * qk + hh * RET_DK
            tb = t.astype(bf16)
            o_ref[:, c0:c0 + RET_DK] = tb
            if part == 1:
                k_heads.append(tb)
    acts = (None, _silu, _gelu, _gelu)
    for j, act in zip(range(2 * qk, n, 512), acts):
        y = jnp.dot(hb, w_ref[:, j:j + 512], preferred_element_type=f32)
        yb = (y if act is None else act(y)).astype(bf16)
        o_ref[:, j:j + 512] = yb
        if bwd_states and j == 2 * qk:
            vb = yb
    if bwd_states:
        row = lax.broadcasted_iota(jnp.int32, (RET_CHUNK, LANES), 0).astype(f32)
        nchunk = x_ref.shape[0] // RET_CHUNK
        for hh in range(RET_HEADS):
            lg_b = jax.nn.log_sigmoid(dec_ref[1, hh])
            kd_b = jnp.exp(lg_b * row)
            gc_b = jnp.exp(lg_b * float(RET_CHUNK))
            cs = slice(hh * RET_DK, (hh + 1) * RET_DK)
            for cc in reversed(range(nchunk)):
                rs = slice(cc * RET_CHUNK, (cc + 1) * RET_CHUNK)
                s_next = sb_scr[hh]
                sbn_ref[cc, hh] = s_next.astype(bf16)
                kd = (k_heads[hh][rs].astype(f32) * kd_b).astype(bf16)
                kv = lax.dot_general(kd, vb[rs, cs], (((0,), (0,)), ((), ())), preferred_element_type=f32)
                sb_scr[hh] = gc_b * s_next + kv


def _inproj_call(x3, modl, gain, w, rope_tabs, *, retention, sh_col, sc_col, bwd=None):
    bg, lg, d = x3.shape
    n = w.shape[1]
    t = INPROJ_TILE
    ns = lg // t
    rope = rope_tabs is not None
    pos = (lambda i: ns - 1 - i) if bwd is not None else (lambda i: i)
    in_specs = [
        pl.BlockSpec((None, t, d), lambda b, i: (b, pos(i), 0)),
        pl.BlockSpec((None, 1, d), lambda b, i: (b, 0, sh_col)),
        pl.BlockSpec((None, 1, d), lambda b, i: (b, 0, sc_col)),
        pl.BlockSpec((1, d), lambda b, i: (0, 0)),
        pl.BlockSpec((d, n), lambda b, i: (0, 0), pipeline_mode=pl.Buffered(1)),
    ]
    args = [x3, modl, modl, gain.reshape(1, d), w]
    if rope:
        in_specs += [pl.BlockSpec((t, LANES), lambda b, i: (pos(i), 0))] * 2
        args += list(rope_tabs)
    out_specs = [pl.BlockSpec((None, t, n), lambda b, i: (b, pos(i), 0))]
    out_shape = [jax.ShapeDtypeStruct((bg, lg, n), bf16)]
    scratch = []
    if bwd is not None:
        st = (RET_HEADS, RET_DK, RET_DV)
        tc = t // RET_CHUNK
        in_specs += [pl.BlockSpec((2, RET_HEADS, 1, LANES), lambda b, i: (0, 0, 0, 0)),
                     pl.BlockSpec((None,) + st, lambda b, i: (b, 0, 0, 0))]
        args += list(bwd)
        out_specs.append(pl.BlockSpec((None, tc) + st, lambda b, i: (b, pos(i), 0, 0, 0)))
        out_shape.append(jax.ShapeDtypeStruct((bg, lg // RET_CHUNK) + st, bf16))
        scratch.append(pltpu.VMEM(st, f32))
    res = pl.pallas_call(
        functools.partial(_inproj_kernel, retention=retention, rope=rope, bwd_states=bwd is not None),
        grid=(bg, ns),
        in_specs=in_specs,
        out_specs=out_specs,
        out_shape=out_shape,
        scratch_shapes=scratch,
        compiler_params=_cparams(("parallel", "arbitrary" if bwd is not None else "parallel"), 48),
        name="inproj_ret" if retention else "inproj_lru",
    )(*args)
    return res if bwd is not None else res[0]


def _row_iota():
    return lax.broadcasted_iota(jnp.int32, (RET_CHUNK, LANES), 0).astype(f32)


def _mix0_kernel(*refs, tc, whole_seq):
    if whole_seq:
        dec_ref, q_ref, k_ref, v_ref, g_ref, u_ref, vv_ref, ws_ref, bs_ref, o_ref, sff_ref, sbf_ref = refs
    else:
        (dec_ref, q_ref, k_ref, v_ref, g_ref, u_ref, vv_ref, s0f_ref, sbn_ref, ws_ref, bs_ref,
         o_ref, sf_scr) = refs

        @pl.when(pl.program_id(1) == 0)
        def _():
            sf_scr[...] = s0f_ref[...]

    row = _row_iota()
    col = lax.broadcasted_iota(jnp.int32, (RET_CHUNK, LANES), 1).astype(f32)
    c = float(RET_CHUNK)
    nt = (((1,), (1,)), ((), ()))
    tn = (((0,), (0,)), ((), ()))
    for hh in range(RET_HEADS):
        lg_f = jax.nn.log_sigmoid(dec_ref[0, hh])
        lg_b = jax.nn.log_sigmoid(dec_ref[1, hh])
        decay = jnp.where(row >= col,
                          jnp.exp(lg_f * jnp.maximum(row - col, 0.0)),
                          jnp.exp(lg_b * jnp.maximum(col - row - 1.0, 0.0)))
        qd_f = jnp.exp(lg_f * (row + 1.0))
        qd_b = jnp.exp(lg_b * (c - 1.0 - row))
        cs = slice(hh * RET_DK, (hh + 1) * RET_DK)
        kd_f = jnp.exp(lg_f * (c - 1.0 - row))
        kd_b = jnp.exp(lg_b * row)
        gc_f = jnp.exp(lg_f * c)
        gc_b = jnp.exp(lg_b * c)
        s_prev, s_next = [None] * tc, [None] * tc
        sf = None if whole_seq else sf_scr[hh]
        sb = None
        for cc in range(tc):
            rs = slice(cc * RET_CHUNK, (cc + 1) * RET_CHUNK)
            kf = k_ref[rs, cs].astype(f32)
            kv = lax.dot_general((kf * kd_f).astype(bf16), v_ref[rs, cs], tn, preferred_element_type=f32)
            s_prev[cc] = sf
            sf = kv if sf is None else gc_f * sf + kv
        if whole_seq:
            for cc in reversed(range(tc)):
                rs = slice(cc * RET_CHUNK, (cc + 1) * RET_CHUNK)
                kf = k_ref[rs, cs].astype(f32)
                kv = lax.dot_general((kf * kd_b).astype(bf16), v_ref[rs, cs], tn, preferred_element_type=f32)
                s_next[cc] = sb
                sb = kv if sb is None else gc_b * sb + kv
            sff_ref[hh] = sf
            sbf_ref[hh] = sb
        else:
            sf_scr[hh] = sf
        for cc in range(tc):
            rs = slice(cc * RET_CHUNK, (cc + 1) * RET_CHUNK)
            q = q_ref[rs, cs]
            qf = q.astype(f32)
            s = lax.dot_general(q, k_ref[rs, cs], nt, preferred_element_type=f32)
            o = jnp.dot((s * decay).astype(bf16), v_ref[rs, cs], preferred_element_type=f32)
            sp_c = None if s_prev[cc] is None else s_prev[cc].astype(bf16)
            if whole_seq:
                sn_c = None if s_next[cc] is None else s_next[cc].astype(bf16)
            else:
                sn_c = sbn_ref[cc, hh]
            if sp_c is not None:
                o = o + jnp.dot((qf * qd_f).astype(bf16), sp_c, preferred_element_type=f32)
            if sn_c is not None:
                o = o + jnp.dot((qf * qd_b).astype(bf16), sn_c, preferred_element_type=f32)
            mu = jnp.mean(o, axis=-1, keepdims=True)
            var = jnp.mean(jnp.square(o - mu), axis=-1, keepdims=True)
            on = (o - mu) * lax.rsqrt(var + EPS)
            o_ref[rs, cs] = (g_ref[rs, cs].astype(f32) * on).astype(bf16)
    off = RET_HEADS * RET_DV
    for gg in range(GMLP_GROUPS):
        cs = slice(gg * GMLP_CH, (gg + 1) * GMLP_CH)
        w = ws_ref[gg]
        bias = bs_ref[gg]
        for cc in range(tc):
            rs = slice(cc * RET_CHUNK, (cc + 1) * RET_CHUNK)
            gu = u_ref[rs, cs].astype(f32)
            gv = vv_ref[rs, cs].astype(f32)
            gv = gv * lax.rsqrt(jnp.mean(gv * gv, axis=-1, keepdims=True) + EPS)
            sp = jnp.dot(w, gv.astype(bf16), preferred_element_type=f32) + bias
            o_ref[rs, off + gg * GMLP_CH: off + (gg + 1) * GMLP_CH] = (gu * sp).astype(bf16)


def _mix0_call(qkv, dec, s0f, sbn, ws, bs, tc):
    bsz, ls, _ = qkv.shape
    t = tc * RET_CHUNK
    hw = RET_HEADS * RET_DK
    st = (RET_HEADS, RET_DK, RET_DV)
    col_spec = lambda j: pl.BlockSpec((None, t, hw), lambda b, i, j=j: (b, i, j))
    whole_seq = s0f is None
    in_specs = [
        pl.BlockSpec((2, RET_HEADS, 1, LANES), lambda b, i: (0, 0, 0, 0)),
        col_spec(0), col_spec(1), col_spec(2), col_spec(3), col_spec(4), col_spec(5),
    ]
    args = [dec, qkv, qkv, qkv, qkv, qkv, qkv]
    out_specs = [pl.BlockSpec((None, t, AB_OUT), lambda b, i: (b, i, 0))]
    out_shape = [jax.ShapeDtypeStruct((bsz, ls, AB_OUT), bf16)]
    if whole_seq:
        assert ls == t
        out_specs += [pl.BlockSpec((None,) + st, lambda b, i: (b, 0, 0, 0))] * 2
        out_shape += [jax.ShapeDtypeStruct((bsz,) + st, f32)] * 2
    else:
        in_specs += [pl.BlockSpec((None,) + st, lambda b, i: (b, 0, 0, 0)),
                     pl.BlockSpec((None, tc) + st, lambda b, i: (b, i, 0, 0, 0))]
        args += [s0f, sbn]
    in_specs += [
        pl.BlockSpec((GMLP_GROUPS, RET_CHUNK, RET_CHUNK), lambda b, i: (0, 0, 0)),
        pl.BlockSpec((GMLP_GROUPS, RET_CHUNK, 1), lambda b, i: (0, 0, 0)),
    ]
    args += [ws, bs]
    return pl.pallas_call(
        functools.partial(_mix0_kernel, tc=tc, whole_seq=whole_seq),
        grid=(bsz, ls // t),
        in_specs=in_specs,
        out_specs=out_specs,
        out_shape=out_shape,
        scratch_shapes=[] if whole_seq else [pltpu.VMEM(st, f32)],
        compiler_params=_cparams(("parallel", "parallel" if whole_seq else "arbitrary"), 32),
        name="mix0_seq" if whole_seq else "mix0",
    )(*args)


def _loop(n, body, carry):
    for i in range(n):
        carry = body(i, carry)
    return carry


def _lru_body(gate_ref, xin_ref, cw_ref, cb_ref, wg_ref, lam_ref, h0f_ref, h0b_ref,
              o_ref, hfl_ref, hbf_ref, xs, xt, af, bfs, ab, bbs, *, nseg, chained):
    seg, pitch = LRU_SEG, LRU_PITCH
    blk = seg

    def stage(s, carry):
        xs[pl.ds(s * pitch, seg), :] = xin_ref[pl.ds(s * seg, seg), :].astype(f32)
        return carry

    _loop(nseg, stage, 0)

    def tile(j):
        return pl.ds(j * nseg, nseg)

    def gather(j, carry):
        xt[tile(j + 2), :] = xs[pl.ds(j, nseg, stride=pitch), :]
        return carry

    _loop(seg, gather, 0)
    if chained:
        srow = lax.broadcasted_iota(jnp.int32, (nseg, LANES), 0)
        xt[tile(0), :] = jnp.where(srow == 0, 0.0, pltpu.roll(xt[tile(seg), :], 1, axis=0))
        xt[tile(1), :] = jnp.where(srow == 0, 0.0, pltpu.roll(xt[tile(seg + 1), :], 1, axis=0))
        xt[tile(seg + 2), :] = jnp.where(srow == nseg - 1, 0.0, pltpu.roll(xt[tile(2), :], nseg - 1, axis=0))
    else:
        zt = jnp.zeros((nseg, LANES), f32)
        xt[tile(0), :] = zt
        xt[tile(1), :] = zt
        xt[tile(seg + 2), :] = zt

    cw = cw_ref[...]
    cb = cb_ref[...]
    sp4 = (0.5 * LRU_C) * jax.nn.softplus(-lam_ref[...])
    sp4e = sp4 * (-LOG2_E)
    ones = (lax.broadcasted_iota(jnp.int32, (blk, LANES), 1) < N_BIAS_TERMS).astype(bf16)

    def gates(jb, carry):
        p0 = jb * blk
        xm2 = xt[pl.ds(p0, blk), :]
        xm1 = xt[pl.ds(p0 + nseg, blk), :]
        x0 = xt[pl.ds(p0 + 2 * nseg, blk), :]
        xp1 = xt[pl.ds(p0 + 3 * nseg, blk), :]
        xch = cb + xm2 * cw[0:1] + xm1 * cw[1:2] + x0 * cw[2:3] + xp1 * cw[3:4]
        lhs = jnp.concatenate([xch.astype(bf16), ones], axis=1)
        gt = jnp.dot(lhs, wg_ref[...], preferred_element_type=f32)
        for d, (a_s, b_s) in enumerate(((af, bfs), (ab, bbs))):
            tr = jnp.tanh(gt[:, (2 * d) * LANES:(2 * d + 1) * LANES])
            ti = jnp.tanh(gt[:, (2 * d + 1) * LANES:(2 * d + 2) * LANES])
            r2 = tr + 1.0
            nla = r2 * sp4[d:d + 1]
            a = jnp.exp2(r2 * sp4e[d:d + 1])
            m2 = jnp.tanh(nla) * (1.0 + a * a)
            mult = m2 * lax.rsqrt(jnp.maximum(m2, 1e-30))
            a_s[pl.ds(p0, blk), :] = a
            b_s[pl.ds(p0, blk), :] = mult * ((ti + 1.0) * xch)
        return carry

    _loop(nseg, gates, 0)

    if chained:
        def ends(j, carry):
            hf, pf, hb, pb = carry
            a = af[tile(j), :]
            hf = a * hf + bfs[tile(j), :]
            pf = a * pf
            a2 = ab[tile(seg - 1 - j), :]
            hb = a2 * hb + bbs[tile(seg - 1 - j), :]
            pb = a2 * pb
            return hf, pf, hb, pb

        zero = jnp.zeros((nseg, LANES), f32)
        one = jnp.ones((nseg, LANES), f32)
        hf_e, pf_e, hb_e, pb_e = _loop(seg, ends, (zero, one, zero, one))
        cf = [h0f_ref[0:1, :]]
        for s in range(1, nseg):
            cf.append(hf_e[s - 1:s] + pf_e[s - 1:s] * cf[-1])
        cbk = [h0b_ref[0:1, :]]
        for s in range(nseg - 2, -1, -1):
            cbk.append(hb_e[s + 1:s + 2] + pb_e[s + 1:s + 2] * cbk[-1])
        hf0 = jnp.concatenate(cf, axis=0)
        hb0 = jnp.concatenate(cbk[::-1], axis=0)
    else:
        hf0 = h0f_ref[...]
        hb0 = h0b_ref[...]

    def scan_b(k, hb):
        j = seg - 1 - k
        hb = ab[tile(j), :] * hb + bbs[tile(j), :]
        xt[tile(j), :] = hb
        return hb

    hb_l = _loop(seg, scan_b, hb0)

    def scan_f(j, hf):
        hf = af[tile(j), :] * hf + bfs[tile(j), :]
        xs[pl.ds(j, nseg, stride=pitch), :] = hf + xt[tile(j), :]
        return hf

    hf_l = _loop(seg, scan_f, hf0)
    if chained:
        hfl_ref[...] = hf_l[nseg - 1:nseg]
        hbf_ref[...] = hb_l[0:1]
    else:
        hfl_ref[...] = hf_l
        hbf_ref[...] = hb_l

    def combine(s, carry):
        g = gate_ref[pl.ds(s * seg, seg), :].astype(f32)
        o_ref[pl.ds(s * seg, seg), :] = (g * xs[pl.ds(s * pitch, seg), :]).astype(bf16)
        return carry

    _loop(nseg, combine, 0)


def _lru_specs(ntok, nseg, gc, pair):
    rows = LRU_SEG * nseg
    nt = ntok // rows
    nb = LRU_BLOCKS
    in_specs = [
        pl.BlockSpec((rows, LANES), lambda *g: pair(*g)),
        pl.BlockSpec((rows, LANES), lambda *g: (pair(*g)[0], nb + pair(*g)[1])),
        pl.BlockSpec((4, LANES), lambda *g: (0, pair(*g)[1])),
        pl.BlockSpec((1, LANES), lambda *g: (0, pair(*g)[1])),
        pl.BlockSpec((None, 2 * LANES, 4 * LANES), lambda *g: (pair(*g)[1], 0, 0)),
        pl.BlockSpec((2, LANES), lambda *g: (0, pair(*g)[1])),
        pl.BlockSpec((None, gc, LANES), lambda *g: (pair(*g)[0], 0, pair(*g)[1])),
        pl.BlockSpec((None, gc, LANES), lambda *g: (pair(*g)[0], 0, pair(*g)[1])),
    ]
    out_specs = [
        pl.BlockSpec((rows, LANES), lambda *g: pair(*g)),
        pl.BlockSpec((None, gc, LANES), lambda *g: (pair(*g)[0], 0, pair(*g)[1])),
        pl.BlockSpec((None, gc, LANES), lambda *g: (pair(*g)[0], 0, pair(*g)[1])),
    ]
    out_shape = [
        jax.ShapeDtypeStruct((ntok, D_RNN), bf16),
        jax.ShapeDtypeStruct((nt, gc, D_RNN), f32),
        jax.ShapeDtypeStruct((nt, gc, D_RNN), f32),
    ]
    scr = pltpu.VMEM((rows, LANES), f32)
    scratch = [pltpu.VMEM((nseg * LRU_PITCH, LANES), f32), pltpu.VMEM((rows + 3 * nseg, LANES), f32),
               scr, scr, scr, scr]
    return in_specs, out_specs, out_shape, scratch


def _outmlp_body(x_ref, mix_ref, g1_ref, sh2_ref, sc2_ref, g2_ref, gain_ref, fgain_ref,
                 wo_ref, w1_ref, w2_ref, *, final):
    mix = jnp.dot(mix_ref[...], wo_ref[...], preferred_element_type=f32)
    x1 = x_ref[...] + g1_ref[...] * mix
    h = (_rms(x1, gain_ref[...]) * (1.0 + sc2_ref[...]) + sh2_ref[...]).astype(bf16)
    acc = None
    for j in range(0, D_FF, FF_CHUNK):
        hid = jnp.dot(h, w1_ref[:, j:j + FF_CHUNK], preferred_element_type=f32)
        hid = jnp.square(jnp.maximum(hid, 0.0)).astype(bf16)
        part = jnp.dot(hid, w2_ref[j:j + FF_CHUNK, :], preferred_element_type=f32)
        acc = part if acc is None else acc + part
    x2 = x1 + g2_ref[...] * acc
    if final:
        x2 = _rms(x2, fgain_ref[...])
    return x2


N_MLP_IN = 11
N_LRU_IN = 8


def _outmlp_kernel(*refs, final, lru):
    n_in = N_MLP_IN + (N_LRU_IN if lru else 0)
    if lru is not None:
        _lru_body(*refs[N_MLP_IN:n_in], *refs[n_in + 1:], nseg=lru[0], chained=lru[1])
    refs[n_in][...] = _outmlp_body(*refs[:N_MLP_IN], final=final)


def _outmlp_call(x3, mix3, modl, gain, fgain, wo, w1, w2, *, layer, final, t, lru_args=None):
    bg, lg, d = x3.shape
    steps = lg // t
    dm = mix3.shape[-1]
    mod_spec = lambda k: pl.BlockSpec((None, 1, d), lambda b, i, k=k: (b, 0, k))

    def resident(shape, lead=None):
        if lead is None:
            return pl.BlockSpec(shape, lambda b, i: (0, 0), pipeline_mode=pl.Buffered(1))
        return pl.BlockSpec((None,) + shape, lambda b, i: (lead, 0, 0), pipeline_mode=pl.Buffered(1))

    in_specs = [
        pl.BlockSpec((None, t, d), lambda b, i: (b, i, 0)),
        pl.BlockSpec((None, t, dm), lambda b, i: (b, i, 0)),
        mod_spec(2), mod_spec(3), mod_spec(4), mod_spec(5),
        pl.BlockSpec((1, d), lambda b, i: (0, 0)),
        pl.BlockSpec((1, d), lambda b, i: (0, 0)),
        resident((dm, d)), resident((d, D_FF), layer), resident((D_FF, d), layer),
    ]
    args = [x3, mix3, modl, modl, modl, modl, gain.reshape(1, d), fgain.reshape(1, d), wo, w1, w2]
    out_specs = [pl.BlockSpec((None, t, d), lambda b, i: (b, i, 0))]
    out_shape = [jax.ShapeDtypeStruct((bg, lg, d), f32)]
    scratch = []
    lru = None
    if lru_args is not None:
        gx, cw, cb, wg, lam, h0f, h0b, nseg, chained = lru_args
        ntok = gx.shape[0]
        assert (ntok // (LRU_SEG * nseg)) * LRU_BLOCKS == bg * steps, "one RG-LRU (tile, block) pair per MLP step"

        def pair(b, i):
            p = b * steps + i
            return p // LRU_BLOCKS, p % LRU_BLOCKS

        li, lo, ls, scratch = _lru_specs(ntok, nseg, h0f.shape[1], pair)
        in_specs += li
        out_specs += lo
        out_shape += ls
        args += [gx, gx, cw, cb, wg, lam, h0f, h0b]
        lru = (nseg, chained)
    res = pl.pallas_call(
        functools.partial(_outmlp_kernel, final=final, lru=lru),
        grid=(bg, steps),
        in_specs=in_specs,
        out_specs=out_specs,
        out_shape=out_shape,
        scratch_shapes=scratch,
        compiler_params=_cparams(("parallel", "parallel"), 58),
        name=("outmlp_final" if final else "outmlp") + ("_rglru" if lru else ""),
    )(*args)
    return res if lru else res[0]


@functools.lru_cache(maxsize=None)
def _rope_tables(length):
    pos = np.arange(length)
    rowp = (pos // GRID_W).astype(np.float64)
    colp = (pos % GRID_W).astype(np.float64)
    nf = RET_DK // 4
    inv_freq = ROPE_BASE ** (-np.arange(nf, dtype=np.float64) / nf)
    ar = rowp[:, None] * inv_freq[None, :]
    ac = colp[:, None] * inv_freq[None, :]
    cos = np.concatenate([np.cos(ar), np.cos(ar), np.cos(ac), np.cos(ac)], axis=-1).astype(np.float32)
    sin = np.concatenate([-np.sin(ar), np.sin(ar), -np.sin(ac), np.sin(ac)], axis=-1).astype(np.float32)
    return cos, sin


def kernel(x_prompt, x_sample, c, c_ctx, state_ret_fwd, state_ret_bwd, state_lru_fwd, state_lru_bwd, mod_w, mod_b, norm_mix, norm_mlp, mlp_w1, mlp_w2, ab_w_in, ab_w_out, ret_decay_fwd, ret_decay_bwd, gmlp_ws, gmlp_bs, lru_w_in, lru_conv_w, lru_conv_b, lru_gate_a_w, lru_gate_a_b, lru_gate_x_w, lru_gate_x_b, lru_lambda, lru_w_out, final_norm):
    d = D_MODEL
    n_lat = c.shape[0]
    conds = jnp.concatenate([c_ctx[None, :], c, jnp.zeros((COND_ROWS - 1 - n_lat, d), f32)], axis=0)
    mod = _mod_call(conds, mod_w, mod_b)

    w_in0 = ab_w_in[0].astype(bf16)
    w_out0 = ab_w_out[0].astype(bf16)
    w1 = mlp_w1.astype(bf16)
    w2 = mlp_w2.astype(bf16)
    w_in1 = lru_w_in[0].astype(bf16)
    w_out1 = lru_w_out[0].astype(bf16)
    ws = gmlp_ws[0].astype(bf16)
    bs = gmlp_bs[0][:, :, None]
    dec = jnp.broadcast_to(jnp.stack([ret_decay_fwd[0], ret_decay_bwd[0]])[:, :, None, None],
                           (2, RET_HEADS, 1, LANES))
    wg = jnp.concatenate([lru_gate_a_w[0, 0], lru_gate_x_w[0, 0], lru_gate_a_w[0, 1], lru_gate_x_w[0, 1]],
                         axis=-1).astype(bf16)
    bgate = 0.5 * jnp.stack([lru_gate_a_b[0, 0], lru_gate_x_b[0, 0], lru_gate_a_b[0, 1], lru_gate_x_b[0, 1]], axis=0)
    bgate = bgate.reshape(4, LRU_BLOCKS, LRU_BS).transpose(1, 0, 2).reshape(LRU_BLOCKS, 1, 4 * LRU_BS)
    terms, rem = [], bgate
    for _ in range(N_BIAS_TERMS):
        terms.append(rem.astype(bf16))
        rem = rem - terms[-1].astype(f32)
    wg = jnp.concatenate([wg] + terms + [jnp.zeros((LRU_BLOCKS, LANES - N_BIAS_TERMS, 4 * LRU_BS), bf16)], axis=1)
    lru_w = (0.5 * lru_conv_w[0], 0.5 * lru_conv_b[0].reshape(1, D_RNN), wg, lru_lambda[0])

    def mixer0(x3, modl, bsz, ls, ret_f0, ret_b0, latent):
        tc = min(4, ls // RET_CHUNK)
        if latent:
            qkv, sbn = _inproj_call(x3, modl, norm_mix[0], w_in0, _rope_tables(ls), retention=True,
                                    sh_col=0, sc_col=1, bwd=(dec, ret_b0))
            mix = _mix0_call(qkv, dec, ret_f0, sbn, ws, bs, tc)[0]
            sff = sbf = None
        else:
            qkv = _inproj_call(x3, modl, norm_mix[0], w_in0, None, retention=True, sh_col=0, sc_col=1)
            mix, sff, sbf = _mix0_call(qkv.reshape(bsz, ls, AB_IN), dec, None, None, ws, bs, tc)
        return mix.reshape(x3.shape[0], x3.shape[1], AB_OUT), sff, sbf

    def lru_inproj(x3, modl):
        gx = _inproj_call(x3, modl, norm_mix[1], w_in1, None, retention=False, sh_col=0, sc_col=1)
        return gx.reshape(-1, 2 * D_RNN)

    bp, lp, _ = x_prompt.shape
    bl, ll, _ = x_sample.shape
    assert lp == LRU_SEG and ll % LRU_SEG == 0
    xc = x_prompt.reshape(1, bp * lp, d)
    xl = x_sample
    mod_c = [mod[l, 0:1][:, None, :] for l in range(DEPTH)]
    mod_l = [mod[l, 1:1 + n_lat][:, None, :] for l in range(DEPTH)]

    mix_c, sff, sbf = mixer0(xc, mod_c[0], bp, lp, None, None, False)
    xc = _outmlp_call(xc, mix_c, mod_c[0], norm_mlp[0], final_norm, w_out0, w1, w2, layer=0, final=False,
                      t=PLAIN_MLP_TILE)
    gx_c = lru_inproj(xc, mod_c[1])
    mix_l, _, _ = mixer0(xl, mod_l[0], bl, ll, state_ret_fwd[:, 0], state_ret_bwd[:, 0], True)

    steps_l = bl * (ll // TOK_TILE)
    nseg_c = (bp * lp // LRU_SEG) * LRU_BLOCKS // steps_l
    zero_lru = jnp.zeros((bp // nseg_c, nseg_c, D_RNN), f32)
    xl, pre_c, hfl, hbf = _outmlp_call(
        xl, mix_l, mod_l[0], norm_mlp[0], final_norm, w_out0, w1, w2, layer=0, final=False, t=TOK_TILE,
        lru_args=(gx_c,) + lru_w + (zero_lru, zero_lru, nseg_c, False))
    gx_l = lru_inproj(xl, mod_l[1])

    nseg_l = ll // LRU_SEG
    t_c = bp * lp // (bl * LRU_BLOCKS)
    y_c, pre_l, _, _ = _outmlp_call(
        xc, pre_c.reshape(1, bp * lp, D_RNN), mod_c[1], norm_mlp[1], final_norm, w_out1, w1, w2,
        layer=1, final=True, t=t_c,
        lru_args=(gx_l,) + lru_w + (state_lru_fwd[:, 0].reshape(bl, 1, D_RNN),
                                    state_lru_bwd[:, 0].reshape(bl, 1, D_RNN), nseg_l, True))
    y_l = _outmlp_call(xl, pre_l.reshape(bl, ll, D_RNN), mod_l[1], norm_mlp[1], final_norm, w_out1, w1, w2,
                       layer=1, final=True, t=PLAIN_MLP_TILE)
    return (y_c.reshape(bp, lp, d), y_l, sff[:, None], sbf[:, None],
            hfl.reshape(bp, 1, D_RNN), hbf.reshape(bp, 1, D_RNN))
```

```python
import functools

import jax
import jax.numpy as jnp
import numpy as np
from jax import lax
from jax.experimental import pallas as pl
from jax.experimental.pallas import tpu as pltpu

f32 = jnp.float32
bf16 = jnp.bfloat16

D_MODEL = 1024
DEPTH = 2
GRID_W = 64
EPS = 1e-6
RET_HEADS = 4
RET_DK = 128
RET_DV = 128
RET_CHUNK = 128
ROPE_BASE = 10000.0
GMLP_GROUPS = 4
GMLP_CH = 128
AB_IN = 3072
AB_OUT = 1024
D_RNN = D_MODEL
LRU_BLOCKS = 8
LRU_BS = 128
LRU_C = 8.0
D_FF = 4 * D_MODEL

LANES = 128
SUBLANES = 8
MIB = 1024 * 1024

TOK_TILE = 512
PLAIN_MLP_TILE = 1024
INPROJ_TILE = 1024
COND_ROWS = 8
MOD_TN = 1536
LRU_SEG = 256
LRU_PITCH = LRU_SEG + SUBLANES
FF_CHUNK = 1024
N_BIAS_TERMS = 3
LOG2_E = 1.4426950408889634
GELU_C1 = 0.7978845608028654
GELU_C3 = 0.044715 * GELU_C1


def _cparams(sem, vmem_mib):
    return pltpu.CompilerParams(dimension_semantics=sem, vmem_limit_bytes=vmem_mib * MIB)


def _gelu(x):
    hx = 0.5 * x
    return hx + hx * jnp.tanh(x * (GELU_C1 + GELU_C3 * (x * x)))


def _silu(x):
    hx = 0.5 * x
    return hx + hx * jnp.tanh(hx)


def _rms(x, gain):
    ms = jnp.mean(x * x, axis=-1, keepdims=True)
    return x * lax.rsqrt(ms + EPS) * gain


def _mod_kernel(cond_ref, w_ref, b_ref, o_ref):
    c = cond_ref[...]
    s = c * jax.nn.sigmoid(c)
    o_ref[...] = jnp.dot(s.astype(bf16), w_ref[...].astype(bf16), preferred_element_type=f32) + b_ref[...]


def _mod_call(conds, mod_w, mod_b):
    depth, d, n = mod_w.shape
    return pl.pallas_call(
        _mod_kernel,
        grid=(depth, n // MOD_TN),
        in_specs=[
            pl.BlockSpec((COND_ROWS, d), lambda l, j: (0, 0)),
            pl.BlockSpec((None, d, MOD_TN), lambda l, j: (l, 0, j)),
            pl.BlockSpec((None, 1, MOD_TN), lambda l, j: (l, 0, j)),
        ],
        out_specs=pl.BlockSpec((None, COND_ROWS, MOD_TN), lambda l, j: (l, 0, j)),
        out_shape=jax.ShapeDtypeStruct((depth, COND_ROWS, n), f32),
        compiler_params=_cparams(("parallel", "parallel"), 32),
        name="mod",
    )(conds, mod_w, mod_b.reshape(depth, 1, n))


def _rope(t, cos, sin_signed, first_half):
    up = pltpu.roll(t, LANES - 32, axis=1)
    dn = pltpu.roll(t, 32, axis=1)
    return t * cos + jnp.where(first_half, up, dn) * sin_signed


def _inproj_kernel(*refs, retention, rope, bwd_states):
    if bwd_states:
        (x_ref, sh_ref, sc_ref, gain_ref, w_ref, cos_ref, sin_ref, dec_ref, s0b_ref,
         o_ref, sbn_ref, sb_scr) = refs

        @pl.when(pl.program_id(1) == 0)
        def _():
            sb_scr[...] = s0b_ref[...]
    elif rope:
        x_ref, sh_ref, sc_ref, gain_ref, w_ref, cos_ref, sin_ref, o_ref = refs
    else:
        x_ref, sh_ref, sc_ref, gain_ref, w_ref, o_ref = refs
    h = _rms(x_ref[...], gain_ref[...]) * (1.0 + sc_ref[...]) + sh_ref[...]
    hb = h.astype(bf16)
    n = w_ref.shape[1]
    if not retention:
        for j in range(0, n, 512):
            y = jnp.dot(hb, w_ref[:, j:j + 512], preferred_element_type=f32)
            o_ref[:, j:j + 512] = (_gelu(y) if j < n // 2 else y).astype(bf16)
        return
    qk = RET_HEADS * RET_DK
    if rope:
        cos = cos_ref[...]
        sin = sin_ref[...]
        lane = lax.broadcasted_iota(jnp.int32, cos.shape, 1)
        first_half = (lane % 64) < 32
    k_heads = []
    for part in range(2):
        y = jnp.dot(hb, w_ref[:, part * qk:(part + 1) * qk], preferred_element_type=f32)
        for hh in range(RET_HEADS):
            t = y[:, hh * RET_DK:(hh + 1) * RET_DK]
            if part == 1:
                t = t * (RET_DK ** -0.5)
            if rope:
                t = _rope(t, cos, sin, first_half)
            c0 = part * qk + hh * RET_DK
            tb = t.astype(bf16)
            o_ref[:, c0:c0 + RET_DK] = tb
            if part == 1:
                k_heads.append(tb)
    acts = (None, _silu, _gelu, _gelu)
    for j, act in zip(range(2 * qk, n, 512), acts):
        y = jnp.dot(hb, w_ref[:, j:j + 512], preferred_element_type=f32)
        yb = (y if act is None else act(y)).astype(bf16)
        o_ref[:, j:j + 512] = yb
        if bwd_states and j == 2 * qk:
            vb = yb
    if bwd_states:
        row = lax.broadcasted_iota(jnp.int32, (RET_CHUNK, LANES), 0).astype(f32)
        nchunk = x_ref.shape[0] // RET_CHUNK
        for hh in range(RET_HEADS):
            lg_b = jax.nn.log_sigmoid(dec_ref[1, hh])
            kd_b = jnp.exp(lg_b * row)
            gc_b = jnp.exp(lg_b * float(RET_CHUNK))
            cs = slice(hh * RET_DK, (hh + 1) * RET_DK)
            for cc in reversed(range(nchunk)):
                rs = slice(cc * RET_CHUNK, (cc + 1) * RET_CHUNK)
                s_next = sb_scr[hh]
                sbn_ref[cc, hh] = s_next.astype(bf16)
                kd = (k_heads[hh][rs].astype(f32) * kd_b).astype(bf16)
                kv = lax.dot_general(kd, vb[rs, cs], (((0,), (0,)), ((), ())), preferred_element_type=f32)
                sb_scr[hh] = gc_b * s_next + kv


def _inproj_call(x3, modl, gain, w, rope_tabs, *, retention, sh_col, sc_col, bwd=None):
    bg, lg, d = x3.shape
    n = w.shape[1]
    t = INPROJ_TILE
    ns = lg // t
    rope = rope_tabs is not None
    pos = (lambda i: ns - 1 - i) if bwd is not None else (lambda i: i)
    in_specs = [
        pl.BlockSpec((None, t, d), lambda b, i: (b, pos(i), 0)),
        pl.BlockSpec((None, 1, d), lambda b, i: (b, 0, sh_col)),
        pl.BlockSpec((None, 1, d), lambda b, i: (b, 0, sc_col)),
        pl.BlockSpec((1, d), lambda b, i: (0, 0)),
        pl.BlockSpec((d, n), lambda b, i: (0, 0), pipeline_mode=pl.Buffered(1)),
    ]
    args = [x3, modl, modl, gain.reshape(1, d), w]
    if rope:
        in_specs += [pl.BlockSpec((t, LANES), lambda b, i: (pos(i), 0))] * 2
        args += list(rope_tabs)
    out_specs = [pl.BlockSpec((None, t, n), lambda b, i: (b, pos(i), 0))]
    out_shape = [jax.ShapeDtypeStruct((bg, lg, n), bf16)]
    scratch = []
    if bwd is not None:
        st = (RET_HEADS, RET_DK, RET_DV)
        tc = t // RET_CHUNK
        in_specs += [pl.BlockSpec((2, RET_HEADS, 1, LANES), lambda b, i: (0, 0, 0, 0)),
                     pl.BlockSpec((None,) + st, lambda b, i: (b, 0, 0, 0))]
        args += list(bwd)
        out_specs.append(pl.BlockSpec((None, tc) + st, lambda b, i: (b, pos(i), 0, 0, 0)))
        out_shape.append(jax.ShapeDtypeStruct((bg, lg // RET_CHUNK) + st, bf16))
        scratch.append(pltpu.VMEM(st, f32))
    res = pl.pallas_call(
        functools.partial(_inproj_kernel, retention=retention, rope=rope, bwd_states=bwd is not None),
        grid=(bg, ns),
        in_specs=in_specs,
        out_specs=out_specs,
        out_shape=out_shape,
        scratch_shapes=scratch,
        compiler_params=_cparams(("parallel", "arbitrary" if bwd is not None else "parallel"), 48),
        name="inproj_ret" if retention else "inproj_lru",
    )(*args)
    return res if bwd is not None else res[0]


def _row_iota():
    return lax.broadcasted_iota(jnp.int32, (RET_CHUNK, LANES), 0).astype(f32)


def _mix0_kernel(*refs, tc, whole_seq):
    if whole_seq:
        dec_ref, q_ref, k_ref, v_ref, g_ref, u_ref, vv_ref, ws_ref, bs_ref, o_ref, sff_ref, sbf_ref = refs
    else:
        (dec_ref, q_ref, k_ref, v_ref, g_ref, u_ref, vv_ref, s0f_ref, sbn_ref, ws_ref, bs_ref,
         o_ref, sf_scr) = refs

        @pl.when(pl.program_id(1) == 0)
        def _():
            sf_scr[...] = s0f_ref[...]

    row = _row_iota()
    col = lax.broadcasted_iota(jnp.int32, (RET_CHUNK, LANES), 1).astype(f32)
    c = float(RET_CHUNK)
    nt = (((1,), (1,)), ((), ()))
    tn = (((0,), (0,)), ((), ()))
    for hh in range(RET_HEADS):
        lg_f = jax.nn.log_sigmoid(dec_ref[0, hh])
        lg_b = jax.nn.log_sigmoid(dec_ref[1, hh])
        decay = jnp.where(row >= col,
                          jnp.exp(lg_f * jnp.maximum(row - col, 0.0)),
                          jnp.exp(lg_b * jnp.maximum(col - row - 1.0, 0.0)))
        qd_f = jnp.exp(lg_f * (row + 1.0))
        qd_b = jnp.exp(lg_b * (c - 1.0 - row))
        cs = slice(hh * RET_DK, (hh + 1) * RET_DK)
        kd_f = jnp.exp(lg_f * (c - 1.0 - row))
        kd_b = jnp.exp(lg_b * row)
        gc_f = jnp.exp(lg_f * c)
        gc_b = jnp.exp(lg_b * c)
        s_prev, s_next = [None] * tc, [None] * tc
        sf = None if whole_seq else sf_scr[hh]
        sb = None
        for cc in range(tc):
            rs = slice(cc * RET_CHUNK, (cc + 1) * RET_CHUNK)
            kf = k_ref[rs, cs].astype(f32)
            kv = lax.dot_general((kf * kd_f).astype(bf16), v_ref[rs, cs], tn, preferred_element_type=f32)
            s_prev[cc] = sf
            sf = kv if sf is None else gc_f * sf + kv
        if whole_seq:
            for cc in reversed(range(tc)):
                rs = slice(cc * RET_CHUNK, (cc + 1) * RET_CHUNK)
                kf = k_ref[rs, cs].astype(f32)
                kv = lax.dot_general((kf * kd_b).astype(bf16), v_ref[rs, cs], tn, preferred_element_type=f32)
                s_next[cc] = sb
                sb = kv if sb is None else gc_b * sb + kv
            sff_ref[hh] = sf
            sbf_ref[hh] = sb
        else:
            sf_scr[hh] = sf
        for cc in range(tc):
            rs = slice(cc * RET_CHUNK, (cc + 1) * RET_CHUNK)
            q = q_ref[rs, cs]
            qf = q.astype(f32)
            s = lax.dot_general(q, k_ref[rs, cs], nt, preferred_element_type=f32)
            o = jnp.dot((s * decay).astype(bf16), v_ref[rs, cs], preferred_element_type=f32)
            sp_c = None if s_prev[cc] is None else s_prev[cc].astype(bf16)
            if whole_seq:
                sn_c = None if s_next[cc] is None else s_next[cc].astype(bf16)
            else:
                sn_c = sbn_ref[cc, hh]
            if sp_c is not None:
                o = o + jnp.dot((qf * qd_f).astype(bf16), sp_c, preferred_element_type=f32)
            if sn_c is not None:
                o = o + jnp.dot((qf * qd_b).astype(bf16), sn_c, preferred_element_type=f32)
            mu = jnp.mean(o, axis=-1, keepdims=True)
            var = jnp.mean(jnp.square(o - mu), axis=-1, keepdims=True)
            on = (o - mu) * lax.rsqrt(var + EPS)
            o_ref[rs, cs] = (g_ref[rs, cs].astype(f32) * on).astype(bf16)
    off = RET_HEADS * RET_DV
    for gg in range(GMLP_GROUPS):
        cs = slice(gg * GMLP_CH, (gg + 1) * GMLP_CH)
        w = ws_ref[gg]
        bias = bs_ref[gg]
        for cc in range(tc):
            rs = slice(cc * RET_CHUNK, (cc + 1) * RET_CHUNK)
            gu = u_ref[rs, cs].astype(f32)
            gv = vv_ref[rs, cs].astype(f32)
            gv = gv * lax.rsqrt(jnp.mean(gv * gv, axis=-1, keepdims=True) + EPS)
            sp = jnp.dot(w, gv.astype(bf16), preferred_element_type=f32) + bias
            o_ref[rs, off + gg * GMLP_CH: off + (gg + 1) * GMLP_CH] = (gu * sp).astype(bf16)


def _mix0_call(qkv, dec, s0f, sbn, ws, bs, tc):
    bsz, ls, _ = qkv.shape
    t = tc * RET_CHUNK
    hw = RET_HEADS * RET_DK
    st = (RET_HEADS, RET_DK, RET_DV)
    col_spec = lambda j: pl.BlockSpec((None, t, hw), lambda b, i, j=j: (b, i, j))
    whole_seq = s0f is None
    in_specs = [
        pl.BlockSpec((2, RET_HEADS, 1, LANES), lambda b, i: (0, 0, 0, 0)),
        col_spec(0), col_spec(1), col_spec(2), col_spec(3), col_spec(4), col_spec(5),
    ]
    args = [dec, qkv, qkv, qkv, qkv, qkv, qkv]
    out_specs = [pl.BlockSpec((None, t, AB_OUT), lambda b, i: (b, i, 0))]
    out_shape = [jax.ShapeDtypeStruct((bsz, ls, AB_OUT), bf16)]
    if whole_seq:
        assert ls == t
        out_specs += [pl.BlockSpec((None,) + st, lambda b, i: (b, 0, 0, 0))] * 2
        out_shape += [jax.ShapeDtypeStruct((bsz,) + st, f32)] * 2
    else:
        in_specs += [pl.BlockSpec((None,) + st, lambda b, i: (b, 0, 0, 0)),
                     pl.BlockSpec((None, tc) + st, lambda b, i: (b, i, 0, 0, 0))]
        args += [s0f, sbn]
    in_specs += [
        pl.BlockSpec((GMLP_GROUPS, RET_CHUNK, RET_CHUNK), lambda b, i: (0, 0, 0)),
        pl.BlockSpec((GMLP_GROUPS, RET_CHUNK, 1), lambda b, i: (0, 0, 0)),
    ]
    args += [ws, bs]
    return pl.pallas_call(
        functools.partial(_mix0_kernel, tc=tc, whole_seq=whole_seq),
        grid=(bsz, ls // t),
        in_specs=in_specs,
        out_specs=out_specs,
        out_shape=out_shape,
        scratch_shapes=[] if whole_seq else [pltpu.VMEM(st, f32)],
        compiler_params=_cparams(("parallel", "parallel" if whole_seq else "arbitrary"), 32),
        name="mix0_seq" if whole_seq else "mix0",
    )(*args)


def _loop(n, body, carry):
    for i in range(n):
        carry = body(i, carry)
    return carry


def _lru_body(gate_ref, xin_ref, cw_ref, cb_ref, wg_ref, lam_ref, h0f_ref, h0b_ref,
              o_ref, hfl_ref, hbf_ref, xs, xt, af, bfs, ab, bbs, *, nseg, chained):
    seg, pitch = LRU_SEG, LRU_PITCH
    blk = seg

    def stage(s, carry):
        xs[pl.ds(s * pitch, seg), :] = xin_ref[pl.ds(s * seg, seg), :].astype(f32)
        return carry

    _loop(nseg, stage, 0)

    def tile(j):
        return pl.ds(j * nseg, nseg)

    def gather(j, carry):
        xt[tile(j + 2), :] = xs[pl.ds(j, nseg, stride=pitch), :]
        return carry

    _loop(seg, gather, 0)
    if chained:
        srow = lax.broadcasted_iota(jnp.int32, (nseg, LANES), 0)
        xt[tile(0), :] = jnp.where(srow == 0, 0.0, pltpu.roll(xt[tile(seg), :], 1, axis=0))
        xt[tile(1), :] = jnp.where(srow == 0, 0.0, pltpu.roll(xt[tile(seg + 1), :], 1, axis=0))
        xt[tile(seg + 2), :] = jnp.where(srow == nseg - 1, 0.0, pltpu.roll(xt[tile(2), :], nseg - 1, axis=0))
    else:
        zt = jnp.zeros((nseg, LANES), f32)
        xt[tile(0), :] = zt
        xt[tile(1), :] = zt
        xt[tile(seg + 2), :] = zt

    cw = cw_ref[...]
    cb = cb_ref[...]
    sp4 = (0.5 * LRU_C) * jax.nn.softplus(-lam_ref[...])
    sp4e = sp4 * (-LOG2_E)
    ones = (lax.broadcasted_iota(jnp.int32, (blk, LANES), 1) < N_BIAS_TERMS).astype(bf16)

    def gates(jb, carry):
        p0 = jb * blk
        xm2 = xt[pl.ds(p0, blk), :]
        xm1 = xt[pl.ds(p0 + nseg, blk), :]
        x0 = xt[pl.ds(p0 + 2 * nseg, blk), :]
        xp1 = xt[pl.ds(p0 + 3 * nseg, blk), :]
        xch = cb + xm2 * cw[0:1] + xm1 * cw[1:2] + x0 * cw[2:3] + xp1 * cw[3:4]
        lhs = jnp.concatenate([xch.astype(bf16), ones], axis=1)
        gt = jnp.dot(lhs, wg_ref[...], preferred_element_type=f32)
        for d, (a_s, b_s) in enumerate(((af, bfs), (ab, bbs))):
            tr = jnp.tanh(gt[:, (2 * d) * LANES:(2 * d + 1) * LANES])
            ti = jnp.tanh(gt[:, (2 * d + 1) * LANES:(2 * d + 2) * LANES])
            r2 = tr + 1.0
            nla = r2 * sp4[d:d + 1]
            a = jnp.exp2(r2 * sp4e[d:d + 1])
            m2 = jnp.tanh(nla) * (1.0 + a * a)
            mult = m2 * lax.rsqrt(jnp.maximum(m2, 1e-30))
            a_s[pl.ds(p0, blk), :] = a
            b_s[pl.ds(p0, blk), :] = mult * ((ti + 1.0) * xch)
        return carry

    _loop(nseg, gates, 0)

    if chained:
        def ends(j, carry):
            hf, pf, hb, pb = carry
            a = af[tile(j), :]
            hf = a * hf + bfs[tile(j), :]
            pf = a * pf
            a2 = ab[tile(seg - 1 - j), :]
            hb = a2 * hb + bbs[tile(seg - 1 - j), :]
            pb = a2 * pb
            return hf, pf, hb, pb

        zero = jnp.zeros((nseg, LANES), f32)
        one = jnp.ones((nseg, LANES), f32)
        hf_e, pf_e, hb_e, pb_e = _loop(seg, ends, (zero, one, zero, one))
        cf = [h0f_ref[0:1, :]]
        for s in range(1, nseg):
            cf.append(hf_e[s - 1:s] + pf_e[s - 1:s] * cf[-1])
        cbk = [h0b_ref[0:1, :]]
        for s in range(nseg - 2, -1, -1):
            cbk.append(hb_e[s + 1:s + 2] + pb_e[s + 1:s + 2] * cbk[-1])
        hf0 = jnp.concatenate(cf, axis=0)
        hb0 = jnp.concatenate(cbk[::-1], axis=0)
    else:
        hf0 = h0f_ref[...]
        hb0 = h0b_ref[...]

    def scan_b(k, hb):
        j = seg - 1 - k
        hb = ab[tile(j), :] * hb + bbs[tile(j), :]
        xt[tile(j), :] = hb
        return hb

    hb_l = _loop(seg, scan_b, hb0)

    def scan_f(j, hf):
        hf = af[tile(j), :] * hf + bfs[tile(j), :]
        xs[pl.ds(j, nseg, stride=pitch), :] = hf + xt[tile(j), :]
        return hf

    hf_l = _loop(seg, scan_f, hf0)
    if chained:
        hfl_ref[...] = hf_l[nseg - 1:nseg]
        hbf_ref[...] = hb_l[0:1]
    else:
        hfl_ref[...] = hf_l
        hbf_ref[...] = hb_l

    def combine(s, carry):
        g = gate_ref[pl.ds(s * seg, seg), :].astype(f32)
        o_ref[pl.ds(s * seg, seg), :] = (g * xs[pl.ds(s * pitch, seg), :]).astype(bf16)
        return carry

    _loop(nseg, combine, 0)


def _lru_specs(ntok, nseg, gc, pair):
    rows = LRU_SEG * nseg
    nt = ntok // rows
    nb = LRU_BLOCKS
    in_specs = [
        pl.BlockSpec((rows, LANES), lambda *g: pair(*g)),
        pl.BlockSpec((rows, LANES), lambda *g: (pair(*g)[0], nb + pair(*g)[1])),
        pl.BlockSpec((4, LANES), lambda *g: (0, pair(*g)[1])),
        pl.BlockSpec((1, LANES), lambda *g: (0, pair(*g)[1])),
        pl.BlockSpec((None, 2 * LANES, 4 * LANES), lambda *g: (pair(*g)[1], 0, 0)),
        pl.BlockSpec((2, LANES), lambda *g: (0, pair(*g)[1])),
        pl.BlockSpec((None, gc, LANES), lambda *g: (pair(*g)[0], 0, pair(*g)[1])),
        pl.BlockSpec((None, gc, LANES), lambda *g: (pair(*g)[0], 0, pair(*g)[1])),
    ]
    out_specs = [
        pl.BlockSpec((rows, LANES), lambda *g: pair(*g)),
        pl.BlockSpec((None, gc, LANES), lambda *g: (pair(*g)[0], 0, pair(*g)[1])),
        pl.BlockSpec((None, gc, LANES), lambda *g: (pair(*g)[0], 0, pair(*g)[1])),
    ]
    out_shape = [
        jax.ShapeDtypeStruct((ntok, D_RNN), bf16),
        jax.ShapeDtypeStruct((nt, gc, D_RNN), f32),
        jax.ShapeDtypeStruct((nt, gc, D_RNN), f32),
    ]
    scr = pltpu.VMEM((rows, LANES), f32)
    scratch = [pltpu.VMEM((nseg * LRU_PITCH, LANES), f32), pltpu.VMEM((rows + 3 * nseg, LANES), f32),
               scr, scr, scr, scr]
    return in_specs, out_specs, out_shape, scratch


def _outmlp_body(x_ref, mix_ref, g1_ref, sh2_ref, sc2_ref, g2_ref, gain_ref, fgain_ref,
                 wo_ref, w1_ref, w2_ref, *, final):
    mix = jnp.dot(mix_ref[...], wo_ref[...], preferred_element_type=f32)
    x1 = x_ref[...] + g1_ref[...] * mix
    h = (_rms(x1, gain_ref[...]) * (1.0 + sc2_ref[...]) + sh2_ref[...]).astype(bf16)
    acc = None
    for j in range(0, D_FF, FF_CHUNK):
        hid = jnp.dot(h, w1_ref[:, j:j + FF_CHUNK], preferred_element_type=f32)
        hid = jnp.square(jnp.maximum(hid, 0.0)).astype(bf16)
        part = jnp.dot(hid, w2_ref[j:j + FF_CHUNK, :], preferred_element_type=f32)
        acc = part if acc is None else acc + part
    x2 = x1 + g2_ref[...] * acc
    if final:
        x2 = _rms(x2, fgain_ref[...])
    return x2


N_MLP_IN = 11
N_LRU_IN = 8


def _outmlp_kernel(*refs, final, lru):
    n_in = N_MLP_IN + (N_LRU_IN if lru else 0)
    if lru is not None:
        _lru_body(*refs[N_MLP_IN:n_in], *refs[n_in + 1:], nseg=lru[0], chained=lru[1])
    refs[n_in][...] = _outmlp_body(*refs[:N_MLP_IN], final=final)


def _outmlp_call(x3, mix3, modl, gain, fgain, wo, w1, w2, *, layer, final, t, lru_args=None):
    bg, lg, d = x3.shape
    steps = lg // t
    dm = mix3.shape[-1]
    mod_spec = lambda k: pl.BlockSpec((None, 1, d), lambda b, i, k=k: (b, 0, k))

    def resident(shape, lead=None):
        if lead is None:
            return pl.BlockSpec(shape, lambda b, i: (0, 0), pipeline_mode=pl.Buffered(1))
        return pl.BlockSpec((None,) + shape, lambda b, i: (lead, 0, 0), pipeline_mode=pl.Buffered(1))

    in_specs = [
        pl.BlockSpec((None, t, d), lambda b, i: (b, i, 0)),
        pl.BlockSpec((None, t, dm), lambda b, i: (b, i, 0)),
        mod_spec(2), mod_spec(3), mod_spec(4), mod_spec(5),
        pl.BlockSpec((1, d), lambda b, i: (0, 0)),
        pl.BlockSpec((1, d), lambda b, i: (0, 0)),
        resident((dm, d)), resident((d, D_FF), layer), resident((D_FF, d), layer),
    ]
    args = [x3, mix3, modl, modl, modl, modl, gain.reshape(1, d), fgain.reshape(1, d), wo, w1, w2]
    out_specs = [pl.BlockSpec((None, t, d), lambda b, i: (b, i, 0))]
    out_shape = [jax.ShapeDtypeStruct((bg, lg, d), f32)]
    scratch = []
    lru = None
    if lru_args is not None:
        gx, cw, cb, wg, lam, h0f, h0b, nseg, chained = lru_args
        ntok = gx.shape[0]
        assert (ntok // (LRU_SEG * nseg)) * LRU_BLOCKS == bg * steps, "one RG-LRU (tile, block) pair per MLP step"

        def pair(b, i):
            p = b * steps + i
            return p // LRU_BLOCKS, p % LRU_BLOCKS

        li, lo, ls, scratch = _lru_specs(ntok, nseg, h0f.shape[1], pair)
        in_specs += li
        out_specs += lo
        out_shape += ls
        args += [gx, gx, cw, cb, wg, lam, h0f, h0b]
        lru = (nseg, chained)
    res = pl.pallas_call(
        functools.partial(_outmlp_kernel, final=final, lru=lru),
        grid=(bg, steps),
        in_specs=in_specs,
        out_specs=out_specs,
        out_shape=out_shape,
        scratch_shapes=scratch,
        compiler_params=_cparams(("parallel", "parallel"), 58),
        name=("outmlp_final" if final else "outmlp") + ("_rglru" if lru else ""),
    )(*args)
    return res if lru else res[0]


@functools.lru_cache(maxsize=None)
def _rope_tables(length):
    pos = np.arange(length)
    rowp = (pos // GRID_W).astype(np.float64)
    colp = (pos % GRID_W).astype(np.float64)
    nf = RET_DK // 4
    inv_freq = ROPE_BASE ** (-np.arange(nf, dtype=np.float64) / nf)
    ar = rowp[:, None] * inv_freq[None, :]
    ac = colp[:, None] * inv_freq[None, :]
    cos = np.concatenate([np.cos(ar), np.cos(ar), np.cos(ac), np.cos(ac)], axis=-1).astype(np.float32)
    sin = np.concatenate([-np.sin(ar), np.sin(ar), -np.sin(ac), np.sin(ac)], axis=-1).astype(np.float32)
    return cos, sin


def kernel(x_prompt, x_sample, c, c_ctx, state_ret_fwd, state_ret_bwd, state_lru_fwd, state_lru_bwd, mod_w, mod_b, norm_mix, norm_mlp, mlp_w1, mlp_w2, ab_w_in, ab_w_out, ret_decay_fwd, ret_decay_bwd, gmlp_ws, gmlp_bs, lru_w_in, lru_conv_w, lru_conv_b, lru_gate_a_w, lru_gate_a_b, lru_gate_x_w, lru_gate_x_b, lru_lambda, lru_w_out, final_norm):
    d = D_MODEL
    n_lat = c.shape[0]
    conds = jnp.concatenate([c_ctx[None, :], c, jnp.zeros((COND_ROWS - 1 - n_lat, d), f32)], axis=0)
    mod = _mod_call(conds, mod_w, mod_b)

    w_in0 = ab_w_in[0].astype(bf16)
    w_out0 = ab_w_out[0].astype(bf16)
    w1 = mlp_w1.astype(bf16)
    w2 = mlp_w2.astype(bf16)
    w_in1 = lru_w_in[0].astype(bf16)
    w_out1 = lru_w_out[0].astype(bf16)
    ws = gmlp_ws[0].astype(bf16)
    bs = gmlp_bs[0][:, :, None]
    dec = jnp.broadcast_to(jnp.stack([ret_decay_fwd[0], ret_decay_bwd[0]])[:, :, None, None],
                           (2, RET_HEADS, 1, LANES))
    wg = jnp.concatenate([lru_gate_a_w[0, 0], lru_gate_x_w[0, 0], lru_gate_a_w[0, 1], lru_gate_x_w[0, 1]],
                         axis=-1).astype(bf16)
    bgate = 0.5 * jnp.stack([lru_gate_a_b[0, 0], lru_gate_x_b[0, 0], lru_gate_a_b[0, 1], lru_gate_x_b[0, 1]], axis=0)
    bgate = bgate.reshape(4, LRU_BLOCKS, LRU_BS).transpose(1, 0, 2).reshape(LRU_BLOCKS, 1, 4 * LRU_BS)
    terms, rem = [], bgate
    for _ in range(N_BIAS_TERMS):
        terms.append(rem.astype(bf16))
        rem = rem - terms[-1].astype(f32)
    bias_rows = jnp.pad(jnp.concatenate(terms, axis=1), ((0, 0), (0, LANES - N_BIAS_TERMS), (0, 0)))
    wg = jnp.concatenate([wg, bias_rows], axis=1)
    lru_w = (0.5 * lru_conv_w[0], 0.5 * lru_conv_b[0].reshape(1, D_RNN), wg, lru_lambda[0])

    def mixer0(x3, modl, bsz, ls, ret_f0, ret_b0, latent):
        tc = min(4, ls // RET_CHUNK)
        if latent:
            qkv, sbn = _inproj_call(x3, modl, norm_mix[0], w_in0, _rope_tables(ls), retention=True,
                                    sh_col=0, sc_col=1, bwd=(dec, ret_b0))
            mix = _mix0_call(qkv, dec, ret_f0, sbn, ws, bs, tc)[0]
            sff = sbf = None
        else:
            qkv = _inproj_call(x3, modl, norm_mix[0], w_in0, None, retention=True, sh_col=0, sc_col=1)
            mix, sff, sbf = _mix0_call(qkv.reshape(bsz, ls, AB_IN), dec, None, None, ws, bs, tc)
        return mix.reshape(x3.shape[0], x3.shape[1], AB_OUT), sff, sbf

    def lru_inproj(x3, modl):
        gx = _inproj_call(x3, modl, norm_mix[1], w_in1, None, retention=False, sh_col=0, sc_col=1)
        return gx.reshape(-1, 2 * D_RNN)

    bp, lp, _ = x_prompt.shape
    bl, ll, _ = x_sample.shape
    assert lp == LRU_SEG and ll % LRU_SEG == 0
    xc = x_prompt.reshape(1, bp * lp, d)
    xl = x_sample
    mod_c = [mod[l, 0:1][:, None, :] for l in range(DEPTH)]
    mod_l = [mod[l, 1:1 + n_lat][:, None, :] for l in range(DEPTH)]

    mix_c, sff, sbf = mixer0(xc, mod_c[0], bp, lp, None, None, False)
    xc = _outmlp_call(xc, mix_c, mod_c[0], norm_mlp[0], final_norm, w_out0, w1, w2, layer=0, final=False,
                      t=PLAIN_MLP_TILE)
    gx_c = lru_inproj(xc, mod_c[1])
    mix_l, _, _ = mixer0(xl, mod_l[0], bl, ll, state_ret_fwd[:, 0], state_ret_bwd[:, 0], True)

    steps_l = bl * (ll // TOK_TILE)
    nseg_c = (bp * lp // LRU_SEG) * LRU_BLOCKS // steps_l
    zero_lru = jnp.zeros((bp // nseg_c, nseg_c, D_RNN), f32)
    xl, pre_c, hfl, hbf = _outmlp_call(
        xl, mix_l, mod_l[0], norm_mlp[0], final_norm, w_out0, w1, w2, layer=0, final=False, t=TOK_TILE,
        lru_args=(gx_c,) + lru_w + (zero_lru, zero_lru, nseg_c, False))
    gx_l = lru_inproj(xl, mod_l[1])

    nseg_l = ll // LRU_SEG
    t_c = bp * lp // (bl * LRU_BLOCKS)
    y_c, pre_l, _, _ = _outmlp_call(
        xc, pre_c.reshape(1, bp * lp, D_RNN), mod_c[1], norm_mlp[1], final_norm, w_out1, w1, w2,
        layer=1, final=True, t=t_c,
        lru_args=(gx_l,) + lru_w + (state_lru_fwd[:, 0].reshape(bl, 1, D_RNN),
                                    state_lru_bwd[:, 0].reshape(bl, 1, D_RNN), nseg_l, True))
    y_l = _outmlp_call(xl, pre_l.reshape(bl, ll, D_RNN), mod_l[1], norm_mlp[1], final_norm, w_out1, w1, w2,
                       layer=1, final=True, t=PLAIN_MLP_TILE)
    return (y_c.reshape(bp, lp, d), y_l, sff[:, None], sbf[:, None],
            hfl.reshape(bp, 1, D_RNN), hbf.reshape(bp, 1, D_RNN))
```

```python
import functools

import jax
import jax.numpy as jnp
import numpy as np
from jax import lax
from jax.experimental import pallas as pl
from jax.experimental.pallas import tpu as pltpu

f32 = jnp.float32
bf16 = jnp.bfloat16

D_MODEL = 1024
DEPTH = 2
GRID_W = 64
EPS = 1e-6
RET_HEADS = 4
RET_DK = 128
RET_DV = 128
RET_CHUNK = 128
ROPE_BASE = 10000.0
GMLP_GROUPS = 4
GMLP_CH = 128
AB_IN = 3072
AB_OUT = 1024
D_RNN = D_MODEL
LRU_BLOCKS = 8
LRU_BS = 128
LRU_C = 8.0
D_FF = 4 * D_MODEL

LANES = 128
SUBLANES = 8
MIB = 1024 * 1024

TOK_TILE = 512
PLAIN_MLP_TILE = 1024
INPROJ_TILE = 1024
COND_ROWS = 8
MOD_TN = 1536
LRU_SEG = 256
LRU_PITCH = LRU_SEG + SUBLANES
FF_CHUNK = 1024
N_BIAS_TERMS = 3
LOG2_E = 1.4426950408889634
GELU_C1 = 0.7978845608028654
GELU_C3 = 0.044715 * GELU_C1
PROJ_COLS = 512
ROPE_ROT = RET_DK // 4
VMEM_MIX_MIB = 32
VMEM_PROJ_MIB = 48
VMEM_MLP_MIB = 58


def _cparams(sem, vmem_mib):
    return pltpu.CompilerParams(dimension_semantics=sem, vmem_limit_bytes=vmem_mib * MIB)


def _gelu(x):
    hx = 0.5 * x
    return hx + hx * jnp.tanh(x * (GELU_C1 + GELU_C3 * (x * x)))


def _silu(x):
    hx = 0.5 * x
    return hx + hx * jnp.tanh(hx)


def _rms(x, gain):
    ms = jnp.mean(x * x, axis=-1, keepdims=True)
    return x * lax.rsqrt(ms + EPS) * gain


def _mod_kernel(cond_ref, w_ref, b_ref, o_ref):
    c = cond_ref[...]
    s = c * jax.nn.sigmoid(c)
    o_ref[...] = jnp.dot(s.astype(bf16), w_ref[...].astype(bf16), preferred_element_type=f32) + b_ref[...]


def _mod_call(conds, mod_w, mod_b):
    depth, d, n = mod_w.shape
    return pl.pallas_call(
        _mod_kernel,
        grid=(depth, n // MOD_TN),
        in_specs=[
            pl.BlockSpec((COND_ROWS, d), lambda l, j: (0, 0)),
            pl.BlockSpec((None, d, MOD_TN), lambda l, j: (l, 0, j)),
            pl.BlockSpec((None, 1, MOD_TN), lambda l, j: (l, 0, j)),
        ],
        out_specs=pl.BlockSpec((None, COND_ROWS, MOD_TN), lambda l, j: (l, 0, j)),
        out_shape=jax.ShapeDtypeStruct((depth, COND_ROWS, n), f32),
        compiler_params=_cparams(("parallel", "parallel"), VMEM_MIX_MIB),
        name="mod",
    )(conds, mod_w, mod_b.reshape(depth, 1, n))


def _rope(t, cos, sin_signed, first_half):
    up = pltpu.roll(t, LANES - ROPE_ROT, axis=1)
    dn = pltpu.roll(t, ROPE_ROT, axis=1)
    return t * cos + jnp.where(first_half, up, dn) * sin_signed


def _inproj_kernel(*refs, retention, rope, bwd_states):
    if bwd_states:
        (x_ref, sh_ref, sc_ref, gain_ref, w_ref, cos_ref, sin_ref, dec_ref, s0b_ref,
         o_ref, sbn_ref, sb_scr) = refs

        @pl.when(pl.program_id(1) == 0)
        def _():
            sb_scr[...] = s0b_ref[...]
    elif rope:
        x_ref, sh_ref, sc_ref, gain_ref, w_ref, cos_ref, sin_ref, o_ref = refs
    else:
        x_ref, sh_ref, sc_ref, gain_ref, w_ref, o_ref = refs
    h = _rms(x_ref[...], gain_ref[...]) * (1.0 + sc_ref[...]) + sh_ref[...]
    hb = h.astype(bf16)
    n = w_ref.shape[1]
    if not retention:
        for j in range(0, n, PROJ_COLS):
            y = jnp.dot(hb, w_ref[:, j:j + PROJ_COLS], preferred_element_type=f32)
            o_ref[:, j:j + PROJ_COLS] = (_gelu(y) if j < n // 2 else y).astype(bf16)
        return
    qk = RET_HEADS * RET_DK
    if rope:
        cos = cos_ref[...]
        sin = sin_ref[...]
        lane = lax.broadcasted_iota(jnp.int32, cos.shape, 1)
        first_half = (lane % (2 * ROPE_ROT)) < ROPE_ROT
    k_heads = []
    for part in range(2):
        y = jnp.dot(hb, w_ref[:, part * qk:(part + 1) * qk], preferred_element_type=f32)
        for hh in range(RET_HEADS):
            t = y[:, hh * RET_DK:(hh + 1) * RET_DK]
            if part == 1:
                t = t * (RET_DK ** -0.5)
            if rope:
                t = _rope(t, cos, sin, first_half)
            c0 = part * qk + hh * RET_DK
            tb = t.astype(bf16)
            o_ref[:, c0:c0 + RET_DK] = tb
            if part == 1:
                k_heads.append(tb)
    acts = (None, _silu, _gelu, _gelu)
    assert qk == PROJ_COLS and n - 2 * qk == len(acts) * PROJ_COLS
    for j, act in zip(range(2 * qk, n, PROJ_COLS), acts):
        y = jnp.dot(hb, w_ref[:, j:j + PROJ_COLS], preferred_element_type=f32)
        yb = (y if act is None else act(y)).astype(bf16)
        o_ref[:, j:j + PROJ_COLS] = yb
        if bwd_states and j == 2 * qk:
            vb = yb
    if bwd_states:
        row = lax.broadcasted_iota(jnp.int32, (RET_CHUNK, LANES), 0).astype(f32)
        nchunk = x_ref.shape[0] // RET_CHUNK
        for hh in range(RET_HEADS):
            lg_b = jax.nn.log_sigmoid(dec_ref[1, hh])
            kd_b = jnp.exp(lg_b * row)
            gc_b = jnp.exp(lg_b * float(RET_CHUNK))
            cs = slice(hh * RET_DK, (hh + 1) * RET_DK)
            for cc in reversed(range(nchunk)):
                rs = slice(cc * RET_CHUNK, (cc + 1) * RET_CHUNK)
                s_next = sb_scr[hh]
                sbn_ref[cc, hh] = s_next.astype(bf16)
                kd = (k_heads[hh][rs].astype(f32) * kd_b).astype(bf16)
                kv = lax.dot_general(kd, vb[rs, cs], (((0,), (0,)), ((), ())), preferred_element_type=f32)
                sb_scr[hh] = gc_b * s_next + kv


def _inproj_call(x3, modl, gain, w, rope_tabs, *, retention, sh_col, sc_col, bwd=None):
    bg, lg, d = x3.shape
    n = w.shape[1]
    t = INPROJ_TILE
    ns = lg // t
    rope = rope_tabs is not None
    pos = (lambda i: ns - 1 - i) if bwd is not None else (lambda i: i)
    in_specs = [
        pl.BlockSpec((None, t, d), lambda b, i: (b, pos(i), 0)),
        pl.BlockSpec((None, 1, d), lambda b, i: (b, 0, sh_col)),
        pl.BlockSpec((None, 1, d), lambda b, i: (b, 0, sc_col)),
        pl.BlockSpec((1, d), lambda b, i: (0, 0)),
        pl.BlockSpec((d, n), lambda b, i: (0, 0), pipeline_mode=pl.Buffered(1)),
    ]
    args = [x3, modl, modl, gain.reshape(1, d), w]
    if rope:
        in_specs += [pl.BlockSpec((t, LANES), lambda b, i: (pos(i), 0))] * 2
        args += list(rope_tabs)
    out_specs = [pl.BlockSpec((None, t, n), lambda b, i: (b, pos(i), 0))]
    out_shape = [jax.ShapeDtypeStruct((bg, lg, n), bf16)]
    scratch = []
    if bwd is not None:
        st = (RET_HEADS, RET_DK, RET_DV)
        tc = t // RET_CHUNK
        in_specs += [pl.BlockSpec((2, RET_HEADS, 1, LANES), lambda b, i: (0, 0, 0, 0)),
                     pl.BlockSpec((None,) + st, lambda b, i: (b, 0, 0, 0))]
        args += list(bwd)
        out_specs.append(pl.BlockSpec((None, tc) + st, lambda b, i: (b, pos(i), 0, 0, 0)))
        out_shape.append(jax.ShapeDtypeStruct((bg, lg // RET_CHUNK) + st, bf16))
        scratch.append(pltpu.VMEM(st, f32))
    res = pl.pallas_call(
        functools.partial(_inproj_kernel, retention=retention, rope=rope, bwd_states=bwd is not None),
        grid=(bg, ns),
        in_specs=in_specs,
        out_specs=out_specs,
        out_shape=out_shape,
        scratch_shapes=scratch,
        compiler_params=_cparams(("parallel", "arbitrary" if bwd is not None else "parallel"), VMEM_PROJ_MIB),
        name="inproj_ret" if retention else "inproj_lru",
    )(*args)
    return res if bwd is not None else res[0]


def _row_iota():
    return lax.broadcasted_iota(jnp.int32, (RET_CHUNK, LANES), 0).astype(f32)


def _mix0_kernel(*refs, tc, whole_seq):
    if whole_seq:
        dec_ref, q_ref, k_ref, v_ref, g_ref, u_ref, vv_ref, ws_ref, bs_ref, o_ref, sff_ref, sbf_ref = refs
    else:
        (dec_ref, q_ref, k_ref, v_ref, g_ref, u_ref, vv_ref, s0f_ref, sbn_ref, ws_ref, bs_ref,
         o_ref, sf_scr) = refs

        @pl.when(pl.program_id(1) == 0)
        def _():
            sf_scr[...] = s0f_ref[...]

    row = _row_iota()
    col = lax.broadcasted_iota(jnp.int32, (RET_CHUNK, LANES), 1).astype(f32)
    c = float(RET_CHUNK)
    nt = (((1,), (1,)), ((), ()))
    tn = (((0,), (0,)), ((), ()))
    for hh in range(RET_HEADS):
        lg_f = jax.nn.log_sigmoid(dec_ref[0, hh])
        lg_b = jax.nn.log_sigmoid(dec_ref[1, hh])
        decay = jnp.where(row >= col,
                          jnp.exp(lg_f * jnp.maximum(row - col, 0.0)),
                          jnp.exp(lg_b * jnp.maximum(col - row - 1.0, 0.0)))
        qd_f = jnp.exp(lg_f * (row + 1.0))
        qd_b = jnp.exp(lg_b * (c - 1.0 - row))
        cs = slice(hh * RET_DK, (hh + 1) * RET_DK)
        kd_f = jnp.exp(lg_f * (c - 1.0 - row))
        kd_b = jnp.exp(lg_b * row)
        gc_f = jnp.exp(lg_f * c)
        gc_b = jnp.exp(lg_b * c)
        s_prev, s_next = [None] * tc, [None] * tc
        sf = None if whole_seq else sf_scr[hh]
        sb = None
        for cc in range(tc):
            rs = slice(cc * RET_CHUNK, (cc + 1) * RET_CHUNK)
            kf = k_ref[rs, cs].astype(f32)
            kv = lax.dot_general((kf * kd_f).astype(bf16), v_ref[rs, cs], tn, preferred_element_type=f32)
            s_prev[cc] = sf
            sf = kv if sf is None else gc_f * sf + kv
        if whole_seq:
            for cc in reversed(range(tc)):
                rs = slice(cc * RET_CHUNK, (cc + 1) * RET_CHUNK)
                kf = k_ref[rs, cs].astype(f32)
                kv = lax.dot_general((kf * kd_b).astype(bf16), v_ref[rs, cs], tn, preferred_element_type=f32)
                s_next[cc] = sb
                sb = kv if sb is None else gc_b * sb + kv
            sff_ref[hh] = sf
            sbf_ref[hh] = sb
        else:
            sf_scr[hh] = sf
        for cc in range(tc):
            rs = slice(cc * RET_CHUNK, (cc + 1) * RET_CHUNK)
            q = q_ref[rs, cs]
            qf = q.astype(f32)
            s = lax.dot_general(q, k_ref[rs, cs], nt, preferred_element_type=f32)
            o = jnp.dot((s * decay).astype(bf16), v_ref[rs, cs], preferred_element_type=f32)
            sp_c = None if s_prev[cc] is None else s_prev[cc].astype(bf16)
            if whole_seq:
                sn_c = None if s_next[cc] is None else s_next[cc].astype(bf16)
            else:
                sn_c = sbn_ref[cc, hh]
            if sp_c is not None:
                o = o + jnp.dot((qf * qd_f).astype(bf16), sp_c, preferred_element_type=f32)
            if sn_c is not None:
                o = o + jnp.dot((qf * qd_b).astype(bf16), sn_c, preferred_element_type=f32)
            mu = jnp.mean(o, axis=-1, keepdims=True)
            var = jnp.mean(jnp.square(o - mu), axis=-1, keepdims=True)
            on = (o - mu) * lax.rsqrt(var + EPS)
            o_ref[rs, cs] = (g_ref[rs, cs].astype(f32) * on).astype(bf16)
    off = RET_HEADS * RET_DV
    for gg in range(GMLP_GROUPS):
        cs = slice(gg * GMLP_CH, (gg + 1) * GMLP_CH)
        w = ws_ref[gg]
        bias = bs_ref[gg]
        for cc in range(tc):
            rs = slice(cc * RET_CHUNK, (cc + 1) * RET_CHUNK)
            gu = u_ref[rs, cs].astype(f32)
            gv = vv_ref[rs, cs].astype(f32)
            gv = gv * lax.rsqrt(jnp.mean(gv * gv, axis=-1, keepdims=True) + EPS)
            sp = jnp.dot(w, gv.astype(bf16), preferred_element_type=f32) + bias
            o_ref[rs, off + gg * GMLP_CH: off + (gg + 1) * GMLP_CH] = (gu * sp).astype(bf16)


def _mix0_call(qkv, dec, s0f, sbn, ws, bs, tc):
    bsz, ls, _ = qkv.shape
    t = tc * RET_CHUNK
    hw = RET_HEADS * RET_DK
    st = (RET_HEADS, RET_DK, RET_DV)
    col_spec = lambda j: pl.BlockSpec((None, t, hw), lambda b, i, j=j: (b, i, j))
    whole_seq = s0f is None
    in_specs = [
        pl.BlockSpec((2, RET_HEADS, 1, LANES), lambda b, i: (0, 0, 0, 0)),
        col_spec(0), col_spec(1), col_spec(2), col_spec(3), col_spec(4), col_spec(5),
    ]
    args = [dec, qkv, qkv, qkv, qkv, qkv, qkv]
    out_specs = [pl.BlockSpec((None, t, AB_OUT), lambda b, i: (b, i, 0))]
    out_shape = [jax.ShapeDtypeStruct((bsz, ls, AB_OUT), bf16)]
    if whole_seq:
        assert ls == t
        out_specs += [pl.BlockSpec((None,) + st, lambda b, i: (b, 0, 0, 0))] * 2
        out_shape += [jax.ShapeDtypeStruct((bsz,) + st, f32)] * 2
    else:
        in_specs += [pl.BlockSpec((None,) + st, lambda b, i: (b, 0, 0, 0)),
                     pl.BlockSpec((None, tc) + st, lambda b, i: (b, i, 0, 0, 0))]
        args += [s0f, sbn]
    in_specs += [
        pl.BlockSpec((GMLP_GROUPS, RET_CHUNK, RET_CHUNK), lambda b, i: (0, 0, 0)),
        pl.BlockSpec((GMLP_GROUPS, RET_CHUNK, 1), lambda b, i: (0, 0, 0)),
    ]
    args += [ws, bs]
    return pl.pallas_call(
        functools.partial(_mix0_kernel, tc=tc, whole_seq=whole_seq),
        grid=(bsz, ls // t),
        in_specs=in_specs,
        out_specs=out_specs,
        out_shape=out_shape,
        scratch_shapes=[] if whole_seq else [pltpu.VMEM(st, f32)],
        compiler_params=_cparams(("parallel", "parallel" if whole_seq else "arbitrary"), VMEM_MIX_MIB),
        name="mix0_seq" if whole_seq else "mix0",
    )(*args)


def _loop(n, body, carry):
    for i in range(n):
        carry = body(i, carry)
    return carry


def _lru_body(gate_ref, xin_ref, cw_ref, cb_ref, wg_ref, lam_ref, h0f_ref, h0b_ref,
              o_ref, hfl_ref, hbf_ref, xs, xt, af, bfs, ab, bbs, *, nseg, chained):
    seg, pitch = LRU_SEG, LRU_PITCH
    blk = seg

    def stage(s, carry):
        xs[pl.ds(s * pitch, seg), :] = xin_ref[pl.ds(s * seg, seg), :].astype(f32)
        return carry

    _loop(nseg, stage, 0)

    def tile(j):
        return pl.ds(j * nseg, nseg)

    def gather(j, carry):
        xt[tile(j + 2), :] = xs[pl.ds(j, nseg, stride=pitch), :]
        return carry

    _loop(seg, gather, 0)
    if chained:
        srow = lax.broadcasted_iota(jnp.int32, (nseg, LANES), 0)
        xt[tile(0), :] = jnp.where(srow == 0, 0.0, pltpu.roll(xt[tile(seg), :], 1, axis=0))
        xt[tile(1), :] = jnp.where(srow == 0, 0.0, pltpu.roll(xt[tile(seg + 1), :], 1, axis=0))
        xt[tile(seg + 2), :] = jnp.where(srow == nseg - 1, 0.0, pltpu.roll(xt[tile(2), :], nseg - 1, axis=0))
    else:
        zt = jnp.zeros((nseg, LANES), f32)
        xt[tile(0), :] = zt
        xt[tile(1), :] = zt
        xt[tile(seg + 2), :] = zt

    cw = cw_ref[...]
    cb = cb_ref[...]
    sp4 = (0.5 * LRU_C) * jax.nn.softplus(-lam_ref[...])
    sp4e = sp4 * (-LOG2_E)
    ones = (lax.broadcasted_iota(jnp.int32, (blk, LANES), 1) < N_BIAS_TERMS).astype(bf16)

    def gates(jb, carry):
        p0 = jb * blk
        xm2 = xt[pl.ds(p0, blk), :]
        xm1 = xt[pl.ds(p0 + nseg, blk), :]
        x0 = xt[pl.ds(p0 + 2 * nseg, blk), :]
        xp1 = xt[pl.ds(p0 + 3 * nseg, blk), :]
        xch = cb + xm2 * cw[0:1] + xm1 * cw[1:2] + x0 * cw[2:3] + xp1 * cw[3:4]
        lhs = jnp.concatenate([xch.astype(bf16), ones], axis=1)
        gt = jnp.dot(lhs, wg_ref[...], preferred_element_type=f32)
        for d, (a_s, b_s) in enumerate(((af, bfs), (ab, bbs))):
            tr = jnp.tanh(gt[:, (2 * d) * LANES:(2 * d + 1) * LANES])
            ti = jnp.tanh(gt[:, (2 * d + 1) * LANES:(2 * d + 2) * LANES])
            r2 = tr + 1.0
            nla = r2 * sp4[d:d + 1]
            a = jnp.exp2(r2 * sp4e[d:d + 1])
            m2 = jnp.tanh(nla) * (1.0 + a * a)
            mult = m2 * lax.rsqrt(jnp.maximum(m2, 1e-30))
            a_s[pl.ds(p0, blk), :] = a
            b_s[pl.ds(p0, blk), :] = mult * ((ti + 1.0) * xch)
        return carry

    _loop(nseg, gates, 0)

    if chained:
        def ends(j, carry):
            hf, pf, hb, pb = carry
            a = af[tile(j), :]
            hf = a * hf + bfs[tile(j), :]
            pf = a * pf
            a2 = ab[tile(seg - 1 - j), :]
            hb = a2 * hb + bbs[tile(seg - 1 - j), :]
            pb = a2 * pb
            return hf, pf, hb, pb

        zero = jnp.zeros((nseg, LANES), f32)
        one = jnp.ones((nseg, LANES), f32)
        hf_e, pf_e, hb_e, pb_e = _loop(seg, ends, (zero, one, zero, one))
        cf = [h0f_ref[0:1, :]]
        for s in range(1, nseg):
            cf.append(hf_e[s - 1:s] + pf_e[s - 1:s] * cf[-1])
        cbk = [h0b_ref[0:1, :]]
        for s in range(nseg - 2, -1, -1):
            cbk.append(hb_e[s + 1:s + 2] + pb_e[s + 1:s + 2] * cbk[-1])
        hf0 = jnp.concatenate(cf, axis=0)
        hb0 = jnp.concatenate(cbk[::-1], axis=0)
    else:
        hf0 = h0f_ref[...]
        hb0 = h0b_ref[...]

    def scan_b(k, hb):
        j = seg - 1 - k
        hb = ab[tile(j), :] * hb + bbs[tile(j), :]
        xt[tile(j), :] = hb
        return hb

    hb_l = _loop(seg, scan_b, hb0)

    def scan_f(j, hf):
        hf = af[tile(j), :] * hf + bfs[tile(j), :]
        xs[pl.ds(j, nseg, stride=pitch), :] = hf + xt[tile(j), :]
        return hf

    hf_l = _loop(seg, scan_f, hf0)
    if chained:
        hfl_ref[...] = hf_l[nseg - 1:nseg]
        hbf_ref[...] = hb_l[0:1]
    else:
        hfl_ref[...] = hf_l
        hbf_ref[...] = hb_l

    def combine(s, carry):
        g = gate_ref[pl.ds(s * seg, seg), :].astype(f32)
        o_ref[pl.ds(s * seg, seg), :] = (g * xs[pl.ds(s * pitch, seg), :]).astype(bf16)
        return carry

    _loop(nseg, combine, 0)


def _lru_specs(ntok, nseg, gc, pair):
    rows = LRU_SEG * nseg
    nt = ntok // rows
    nb = LRU_BLOCKS
    in_specs = [
        pl.BlockSpec((rows, LANES), lambda *g: pair(*g)),
        pl.BlockSpec((rows, LANES), lambda *g: (pair(*g)[0], nb + pair(*g)[1])),
        pl.BlockSpec((4, LANES), lambda *g: (0, pair(*g)[1])),
        pl.BlockSpec((1, LANES), lambda *g: (0, pair(*g)[1])),
        pl.BlockSpec((None, 2 * LANES, 4 * LANES), lambda *g: (pair(*g)[1], 0, 0)),
        pl.BlockSpec((2, LANES), lambda *g: (0, pair(*g)[1])),
        pl.BlockSpec((None, gc, LANES), lambda *g: (pair(*g)[0], 0, pair(*g)[1])),
        pl.BlockSpec((None, gc, LANES), lambda *g: (pair(*g)[0], 0, pair(*g)[1])),
    ]
    out_specs = [
        pl.BlockSpec((rows, LANES), lambda *g: pair(*g)),
        pl.BlockSpec((None, gc, LANES), lambda *g: (pair(*g)[0], 0, pair(*g)[1])),
        pl.BlockSpec((None, gc, LANES), lambda *g: (pair(*g)[0], 0, pair(*g)[1])),
    ]
    out_shape = [
        jax.ShapeDtypeStruct((ntok, D_RNN), bf16),
        jax.ShapeDtypeStruct((nt, gc, D_RNN), f32),
        jax.ShapeDtypeStruct((nt, gc, D_RNN), f32),
    ]
    scr = pltpu.VMEM((rows, LANES), f32)
    scratch = [pltpu.VMEM((nseg * LRU_PITCH, LANES), f32), pltpu.VMEM((rows + 3 * nseg, LANES), f32),
               scr, scr, scr, scr]
    return in_specs, out_specs, out_shape, scratch


def _outmlp_body(x_ref, mix_ref, g1_ref, sh2_ref, sc2_ref, g2_ref, gain_ref, fgain_ref,
                 wo_ref, w1_ref, w2_ref, *, final):
    mix = jnp.dot(mix_ref[...], wo_ref[...], preferred_element_type=f32)
    x1 = x_ref[...] + g1_ref[...] * mix
    h = (_rms(x1, gain_ref[...]) * (1.0 + sc2_ref[...]) + sh2_ref[...]).astype(bf16)
    acc = None
    for j in range(0, D_FF, FF_CHUNK):
        hid = jnp.dot(h, w1_ref[:, j:j + FF_CHUNK], preferred_element_type=f32)
        hid = jnp.square(jnp.maximum(hid, 0.0)).astype(bf16)
        part = jnp.dot(hid, w2_ref[j:j + FF_CHUNK, :], preferred_element_type=f32)
        acc = part if acc is None else acc + part
    x2 = x1 + g2_ref[...] * acc
    if final:
        x2 = _rms(x2, fgain_ref[...])
    return x2


N_MLP_IN = 11
N_LRU_IN = 8


def _outmlp_kernel(*refs, final, lru):
    n_in = N_MLP_IN + (N_LRU_IN if lru else 0)
    if lru is not None:
        _lru_body(*refs[N_MLP_IN:n_in], *refs[n_in + 1:], nseg=lru[0], chained=lru[1])
    refs[n_in][...] = _outmlp_body(*refs[:N_MLP_IN], final=final)


def _outmlp_call(x3, mix3, modl, gain, fgain, wo, w1, w2, *, layer, final, t, lru_args=None):
    bg, lg, d = x3.shape
    steps = lg // t
    dm = mix3.shape[-1]
    mod_spec = lambda k: pl.BlockSpec((None, 1, d), lambda b, i, k=k: (b, 0, k))

    def resident(shape, lead=None):
        if lead is None:
            return pl.BlockSpec(shape, lambda b, i: (0, 0), pipeline_mode=pl.Buffered(1))
        return pl.BlockSpec((None,) + shape, lambda b, i: (lead, 0, 0), pipeline_mode=pl.Buffered(1))

    in_specs = [
        pl.BlockSpec((None, t, d), lambda b, i: (b, i, 0)),
        pl.BlockSpec((None, t, dm), lambda b, i: (b, i, 0)),
        mod_spec(2), mod_spec(3), mod_spec(4), mod_spec(5),
        pl.BlockSpec((1, d), lambda b, i: (0, 0)),
        pl.BlockSpec((1, d), lambda b, i: (0, 0)),
        resident((dm, d)), resident((d, D_FF), layer), resident((D_FF, d), layer),
    ]
    args = [x3, mix3, modl, modl, modl, modl, gain.reshape(1, d), fgain.reshape(1, d), wo, w1, w2]
    out_specs = [pl.BlockSpec((None, t, d), lambda b, i: (b, i, 0))]
    out_shape = [jax.ShapeDtypeStruct((bg, lg, d), f32)]
    scratch = []
    lru = None
    if lru_args is not None:
        gx, cw, cb, wg, lam, h0f, h0b, nseg, chained = lru_args
        ntok = gx.shape[0]
        assert (ntok // (LRU_SEG * nseg)) * LRU_BLOCKS == bg * steps, "one RG-LRU (tile, block) pair per MLP step"

        def pair(b, i):
            p = b * steps + i
            return p // LRU_BLOCKS, p % LRU_BLOCKS

        li, lo, ls, scratch = _lru_specs(ntok, nseg, h0f.shape[1], pair)
        in_specs += li
        out_specs += lo
        out_shape += ls
        args += [gx, gx, cw, cb, wg, lam, h0f, h0b]
        lru = (nseg, chained)
    res = pl.pallas_call(
        functools.partial(_outmlp_kernel, final=final, lru=lru),
        grid=(bg, steps),
        in_specs=in_specs,
        out_specs=out_specs,
        out_shape=out_shape,
        scratch_shapes=scratch,
        compiler_params=_cparams(("parallel", "parallel"), VMEM_MLP_MIB),
        name=("outmlp_final" if final else "outmlp") + ("_rglru" if lru else ""),
    )(*args)
    return res if lru else res[0]


@functools.lru_cache(maxsize=None)
def _rope_tables(length):
    pos = np.arange(length)
    rowp = (pos // GRID_W).astype(np.float64)
    colp = (pos % GRID_W).astype(np.float64)
    nf = RET_DK // 4
    inv_freq = ROPE_BASE ** (-np.arange(nf, dtype=np.float64) / nf)
    ar = rowp[:, None] * inv_freq[None, :]
    ac = colp[:, None] * inv_freq[None, :]
    cos = np.concatenate([np.cos(ar), np.cos(ar), np.cos(ac), np.cos(ac)], axis=-1).astype(np.float32)
    sin = np.concatenate([-np.sin(ar), np.sin(ar), -np.sin(ac), np.sin(ac)], axis=-1).astype(np.float32)
    return cos, sin


def kernel(x_prompt, x_sample, c, c_ctx, state_ret_fwd, state_ret_bwd, state_lru_fwd, state_lru_bwd, mod_w, mod_b, norm_mix, norm_mlp, mlp_w1, mlp_w2, ab_w_in, ab_w_out, ret_decay_fwd, ret_decay_bwd, gmlp_ws, gmlp_bs, lru_w_in, lru_conv_w, lru_conv_b, lru_gate_a_w, lru_gate_a_b, lru_gate_x_w, lru_gate_x_b, lru_lambda, lru_w_out, final_norm):
    d = D_MODEL
    n_lat = c.shape[0]
    conds = jnp.concatenate([c_ctx[None, :], c, jnp.zeros((COND_ROWS - 1 - n_lat, d), f32)], axis=0)
    mod = _mod_call(conds, mod_w, mod_b)

    w_in0 = ab_w_in[0].astype(bf16)
    w_out0 = ab_w_out[0].astype(bf16)
    w1 = mlp_w1.astype(bf16)
    w2 = mlp_w2.astype(bf16)
    w_in1 = lru_w_in[0].astype(bf16)
    w_out1 = lru_w_out[0].astype(bf16)
    ws = gmlp_ws[0].astype(bf16)
    bs = gmlp_bs[0][:, :, None]
    dec = jnp.broadcast_to(jnp.stack([ret_decay_fwd[0], ret_decay_bwd[0]])[:, :, None, None],
                           (2, RET_HEADS, 1, LANES))
    wg = jnp.concatenate([lru_gate_a_w[0, 0], lru_gate_x_w[0, 0], lru_gate_a_w[0, 1], lru_gate_x_w[0, 1]],
                         axis=-1).astype(bf16)
    bgate = 0.5 * jnp.stack([lru_gate_a_b[0, 0], lru_gate_x_b[0, 0], lru_gate_a_b[0, 1], lru_gate_x_b[0, 1]], axis=0)
    bgate = bgate.reshape(4, LRU_BLOCKS, LRU_BS).transpose(1, 0, 2).reshape(LRU_BLOCKS, 1, 4 * LRU_BS)
    terms, rem = [], bgate
    for _ in range(N_BIAS_TERMS):
        terms.append(rem.astype(bf16))
        rem = rem - terms[-1].astype(f32)
    bias_rows = jnp.pad(jnp.concatenate(terms, axis=1), ((0, 0), (0, LANES - N_BIAS_TERMS), (0, 0)))
    wg = jnp.concatenate([wg, bias_rows], axis=1)
    lru_w = (0.5 * lru_conv_w[0], 0.5 * lru_conv_b[0].reshape(1, D_RNN), wg, lru_lambda[0])

    def mixer0(x3, modl, bsz, ls, ret_f0, ret_b0, latent):
        tc = min(4, ls // RET_CHUNK)
        if latent:
            qkv, sbn = _inproj_call(x3, modl, norm_mix[0], w_in0, _rope_tables(ls), retention=True,
                                    sh_col=0, sc_col=1, bwd=(dec, ret_b0))
            mix = _mix0_call(qkv, dec, ret_f0, sbn, ws, bs, tc)[0]
            sff = sbf = None
        else:
            qkv = _inproj_call(x3, modl, norm_mix[0], w_in0, None, retention=True, sh_col=0, sc_col=1)
            mix, sff, sbf = _mix0_call(qkv.reshape(bsz, ls, AB_IN), dec, None, None, ws, bs, tc)
        return mix.reshape(x3.shape[0], x3.shape[1], AB_OUT), sff, sbf

    def lru_inproj(x3, modl):
        gx = _inproj_call(x3, modl, norm_mix[1], w_in1, None, retention=False, sh_col=0, sc_col=1)
        return gx.reshape(-1, 2 * D_RNN)

    bp, lp, _ = x_prompt.shape
    bl, ll, _ = x_sample.shape
    assert lp == LRU_SEG and ll % LRU_SEG == 0
    xc = x_prompt.reshape(1, bp * lp, d)
    xl = x_sample
    mod_c = [mod[l, 0:1][:, None, :] for l in range(DEPTH)]
    mod_l = [mod[l, 1:1 + n_lat][:, None, :] for l in range(DEPTH)]

    mix_c, sff, sbf = mixer0(xc, mod_c[0], bp, lp, None, None, False)
    xc = _outmlp_call(xc, mix_c, mod_c[0], norm_mlp[0], final_norm, w_out0, w1, w2, layer=0, final=False,
                      t=PLAIN_MLP_TILE)
    gx_c = lru_inproj(xc, mod_c[1])
    mix_l, _, _ = mixer0(xl, mod_l[0], bl, ll, state_ret_fwd[:, 0], state_ret_bwd[:, 0], True)

    steps_l = bl * (ll // TOK_TILE)
    nseg_c = (bp * lp // LRU_SEG) * LRU_BLOCKS // steps_l
    zero_lru = jnp.zeros((bp // nseg_c, nseg_c, D_RNN), f32)
    xl, pre_c, hfl, hbf = _outmlp_call(
        xl, mix_l, mod_l[0], norm_mlp[0], final_norm, w_out0, w1, w2, layer=0, final=False, t=TOK_TILE,
        lru_args=(gx_c,) + lru_w + (zero_lru, zero_lru, nseg_c, False))
    gx_l = lru_inproj(xl, mod_l[1])

    nseg_l = ll // LRU_SEG
    t_c = bp * lp // (bl * LRU_BLOCKS)
    y_c, pre_l, _, _ = _outmlp_call(
        xc, pre_c.reshape(1, bp * lp, D_RNN), mod_c[1], norm_mlp[1], final_norm, w_out1, w1, w2,
        layer=1, final=True, t=t_c,
        lru_args=(gx_l,) + lru_w + (state_lru_fwd[:, 0].reshape(bl, 1, D_RNN),
                                    state_lru_bwd[:, 0].reshape(bl, 1, D_RNN), nseg_l, True))
    y_l = _outmlp_call(xl, pre_l.reshape(bl, ll, D_RNN), mod_l[1], norm_mlp[1], final_norm, w_out1, w1, w2,
                       layer=1, final=True, t=PLAIN_MLP_TILE)
    return (y_c.reshape(bp, lp, d), y_l, sff[:, None], sbf[:, None],
            hfl.reshape(bp, 1, D_RNN), hbf.reshape(bp, 1, D_RNN))
```

```python
import functools

import jax
import jax.numpy as jnp
import numpy as np
from jax import lax
from jax.experimental import pallas as pl
from jax.experimental.pallas import tpu as pltpu

f32 = jnp.float32
bf16 = jnp.bfloat16

D_MODEL = 1024
DEPTH = 2
GRID_W = 64
EPS = 1e-6
RET_HEADS = 4
RET_DK = 128
RET_DV = 128
RET_CHUNK = 128
ROPE_BASE = 10000.0
GMLP_GROUPS = 4
GMLP_CH = 128
AB_IN = 3072
AB_OUT = 1024
D_RNN = D_MODEL
LRU_BLOCKS = 8
LRU_BS = 128
LRU_C = 8.0
D_FF = 4 * D_MODEL

LANES = 128
SUBLANES = 8
MIB = 1024 * 1024

TOK_TILE = 512
PLAIN_MLP_TILE = 1024
INPROJ_TILE = 1024
COND_ROWS = 8
MOD_TN = 1536
LRU_SEG = 256
LRU_PITCH = LRU_SEG + SUBLANES
FF_CHUNK = 1024
N_BIAS_TERMS = 3
LOG2_E = 1.4426950408889634
GELU_C1 = 0.7978845608028654
GELU_C3 = 0.044715 * GELU_C1
MIX_CHUNKS = 8
PROJ_COLS = 512
ROPE_ROT = RET_DK // 4
VMEM_MIX_MIB = 32
VMEM_PROJ_MIB = 48
VMEM_MLP_MIB = 58


def _cparams(sem, vmem_mib):
    return pltpu.CompilerParams(dimension_semantics=sem, vmem_limit_bytes=vmem_mib * MIB)


def _gelu(x):
    hx = 0.5 * x
    return hx + hx * jnp.tanh(x * (GELU_C1 + GELU_C3 * (x * x)))


def _silu(x):
    hx = 0.5 * x
    return hx + hx * jnp.tanh(hx)


def _rms(x, gain):
    ms = jnp.mean(x * x, axis=-1, keepdims=True)
    return x * lax.rsqrt(ms + EPS) * gain


def _mod_kernel(cond_ref, w_ref, b_ref, o_ref):
    c = cond_ref[...]
    s = c * jax.nn.sigmoid(c)
    o_ref[...] = jnp.dot(s.astype(bf16), w_ref[...].astype(bf16), preferred_element_type=f32) + b_ref[...]


def _mod_call(conds, mod_w, mod_b):
    depth, d, n = mod_w.shape
    return pl.pallas_call(
        _mod_kernel,
        grid=(depth, n // MOD_TN),
        in_specs=[
            pl.BlockSpec((COND_ROWS, d), lambda l, j: (0, 0)),
            pl.BlockSpec((None, d, MOD_TN), lambda l, j: (l, 0, j)),
            pl.BlockSpec((None, 1, MOD_TN), lambda l, j: (l, 0, j)),
        ],
        out_specs=pl.BlockSpec((None, COND_ROWS, MOD_TN), lambda l, j: (l, 0, j)),
        out_shape=jax.ShapeDtypeStruct((depth, COND_ROWS, n), f32),
        compiler_params=_cparams(("parallel", "parallel"), VMEM_MIX_MIB),
        name="mod",
    )(conds, mod_w, mod_b.reshape(depth, 1, n))


def _rope(t, cos, sin_signed, first_half):
    up = pltpu.roll(t, LANES - ROPE_ROT, axis=1)
    dn = pltpu.roll(t, ROPE_ROT, axis=1)
    return t * cos + jnp.where(first_half, up, dn) * sin_signed


def _inproj_kernel(*refs, retention, rope, bwd_states):
    if bwd_states:
        (x_ref, sh_ref, sc_ref, gain_ref, w_ref, cos_ref, sin_ref, dec_ref, s0b_ref,
         o_ref, sbn_ref, sb_scr) = refs

        @pl.when(pl.program_id(1) == 0)
        def _():
            sb_scr[...] = s0b_ref[...]
    elif rope:
        x_ref, sh_ref, sc_ref, gain_ref, w_ref, cos_ref, sin_ref, o_ref = refs
    else:
        x_ref, sh_ref, sc_ref, gain_ref, w_ref, o_ref = refs
    h = _rms(x_ref[...], gain_ref[...]) * (1.0 + sc_ref[...]) + sh_ref[...]
    hb = h.astype(bf16)
    n = w_ref.shape[1]
    if not retention:
        for j in range(0, n, PROJ_COLS):
            y = jnp.dot(hb, w_ref[:, j:j + PROJ_COLS], preferred_element_type=f32)
            o_ref[:, j:j + PROJ_COLS] = (_gelu(y) if j < n // 2 else y).astype(bf16)
        return
    qk = RET_HEADS * RET_DK
    if rope:
        cos = cos_ref[...]
        sin = sin_ref[...]
        lane = lax.broadcasted_iota(jnp.int32, cos.shape, 1)
        first_half = (lane % (2 * ROPE_ROT)) < ROPE_ROT
    k_heads = []
    for part in range(2):
        y = jnp.dot(hb, w_ref[:, part * qk:(part + 1) * qk], preferred_element_type=f32)
        for hh in range(RET_HEADS):
            t = y[:, hh * RET_DK:(hh + 1) * RET_DK]
            if part == 1:
                t = t * (RET_DK ** -0.5)
            if rope:
                t = _rope(t, cos, sin, first_half)
            c0 = part * qk + hh * RET_DK
            tb = t.astype(bf16)
            o_ref[:, c0:c0 + RET_DK] = tb
            if part == 1:
                k_heads.append(tb)
    acts = (None, _silu, _gelu, _gelu)
    assert qk == PROJ_COLS and n - 2 * qk == len(acts) * PROJ_COLS
    for j, act in zip(range(2 * qk, n, PROJ_COLS), acts):
        y = jnp.dot(hb, w_ref[:, j:j + PROJ_COLS], preferred_element_type=f32)
        yb = (y if act is None else act(y)).astype(bf16)
        o_ref[:, j:j + PROJ_COLS] = yb
        if bwd_states and j == 2 * qk:
            vb = yb
    if bwd_states:
        row = lax.broadcasted_iota(jnp.int32, (RET_CHUNK, LANES), 0).astype(f32)
        nchunk = x_ref.shape[0] // RET_CHUNK
        for hh in range(RET_HEADS):
            lg_b = jax.nn.log_sigmoid(dec_ref[1, hh])
            kd_b = jnp.exp(lg_b * row)
            gc_b = jnp.exp(lg_b * float(RET_CHUNK))
            cs = slice(hh * RET_DK, (hh + 1) * RET_DK)
            for cc in reversed(range(nchunk)):
                rs = slice(cc * RET_CHUNK, (cc + 1) * RET_CHUNK)
                s_next = sb_scr[hh]
                sbn_ref[cc, hh] = s_next.astype(bf16)
                kd = (k_heads[hh][rs].astype(f32) * kd_b).astype(bf16)
                kv = lax.dot_general(kd, vb[rs, cs], (((0,), (0,)), ((), ())), preferred_element_type=f32)
                sb_scr[hh] = gc_b * s_next + kv


def _inproj_call(x3, modl, gain, w, rope_tabs, *, retention, sh_col, sc_col, bwd=None):
    bg, lg, d = x3.shape
    n = w.shape[1]
    t = INPROJ_TILE
    ns = lg // t
    rope = rope_tabs is not None
    pos = (lambda i: ns - 1 - i) if bwd is not None else (lambda i: i)
    in_specs = [
        pl.BlockSpec((None, t, d), lambda b, i: (b, pos(i), 0)),
        pl.BlockSpec((None, 1, d), lambda b, i: (b, 0, sh_col)),
        pl.BlockSpec((None, 1, d), lambda b, i: (b, 0, sc_col)),
        pl.BlockSpec((1, d), lambda b, i: (0, 0)),
        pl.BlockSpec((d, n), lambda b, i: (0, 0), pipeline_mode=pl.Buffered(1)),
    ]
    args = [x3, modl, modl, gain.reshape(1, d), w]
    if rope:
        in_specs += [pl.BlockSpec((t, LANES), lambda b, i: (pos(i), 0))] * 2
        args += list(rope_tabs)
    out_specs = [pl.BlockSpec((None, t, n), lambda b, i: (b, pos(i), 0))]
    out_shape = [jax.ShapeDtypeStruct((bg, lg, n), bf16)]
    scratch = []
    if bwd is not None:
        st = (RET_HEADS, RET_DK, RET_DV)
        tc = t // RET_CHUNK
        in_specs += [pl.BlockSpec((2, RET_HEADS, 1, LANES), lambda b, i: (0, 0, 0, 0)),
                     pl.BlockSpec((None,) + st, lambda b, i: (b, 0, 0, 0))]
        args += list(bwd)
        out_specs.append(pl.BlockSpec((None, tc) + st, lambda b, i: (b, pos(i), 0, 0, 0)))
        out_shape.append(jax.ShapeDtypeStruct((bg, lg // RET_CHUNK) + st, bf16))
        scratch.append(pltpu.VMEM(st, f32))
    res = pl.pallas_call(
        functools.partial(_inproj_kernel, retention=retention, rope=rope, bwd_states=bwd is not None),
        grid=(bg, ns),
        in_specs=in_specs,
        out_specs=out_specs,
        out_shape=out_shape,
        scratch_shapes=scratch,
        compiler_params=_cparams(("parallel", "arbitrary" if bwd is not None else "parallel"), VMEM_PROJ_MIB),
        name="inproj_ret" if retention else "inproj_lru",
    )(*args)
    return res if bwd is not None else res[0]


def _row_iota():
    return lax.broadcasted_iota(jnp.int32, (RET_CHUNK, LANES), 0).astype(f32)


def _mix0_kernel(*refs, tc, nsq, whole_seq):
    if whole_seq:
        dec_ref, q_ref, k_ref, v_ref, g_ref, u_ref, vv_ref, ws_ref, bs_ref, o_ref, sff_ref, sbf_ref = refs
    else:
        (dec_ref, q_ref, k_ref, v_ref, g_ref, u_ref, vv_ref, s0f_ref, sbn_ref, ws_ref, bs_ref,
         o_ref, sf_scr) = refs

        @pl.when(pl.program_id(1) == 0)
        def _():
            sf_scr[...] = s0f_ref[...]

    row = _row_iota()
    col = lax.broadcasted_iota(jnp.int32, (RET_CHUNK, LANES), 1).astype(f32)
    c = float(RET_CHUNK)
    nt = (((1,), (1,)), ((), ()))
    tn = (((0,), (0,)), ((), ()))
    for hh in range(RET_HEADS):
        lg_f = jax.nn.log_sigmoid(dec_ref[0, hh])
        lg_b = jax.nn.log_sigmoid(dec_ref[1, hh])
        decay = jnp.where(row >= col,
                          jnp.exp(lg_f * jnp.maximum(row - col, 0.0)),
                          jnp.exp(lg_b * jnp.maximum(col - row - 1.0, 0.0)))
        qd_f = jnp.exp(lg_f * (row + 1.0))
        qd_b = jnp.exp(lg_b * (c - 1.0 - row))
        cs = slice(hh * RET_DK, (hh + 1) * RET_DK)
        kd_f = jnp.exp(lg_f * (c - 1.0 - row))
        kd_b = jnp.exp(lg_b * row)
        gc_f = jnp.exp(lg_f * c)
        gc_b = jnp.exp(lg_b * c)
        s_prev, s_next = [None] * tc, [None] * tc
        cps = tc // nsq
        for sq in range(nsq):
            chunks = range(sq * cps, (sq + 1) * cps)
            sf = None if whole_seq else sf_scr[hh]
            sb = None
            for cc in chunks:
                rs = slice(cc * RET_CHUNK, (cc + 1) * RET_CHUNK)
                kf = k_ref[rs, cs].astype(f32)
                kv = lax.dot_general((kf * kd_f).astype(bf16), v_ref[rs, cs], tn, preferred_element_type=f32)
                s_prev[cc] = sf
                sf = kv if sf is None else gc_f * sf + kv
            if whole_seq:
                for cc in reversed(chunks):
                    rs = slice(cc * RET_CHUNK, (cc + 1) * RET_CHUNK)
                    kf = k_ref[rs, cs].astype(f32)
                    kv = lax.dot_general((kf * kd_b).astype(bf16), v_ref[rs, cs], tn, preferred_element_type=f32)
                    s_next[cc] = sb
                    sb = kv if sb is None else gc_b * sb + kv
                sff_ref[sq, hh] = sf
                sbf_ref[sq, hh] = sb
            else:
                sf_scr[hh] = sf
        for cc in range(tc):
            rs = slice(cc * RET_CHUNK, (cc + 1) * RET_CHUNK)
            q = q_ref[rs, cs]
            qf = q.astype(f32)
            s = lax.dot_general(q, k_ref[rs, cs], nt, preferred_element_type=f32)
            o = jnp.dot((s * decay).astype(bf16), v_ref[rs, cs], preferred_element_type=f32)
            sp_c = None if s_prev[cc] is None else s_prev[cc].astype(bf16)
            if whole_seq:
                sn_c = None if s_next[cc] is None else s_next[cc].astype(bf16)
            else:
                sn_c = sbn_ref[cc, hh]
            if sp_c is not None:
                o = o + jnp.dot((qf * qd_f).astype(bf16), sp_c, preferred_element_type=f32)
            if sn_c is not None:
                o = o + jnp.dot((qf * qd_b).astype(bf16), sn_c, preferred_element_type=f32)
            mu = jnp.mean(o, axis=-1, keepdims=True)
            var = jnp.mean(jnp.square(o - mu), axis=-1, keepdims=True)
            on = (o - mu) * lax.rsqrt(var + EPS)
            o_ref[rs, cs] = (g_ref[rs, cs].astype(f32) * on).astype(bf16)
    off = RET_HEADS * RET_DV
    for gg in range(GMLP_GROUPS):
        cs = slice(gg * GMLP_CH, (gg + 1) * GMLP_CH)
        w = ws_ref[gg]
        bias = bs_ref[gg]
        for cc in range(tc):
            rs = slice(cc * RET_CHUNK, (cc + 1) * RET_CHUNK)
            gu = u_ref[rs, cs].astype(f32)
            gv = vv_ref[rs, cs].astype(f32)
            gv = gv * lax.rsqrt(jnp.mean(gv * gv, axis=-1, keepdims=True) + EPS)
            sp = jnp.dot(w, gv.astype(bf16), preferred_element_type=f32) + bias
            o_ref[rs, off + gg * GMLP_CH: off + (gg + 1) * GMLP_CH] = (gu * sp).astype(bf16)


def _mix0_call(qkv, dec, s0f, sbn, ws, bs, tc, nsq=1):
    bsz, ls, _ = qkv.shape
    t = tc * RET_CHUNK
    hw = RET_HEADS * RET_DK
    st = (RET_HEADS, RET_DK, RET_DV)
    col_spec = lambda j: pl.BlockSpec((None, t, hw), lambda b, i, j=j: (b, i, j))
    whole_seq = s0f is None
    in_specs = [
        pl.BlockSpec((2, RET_HEADS, 1, LANES), lambda b, i: (0, 0, 0, 0)),
        col_spec(0), col_spec(1), col_spec(2), col_spec(3), col_spec(4), col_spec(5),
    ]
    args = [dec, qkv, qkv, qkv, qkv, qkv, qkv]
    out_specs = [pl.BlockSpec((None, t, AB_OUT), lambda b, i: (b, i, 0))]
    out_shape = [jax.ShapeDtypeStruct((bsz, ls, AB_OUT), bf16)]
    if whole_seq:
        assert ls == t
        out_specs += [pl.BlockSpec((nsq,) + st, lambda b, i: (b, 0, 0, 0))] * 2
        out_shape += [jax.ShapeDtypeStruct((bsz * nsq,) + st, f32)] * 2
    else:
        in_specs += [pl.BlockSpec((None,) + st, lambda b, i: (b, 0, 0, 0)),
                     pl.BlockSpec((None, tc) + st, lambda b, i: (b, i, 0, 0, 0))]
        args += [s0f, sbn]
    in_specs += [
        pl.BlockSpec((GMLP_GROUPS, RET_CHUNK, RET_CHUNK), lambda b, i: (0, 0, 0)),
        pl.BlockSpec((GMLP_GROUPS, RET_CHUNK, 1), lambda b, i: (0, 0, 0)),
    ]
    args += [ws, bs]
    return pl.pallas_call(
        functools.partial(_mix0_kernel, tc=tc, nsq=nsq, whole_seq=whole_seq),
        grid=(bsz, ls // t),
        in_specs=in_specs,
        out_specs=out_specs,
        out_shape=out_shape,
        scratch_shapes=[] if whole_seq else [pltpu.VMEM(st, f32)],
        compiler_params=_cparams(("parallel", "parallel" if whole_seq else "arbitrary"), VMEM_MIX_MIB),
        name="mix0_seq" if whole_seq else "mix0",
    )(*args)


def _loop(n, body, carry):
    for i in range(n):
        carry = body(i, carry)
    return carry


def _lru_body(gate_ref, xin_ref, cw_ref, cb_ref, wg_ref, lam_ref, h0f_ref, h0b_ref,
              o_ref, hfl_ref, hbf_ref, xs, xt, af, bfs, ab, bbs, *, nseg, chained):
    seg, pitch = LRU_SEG, LRU_PITCH
    blk = seg

    def stage(s, carry):
        xs[pl.ds(s * pitch, seg), :] = xin_ref[pl.ds(s * seg, seg), :].astype(f32)
        return carry

    _loop(nseg, stage, 0)

    def tile(j):
        return pl.ds(j * nseg, nseg)

    def gather(j, carry):
        xt[tile(j + 2), :] = xs[pl.ds(j, nseg, stride=pitch), :]
        return carry

    _loop(seg, gather, 0)
    if chained:
        srow = lax.broadcasted_iota(jnp.int32, (nseg, LANES), 0)
        xt[tile(0), :] = jnp.where(srow == 0, 0.0, pltpu.roll(xt[tile(seg), :], 1, axis=0))
        xt[tile(1), :] = jnp.where(srow == 0, 0.0, pltpu.roll(xt[tile(seg + 1), :], 1, axis=0))
        xt[tile(seg + 2), :] = jnp.where(srow == nseg - 1, 0.0, pltpu.roll(xt[tile(2), :], nseg - 1, axis=0))
    else:
        zt = jnp.zeros((nseg, LANES), f32)
        xt[tile(0), :] = zt
        xt[tile(1), :] = zt
        xt[tile(seg + 2), :] = zt

    cw = cw_ref[...]
    cb = cb_ref[...]
    sp4 = (0.5 * LRU_C) * jax.nn.softplus(-lam_ref[...])
    sp4e = sp4 * (-LOG2_E)
    ones = (lax.broadcasted_iota(jnp.int32, (blk, LANES), 1) < N_BIAS_TERMS).astype(bf16)

    def gates(jb, carry):
        p0 = jb * blk
        xm2 = xt[pl.ds(p0, blk), :]
        xm1 = xt[pl.ds(p0 + nseg, blk), :]
        x0 = xt[pl.ds(p0 + 2 * nseg, blk), :]
        xp1 = xt[pl.ds(p0 + 3 * nseg, blk), :]
        xch = cb + xm2 * cw[0:1] + xm1 * cw[1:2] + x0 * cw[2:3] + xp1 * cw[3:4]
        lhs = jnp.concatenate([xch.astype(bf16), ones], axis=1)
        gt = jnp.dot(lhs, wg_ref[...], preferred_element_type=f32)
        for d, (a_s, b_s) in enumerate(((af, bfs), (ab, bbs))):
            tr = jnp.tanh(gt[:, (2 * d) * LANES:(2 * d + 1) * LANES])
            ti = jnp.tanh(gt[:, (2 * d + 1) * LANES:(2 * d + 2) * LANES])
            r2 = tr + 1.0
            nla = r2 * sp4[d:d + 1]
            a = jnp.exp2(r2 * sp4e[d:d + 1])
            m2 = jnp.tanh(nla) * (1.0 + a * a)
            mult = m2 * lax.rsqrt(jnp.maximum(m2, 1e-30))
            a_s[pl.ds(p0, blk), :] = a
            b_s[pl.ds(p0, blk), :] = mult * ((ti + 1.0) * xch)
        return carry

    _loop(nseg, gates, 0)

    if chained:
        def ends(j, carry):
            hf, pf, hb, pb = carry
            a = af[tile(j), :]
            hf = a * hf + bfs[tile(j), :]
            pf = a * pf
            a2 = ab[tile(seg - 1 - j), :]
            hb = a2 * hb + bbs[tile(seg - 1 - j), :]
            pb = a2 * pb
            return hf, pf, hb, pb

        zero = jnp.zeros((nseg, LANES), f32)
        one = jnp.ones((nseg, LANES), f32)
        hf_e, pf_e, hb_e, pb_e = _loop(seg, ends, (zero, one, zero, one))
        cf = [h0f_ref[0:1, :]]
        for s in range(1, nseg):
            cf.append(hf_e[s - 1:s] + pf_e[s - 1:s] * cf[-1])
        cbk = [h0b_ref[0:1, :]]
        for s in range(nseg - 2, -1, -1):
            cbk.append(hb_e[s + 1:s + 2] + pb_e[s + 1:s + 2] * cbk[-1])
        hf0 = jnp.concatenate(cf, axis=0)
        hb0 = jnp.concatenate(cbk[::-1], axis=0)
    else:
        hf0 = h0f_ref[...]
        hb0 = h0b_ref[...]

    def scan_b(k, hb):
        j = seg - 1 - k
        hb = ab[tile(j), :] * hb + bbs[tile(j), :]
        xt[tile(j), :] = hb
        return hb

    hb_l = _loop(seg, scan_b, hb0)

    def scan_f(j, hf):
        hf = af[tile(j), :] * hf + bfs[tile(j), :]
        xs[pl.ds(j, nseg, stride=pitch), :] = hf + xt[tile(j), :]
        return hf

    hf_l = _loop(seg, scan_f, hf0)
    if chained:
        hfl_ref[...] = hf_l[nseg - 1:nseg]
        hbf_ref[...] = hb_l[0:1]
    else:
        hfl_ref[...] = hf_l
        hbf_ref[...] = hb_l

    def combine(s, carry):
        g = gate_ref[pl.ds(s * seg, seg), :].astype(f32)
        o_ref[pl.ds(s * seg, seg), :] = (g * xs[pl.ds(s * pitch, seg), :]).astype(bf16)
        return carry

    _loop(nseg, combine, 0)


def _lru_specs(ntok, nseg, gc, pair):
    rows = LRU_SEG * nseg
    nt = ntok // rows
    nb = LRU_BLOCKS
    in_specs = [
        pl.BlockSpec((rows, LANES), lambda *g: pair(*g)),
        pl.BlockSpec((rows, LANES), lambda *g: (pair(*g)[0], nb + pair(*g)[1])),
        pl.BlockSpec((4, LANES), lambda *g: (0, pair(*g)[1])),
        pl.BlockSpec((1, LANES), lambda *g: (0, pair(*g)[1])),
        pl.BlockSpec((None, 2 * LANES, 4 * LANES), lambda *g: (pair(*g)[1], 0, 0)),
        pl.BlockSpec((2, LANES), lambda *g: (0, pair(*g)[1])),
        pl.BlockSpec((None, gc, LANES), lambda *g: (pair(*g)[0], 0, pair(*g)[1])),
        pl.BlockSpec((None, gc, LANES), lambda *g: (pair(*g)[0], 0, pair(*g)[1])),
    ]
    out_specs = [
        pl.BlockSpec((rows, LANES), lambda *g: pair(*g)),
        pl.BlockSpec((None, gc, LANES), lambda *g: (pair(*g)[0], 0, pair(*g)[1])),
        pl.BlockSpec((None, gc, LANES), lambda *g: (pair(*g)[0], 0, pair(*g)[1])),
    ]
    out_shape = [
        jax.ShapeDtypeStruct((ntok, D_RNN), bf16),
        jax.ShapeDtypeStruct((nt, gc, D_RNN), f32),
        jax.ShapeDtypeStruct((nt, gc, D_RNN), f32),
    ]
    scr = pltpu.VMEM((rows, LANES), f32)
    scratch = [pltpu.VMEM((nseg * LRU_PITCH, LANES), f32), pltpu.VMEM((rows + 3 * nseg, LANES), f32),
               scr, scr, scr, scr]
    return in_specs, out_specs, out_shape, scratch


def _outmlp_body(x_ref, mix_ref, g1_ref, sh2_ref, sc2_ref, g2_ref, gain_ref, fgain_ref,
                 wo_ref, w1_ref, w2_ref, *, final):
    mix = jnp.dot(mix_ref[...], wo_ref[...], preferred_element_type=f32)
    x1 = x_ref[...] + g1_ref[...] * mix
    h = (_rms(x1, gain_ref[...]) * (1.0 + sc2_ref[...]) + sh2_ref[...]).astype(bf16)
    acc = None
    for j in range(0, D_FF, FF_CHUNK):
        hid = jnp.dot(h, w1_ref[:, j:j + FF_CHUNK], preferred_element_type=f32)
        hid = jnp.square(jnp.maximum(hid, 0.0)).astype(bf16)
        part = jnp.dot(hid, w2_ref[j:j + FF_CHUNK, :], preferred_element_type=f32)
        acc = part if acc is None else acc + part
    x2 = x1 + g2_ref[...] * acc
    if final:
        x2 = _rms(x2, fgain_ref[...])
    return x2


N_MLP_IN = 11
N_LRU_IN = 8


def _outmlp_kernel(*refs, final, lru):
    n_in = N_MLP_IN + (N_LRU_IN if lru else 0)
    if lru is not None:
        _lru_body(*refs[N_MLP_IN:n_in], *refs[n_in + 1:], nseg=lru[0], chained=lru[1])
    refs[n_in][...] = _outmlp_body(*refs[:N_MLP_IN], final=final)


def _outmlp_call(x3, mix3, modl, gain, fgain, wo, w1, w2, *, layer, final, t, lru_args=None):
    bg, lg, d = x3.shape
    steps = lg // t
    dm = mix3.shape[-1]
    mod_spec = lambda k: pl.BlockSpec((None, 1, d), lambda b, i, k=k: (b, 0, k))

    def resident(shape, lead=None):
        if lead is None:
            return pl.BlockSpec(shape, lambda b, i: (0, 0), pipeline_mode=pl.Buffered(1))
        return pl.BlockSpec((None,) + shape, lambda b, i: (lead, 0, 0), pipeline_mode=pl.Buffered(1))

    in_specs = [
        pl.BlockSpec((None, t, d), lambda b, i: (b, i, 0)),
        pl.BlockSpec((None, t, dm), lambda b, i: (b, i, 0)),
        mod_spec(2), mod_spec(3), mod_spec(4), mod_spec(5),
        pl.BlockSpec((1, d), lambda b, i: (0, 0)),
        pl.BlockSpec((1, d), lambda b, i: (0, 0)),
        resident((dm, d)), resident((d, D_FF), layer), resident((D_FF, d), layer),
    ]
    args = [x3, mix3, modl, modl, modl, modl, gain.reshape(1, d), fgain.reshape(1, d), wo, w1, w2]
    out_specs = [pl.BlockSpec((None, t, d), lambda b, i: (b, i, 0))]
    out_shape = [jax.ShapeDtypeStruct((bg, lg, d), f32)]
    scratch = []
    lru = None
    if lru_args is not None:
        gx, cw, cb, wg, lam, h0f, h0b, nseg, chained = lru_args
        ntok = gx.shape[0]
        assert (ntok // (LRU_SEG * nseg)) * LRU_BLOCKS == bg * steps, "one RG-LRU (tile, block) pair per MLP step"

        def pair(b, i):
            p = b * steps + i
            return p // LRU_BLOCKS, p % LRU_BLOCKS

        li, lo, ls, scratch = _lru_specs(ntok, nseg, h0f.shape[1], pair)
        in_specs += li
        out_specs += lo
        out_shape += ls
        args += [gx, gx, cw, cb, wg, lam, h0f, h0b]
        lru = (nseg, chained)
    res = pl.pallas_call(
        functools.partial(_outmlp_kernel, final=final, lru=lru),
        grid=(bg, steps),
        in_specs=in_specs,
        out_specs=out_specs,
        out_shape=out_shape,
        scratch_shapes=scratch,
        compiler_params=_cparams(("parallel", "parallel"), VMEM_MLP_MIB),
        name=("outmlp_final" if final else "outmlp") + ("_rglru" if lru else ""),
    )(*args)
    return res if lru else res[0]


@functools.lru_cache(maxsize=None)
def _rope_tables(length):
    pos = np.arange(length)
    rowp = (pos // GRID_W).astype(np.float64)
    colp = (pos % GRID_W).astype(np.float64)
    nf = RET_DK // 4
    inv_freq = ROPE_BASE ** (-np.arange(nf, dtype=np.float64) / nf)
    ar = rowp[:, None] * inv_freq[None, :]
    ac = colp[:, None] * inv_freq[None, :]
    cos = np.concatenate([np.cos(ar), np.cos(ar), np.cos(ac), np.cos(ac)], axis=-1).astype(np.float32)
    sin = np.concatenate([-np.sin(ar), np.sin(ar), -np.sin(ac), np.sin(ac)], axis=-1).astype(np.float32)
    return cos, sin


def kernel(x_prompt, x_sample, c, c_ctx, state_ret_fwd, state_ret_bwd, state_lru_fwd, state_lru_bwd, mod_w, mod_b, norm_mix, norm_mlp, mlp_w1, mlp_w2, ab_w_in, ab_w_out, ret_decay_fwd, ret_decay_bwd, gmlp_ws, gmlp_bs, lru_w_in, lru_conv_w, lru_conv_b, lru_gate_a_w, lru_gate_a_b, lru_gate_x_w, lru_gate_x_b, lru_lambda, lru_w_out, final_norm):
    d = D_MODEL
    n_lat = c.shape[0]
    conds = jnp.concatenate([c_ctx[None, :], c, jnp.zeros((COND_ROWS - 1 - n_lat, d), f32)], axis=0)
    mod = _mod_call(conds, mod_w, mod_b)

    w_in0 = ab_w_in[0].astype(bf16)
    w_out0 = ab_w_out[0].astype(bf16)
    w1 = mlp_w1.astype(bf16)
    w2 = mlp_w2.astype(bf16)
    w_in1 = lru_w_in[0].astype(bf16)
    w_out1 = lru_w_out[0].astype(bf16)
    ws = gmlp_ws[0].astype(bf16)
    bs = gmlp_bs[0][:, :, None]
    dec = jnp.broadcast_to(jnp.stack([ret_decay_fwd[0], ret_decay_bwd[0]])[:, :, None, None],
                           (2, RET_HEADS, 1, LANES))
    wg = jnp.concatenate([lru_gate_a_w[0, 0], lru_gate_x_w[0, 0], lru_gate_a_w[0, 1], lru_gate_x_w[0, 1]],
                         axis=-1).astype(bf16)
    bgate = 0.5 * jnp.stack([lru_gate_a_b[0, 0], lru_gate_x_b[0, 0], lru_gate_a_b[0, 1], lru_gate_x_b[0, 1]], axis=0)
    bgate = bgate.reshape(4, LRU_BLOCKS, LRU_BS).transpose(1, 0, 2).reshape(LRU_BLOCKS, 1, 4 * LRU_BS)
    terms, rem = [], bgate
    for _ in range(N_BIAS_TERMS):
        terms.append(rem.astype(bf16))
        rem = rem - terms[-1].astype(f32)
    bias_rows = jnp.pad(jnp.concatenate(terms, axis=1), ((0, 0), (0, LANES - N_BIAS_TERMS), (0, 0)))
    wg = jnp.concatenate([wg, bias_rows], axis=1)
    lru_w = (0.5 * lru_conv_w[0], 0.5 * lru_conv_b[0].reshape(1, D_RNN), wg, lru_lambda[0])

    def mixer0(x3, modl, bsz, ls, ret_f0, ret_b0, latent):
        tc = min(MIX_CHUNKS, ls // RET_CHUNK)
        if latent:
            qkv, sbn = _inproj_call(x3, modl, norm_mix[0], w_in0, _rope_tables(ls), retention=True,
                                    sh_col=0, sc_col=1, bwd=(dec, ret_b0))
            mix = _mix0_call(qkv, dec, ret_f0, sbn, ws, bs, tc)[0]
            sff = sbf = None
        else:
            qkv = _inproj_call(x3, modl, norm_mix[0], w_in0, None, retention=True, sh_col=0, sc_col=1)
            nsq = MIX_CHUNKS // tc
            mix, sff, sbf = _mix0_call(qkv.reshape(bsz // nsq, nsq * ls, AB_IN), dec, None, None, ws, bs,
                                       nsq * tc, nsq)
        return mix.reshape(x3.shape[0], x3.shape[1], AB_OUT), sff, sbf

    def lru_inproj(x3, modl):
        gx = _inproj_call(x3, modl, norm_mix[1], w_in1, None, retention=False, sh_col=0, sc_col=1)
        return gx.reshape(-1, 2 * D_RNN)

    bp, lp, _ = x_prompt.shape
    bl, ll, _ = x_sample.shape
    assert lp == LRU_SEG and ll % LRU_SEG == 0
    xc = x_prompt.reshape(1, bp * lp, d)
    xl = x_sample
    mod_c = [mod[l, 0:1][:, None, :] for l in range(DEPTH)]
    mod_l = [mod[l, 1:1 + n_lat][:, None, :] for l in range(DEPTH)]

    mix_c, sff, sbf = mixer0(xc, mod_c[0], bp, lp, None, None, False)
    xc = _outmlp_call(xc, mix_c, mod_c[0], norm_mlp[0], final_norm, w_out0, w1, w2, layer=0, final=False,
                      t=PLAIN_MLP_TILE)
    gx_c = lru_inproj(xc, mod_c[1])
    mix_l, _, _ = mixer0(xl, mod_l[0], bl, ll, state_ret_fwd[:, 0], state_ret_bwd[:, 0], True)

    steps_l = bl * (ll // TOK_TILE)
    nseg_c = (bp * lp // LRU_SEG) * LRU_BLOCKS // steps_l
    zero_lru = jnp.zeros((bp // nseg_c, nseg_c, D_RNN), f32)
    xl, pre_c, hfl, hbf = _outmlp_call(
        xl, mix_l, mod_l[0], norm_mlp[0], final_norm, w_out0, w1, w2, layer=0, final=False, t=TOK_TILE,
        lru_args=(gx_c,) + lru_w + (zero_lru, zero_lru, nseg_c, False))
    gx_l = lru_inproj(xl, mod_l[1])

    nseg_l = ll // LRU_SEG
    t_c = bp * lp // (bl * LRU_BLOCKS)
    y_c, pre_l, _, _ = _outmlp_call(
        xc, pre_c.reshape(1, bp * lp, D_RNN), mod_c[1], norm_mlp[1], final_norm, w_out1, w1, w2,
        layer=1, final=True, t=t_c,
        lru_args=(gx_l,) + lru_w + (state_lru_fwd[:, 0].reshape(bl, 1, D_RNN),
                                    state_lru_bwd[:, 0].reshape(bl, 1, D_RNN), nseg_l, True))
    y_l = _outmlp_call(xl, pre_l.reshape(bl, ll, D_RNN), mod_l[1], norm_mlp[1], final_norm, w_out1, w1, w2,
                       layer=1, final=True, t=PLAIN_MLP_TILE)
    return (y_c.reshape(bp, lp, d), y_l, sff[:, None], sbf[:, None],
            hfl.reshape(bp, 1, D_RNN), hbf.reshape(bp, 1, D_RNN))
```

```python
import functools

import jax
import jax.numpy as jnp
import numpy as np
from jax import lax
from jax.experimental import pallas as pl
from jax.experimental.pallas import tpu as pltpu

f32 = jnp.float32
bf16 = jnp.bfloat16

D_MODEL = 1024
DEPTH = 2
GRID_W = 64
EPS = 1e-6
RET_HEADS = 4
RET_DK = 128
RET_DV = 128
RET_CHUNK = 128
ROPE_BASE = 10000.0
GMLP_GROUPS = 4
GMLP_CH = 128
AB_IN = 3072
AB_OUT = 1024
D_RNN = D_MODEL
LRU_BLOCKS = 8
LRU_BS = 128
LRU_C = 8.0
D_FF = 4 * D_MODEL

LANES = 128
SUBLANES = 8
MIB = 1024 * 1024

TOK_TILE = 512
PLAIN_MLP_TILE = 1024
INPROJ_TILE = 1024
COND_ROWS = 8
MOD_TN = 3072
LRU_SEG = 256
LRU_PITCH = LRU_SEG + SUBLANES
FF_CHUNK = 1024
N_BIAS_TERMS = 3
LOG2_E = 1.4426950408889634
GELU_C1 = 0.7978845608028654
GELU_C3 = 0.044715 * GELU_C1
MIX_CHUNKS = 8
PROJ_COLS = 512
ROPE_ROT = RET_DK // 4
VMEM_MOD_MIB = 40
VMEM_PROJ_MIB = 48
VMEM_FULL_MIB = 58


def _cparams(sem, vmem_mib):
    return pltpu.CompilerParams(dimension_semantics=sem, vmem_limit_bytes=vmem_mib * MIB)


def _gelu(x):
    hx = 0.5 * x
    return hx + hx * jnp.tanh(x * (GELU_C1 + GELU_C3 * (x * x)))


def _silu(x):
    hx = 0.5 * x
    return hx + hx * jnp.tanh(hx)


def _rms(x, gain):
    ms = jnp.mean(x * x, axis=-1, keepdims=True)
    return x * lax.rsqrt(ms + EPS) * gain


def _mod_kernel(cond_ref, w_ref, b_ref, o_ref):
    c = cond_ref[...]
    s = c * jax.nn.sigmoid(c)
    o_ref[...] = jnp.dot(s.astype(bf16), w_ref[...].astype(bf16), preferred_element_type=f32) + b_ref[...]


def _mod_call(conds, mod_w, mod_b):
    depth, d, n = mod_w.shape
    return pl.pallas_call(
        _mod_kernel,
        grid=(depth, n // MOD_TN),
        in_specs=[
            pl.BlockSpec((COND_ROWS, d), lambda l, j: (0, 0)),
            pl.BlockSpec((None, d, MOD_TN), lambda l, j: (l, 0, j)),
            pl.BlockSpec((None, 1, MOD_TN), lambda l, j: (l, 0, j)),
        ],
        out_specs=pl.BlockSpec((None, COND_ROWS, MOD_TN), lambda l, j: (l, 0, j)),
        out_shape=jax.ShapeDtypeStruct((depth, COND_ROWS, n), f32),
        compiler_params=_cparams(("parallel", "parallel"), VMEM_MOD_MIB),
        name="mod",
    )(conds, mod_w, mod_b.reshape(depth, 1, n))


def _rope(t, cos, sin_signed, first_half):
    up = pltpu.roll(t, LANES - ROPE_ROT, axis=1)
    dn = pltpu.roll(t, ROPE_ROT, axis=1)
    return t * cos + jnp.where(first_half, up, dn) * sin_signed


def _inproj_kernel(*refs, retention, rope, bwd_states):
    if bwd_states:
        (x_ref, sh_ref, sc_ref, gain_ref, w_ref, cos_ref, sin_ref, dec_ref, s0b_ref,
         o_ref, sbn_ref, sb_scr) = refs

        @pl.when(pl.program_id(1) == 0)
        def _():
            sb_scr[...] = s0b_ref[...]
    elif rope:
        x_ref, sh_ref, sc_ref, gain_ref, w_ref, cos_ref, sin_ref, o_ref = refs
    else:
        x_ref, sh_ref, sc_ref, gain_ref, w_ref, o_ref = refs
    h = _rms(x_ref[...], gain_ref[...]) * (1.0 + sc_ref[...]) + sh_ref[...]
    hb = h.astype(bf16)
    n = w_ref.shape[1]
    if not retention:
        for j in range(0, n, PROJ_COLS):
            y = jnp.dot(hb, w_ref[:, j:j + PROJ_COLS], preferred_element_type=f32)
            o_ref[:, j:j + PROJ_COLS] = (_gelu(y) if j < n // 2 else y).astype(bf16)
        return
    qk = RET_HEADS * RET_DK
    if rope:
        cos = cos_ref[...]
        sin = sin_ref[...]
        lane = lax.broadcasted_iota(jnp.int32, cos.shape, 1)
        first_half = (lane % (2 * ROPE_ROT)) < ROPE_ROT
    k_heads = []
    for part in range(2):
        y = jnp.dot(hb, w_ref[:, part * qk:(part + 1) * qk], preferred_element_type=f32)
        for hh in range(RET_HEADS):
            t = y[:, hh * RET_DK:(hh + 1) * RET_DK]
            if part == 1:
                t = t * (RET_DK ** -0.5)
            if rope:
                t = _rope(t, cos, sin, first_half)
            c0 = part * qk + hh * RET_DK
            tb = t.astype(bf16)
            o_ref[:, c0:c0 + RET_DK] = tb
            if part == 1:
                k_heads.append(tb)
    acts = (None, _silu, _gelu, _gelu)
    assert qk == PROJ_COLS and n - 2 * qk == len(acts) * PROJ_COLS
    for j, act in zip(range(2 * qk, n, PROJ_COLS), acts):
        y = jnp.dot(hb, w_ref[:, j:j + PROJ_COLS], preferred_element_type=f32)
        yb = (y if act is None else act(y)).astype(bf16)
        o_ref[:, j:j + PROJ_COLS] = yb
        if bwd_states and j == 2 * qk:
            vb = yb
    if bwd_states:
        row = lax.broadcasted_iota(jnp.int32, (RET_CHUNK, LANES), 0).astype(f32)
        nchunk = x_ref.shape[0] // RET_CHUNK
        for hh in range(RET_HEADS):
            lg_b = jax.nn.log_sigmoid(dec_ref[1, hh])
            kd_b = jnp.exp(lg_b * row)
            gc_b = jnp.exp(lg_b * float(RET_CHUNK))
            cs = slice(hh * RET_DK, (hh + 1) * RET_DK)
            for cc in reversed(range(nchunk)):
                rs = slice(cc * RET_CHUNK, (cc + 1) * RET_CHUNK)
                s_next = sb_scr[hh]
                sbn_ref[cc, hh] = s_next.astype(bf16)
                kd = (k_heads[hh][rs].astype(f32) * kd_b).astype(bf16)
                kv = lax.dot_general(kd, vb[rs, cs], (((0,), (0,)), ((), ())), preferred_element_type=f32)
                sb_scr[hh] = gc_b * s_next + kv


def _inproj_call(x3, modl, gain, w, rope_tabs, *, retention, sh_col, sc_col, bwd=None):
    bg, lg, d = x3.shape
    n = w.shape[1]
    t = INPROJ_TILE
    ns = lg // t
    rope = rope_tabs is not None
    pos = (lambda i: ns - 1 - i) if bwd is not None else (lambda i: i)
    in_specs = [
        pl.BlockSpec((None, t, d), lambda b, i: (b, pos(i), 0)),
        pl.BlockSpec((None, 1, d), lambda b, i: (b, 0, sh_col)),
        pl.BlockSpec((None, 1, d), lambda b, i: (b, 0, sc_col)),
        pl.BlockSpec((1, d), lambda b, i: (0, 0)),
        pl.BlockSpec((d, n), lambda b, i: (0, 0), pipeline_mode=pl.Buffered(1)),
    ]
    args = [x3, modl, modl, gain.reshape(1, d), w]
    if rope:
        in_specs += [pl.BlockSpec((t, LANES), lambda b, i: (pos(i), 0))] * 2
        args += list(rope_tabs)
    out_specs = [pl.BlockSpec((None, t, n), lambda b, i: (b, pos(i), 0))]
    out_shape = [jax.ShapeDtypeStruct((bg, lg, n), bf16)]
    scratch = []
    if bwd is not None:
        st = (RET_HEADS, RET_DK, RET_DV)
        tc = t // RET_CHUNK
        in_specs += [pl.BlockSpec((2, RET_HEADS, 1, LANES), lambda b, i: (0, 0, 0, 0)),
                     pl.BlockSpec((None,) + st, lambda b, i: (b, 0, 0, 0))]
        args += list(bwd)
        out_specs.append(pl.BlockSpec((None, tc) + st, lambda b, i: (b, pos(i), 0, 0, 0)))
        out_shape.append(jax.ShapeDtypeStruct((bg, lg // RET_CHUNK) + st, bf16))
        scratch.append(pltpu.VMEM(st, f32))
    res = pl.pallas_call(
        functools.partial(_inproj_kernel, retention=retention, rope=rope, bwd_states=bwd is not None),
        grid=(bg, ns),
        in_specs=in_specs,
        out_specs=out_specs,
        out_shape=out_shape,
        scratch_shapes=scratch,
        compiler_params=_cparams(("parallel", "arbitrary" if bwd is not None else "parallel"), VMEM_PROJ_MIB),
        name="inproj_ret" if retention else "inproj_lru",
    )(*args)
    return res if bwd is not None else res[0]


def _row_iota():
    return lax.broadcasted_iota(jnp.int32, (RET_CHUNK, LANES), 0).astype(f32)


def _mix0_kernel(*refs, tc, nsq, whole_seq):
    if whole_seq:
        dec_ref, q_ref, k_ref, v_ref, g_ref, u_ref, vv_ref, ws_ref, bs_ref, o_ref, sff_ref, sbf_ref = refs
    else:
        (dec_ref, q_ref, k_ref, v_ref, g_ref, u_ref, vv_ref, s0f_ref, sbn_ref, ws_ref, bs_ref,
         o_ref, sf_scr) = refs

        @pl.when(pl.program_id(1) == 0)
        def _():
            sf_scr[...] = s0f_ref[...]

    row = _row_iota()
    col = lax.broadcasted_iota(jnp.int32, (RET_CHUNK, LANES), 1).astype(f32)
    c = float(RET_CHUNK)
    nt = (((1,), (1,)), ((), ()))
    tn = (((0,), (0,)), ((), ()))
    for hh in range(RET_HEADS):
        lg_f = jax.nn.log_sigmoid(dec_ref[0, hh])
        lg_b = jax.nn.log_sigmoid(dec_ref[1, hh])
        decay = jnp.where(row >= col,
                          jnp.exp(lg_f * jnp.maximum(row - col, 0.0)),
                          jnp.exp(lg_b * jnp.maximum(col - row - 1.0, 0.0)))
        qd_f = jnp.exp(lg_f * (row + 1.0))
        qd_b = jnp.exp(lg_b * (c - 1.0 - row))
        cs = slice(hh * RET_DK, (hh + 1) * RET_DK)
        kd_f = jnp.exp(lg_f * (c - 1.0 - row))
        kd_b = jnp.exp(lg_b * row)
        gc_f = jnp.exp(lg_f * c)
        gc_b = jnp.exp(lg_b * c)
        s_prev, s_next = [None] * tc, [None] * tc
        cps = tc // nsq
        for sq in range(nsq):
            chunks = range(sq * cps, (sq + 1) * cps)
            sf = None if whole_seq else sf_scr[hh]
            sb = None
            for cc in chunks:
                rs = slice(cc * RET_CHUNK, (cc + 1) * RET_CHUNK)
                kf = k_ref[rs, cs].astype(f32)
                kv = lax.dot_general((kf * kd_f).astype(bf16), v_ref[rs, cs], tn, preferred_element_type=f32)
                s_prev[cc] = sf
                sf = kv if sf is None else gc_f * sf + kv
            if whole_seq:
                for cc in reversed(chunks):
                    rs = slice(cc * RET_CHUNK, (cc + 1) * RET_CHUNK)
                    kf = k_ref[rs, cs].astype(f32)
                    kv = lax.dot_general((kf * kd_b).astype(bf16), v_ref[rs, cs], tn, preferred_element_type=f32)
                    s_next[cc] = sb
                    sb = kv if sb is None else gc_b * sb + kv
                sff_ref[sq, hh] = sf
                sbf_ref[sq, hh] = sb
            else:
                sf_scr[hh] = sf
        for cc in range(tc):
            rs = slice(cc * RET_CHUNK, (cc + 1) * RET_CHUNK)
            q = q_ref[rs, cs]
            qf = q.astype(f32)
            s = lax.dot_general(q, k_ref[rs, cs], nt, preferred_element_type=f32)
            o = jnp.dot((s * decay).astype(bf16), v_ref[rs, cs], preferred_element_type=f32)
            sp_c = None if s_prev[cc] is None else s_prev[cc].astype(bf16)
            if whole_seq:
                sn_c = None if s_next[cc] is None else s_next[cc].astype(bf16)
            else:
                sn_c = sbn_ref[cc, hh]
            if sp_c is not None:
                o = o + jnp.dot((qf * qd_f).astype(bf16), sp_c, preferred_element_type=f32)
            if sn_c is not None:
                o = o + jnp.dot((qf * qd_b).astype(bf16), sn_c, preferred_element_type=f32)
            mu = jnp.mean(o, axis=-1, keepdims=True)
            var = jnp.mean(jnp.square(o - mu), axis=-1, keepdims=True)
            on = (o - mu) * lax.rsqrt(var + EPS)
            o_ref[rs, cs] = (g_ref[rs, cs].astype(f32) * on).astype(bf16)
    off = RET_HEADS * RET_DV
    for gg in range(GMLP_GROUPS):
        cs = slice(gg * GMLP_CH, (gg + 1) * GMLP_CH)
        w = ws_ref[gg]
        bias = bs_ref[gg]
        for cc in range(tc):
            rs = slice(cc * RET_CHUNK, (cc + 1) * RET_CHUNK)
            gu = u_ref[rs, cs].astype(f32)
            gv = vv_ref[rs, cs].astype(f32)
            gv = gv * lax.rsqrt(jnp.mean(gv * gv, axis=-1, keepdims=True) + EPS)
            sp = jnp.dot(w, gv.astype(bf16), preferred_element_type=f32) + bias
            o_ref[rs, off + gg * GMLP_CH: off + (gg + 1) * GMLP_CH] = (gu * sp).astype(bf16)


def _mix0_call(qkv, dec, s0f, sbn, ws, bs, tc, nsq=1):
    bsz, ls, _ = qkv.shape
    t = tc * RET_CHUNK
    hw = RET_HEADS * RET_DK
    st = (RET_HEADS, RET_DK, RET_DV)
    col_spec = lambda j: pl.BlockSpec((None, t, hw), lambda b, i, j=j: (b, i, j))
    whole_seq = s0f is None
    in_specs = [
        pl.BlockSpec((2, RET_HEADS, 1, LANES), lambda b, i: (0, 0, 0, 0)),
        col_spec(0), col_spec(1), col_spec(2), col_spec(3), col_spec(4), col_spec(5),
    ]
    args = [dec, qkv, qkv, qkv, qkv, qkv, qkv]
    out_specs = [pl.BlockSpec((None, t, AB_OUT), lambda b, i: (b, i, 0))]
    out_shape = [jax.ShapeDtypeStruct((bsz, ls, AB_OUT), bf16)]
    if whole_seq:
        assert ls == t
        out_specs += [pl.BlockSpec((nsq,) + st, lambda b, i: (b, 0, 0, 0))] * 2
        out_shape += [jax.ShapeDtypeStruct((bsz * nsq,) + st, f32)] * 2
    else:
        in_specs += [pl.BlockSpec((None,) + st, lambda b, i: (b, 0, 0, 0)),
                     pl.BlockSpec((None, tc) + st, lambda b, i: (b, i, 0, 0, 0))]
        args += [s0f, sbn]
    in_specs += [
        pl.BlockSpec((GMLP_GROUPS, RET_CHUNK, RET_CHUNK), lambda b, i: (0, 0, 0)),
        pl.BlockSpec((GMLP_GROUPS, RET_CHUNK, 1), lambda b, i: (0, 0, 0)),
    ]
    args += [ws, bs]
    return pl.pallas_call(
        functools.partial(_mix0_kernel, tc=tc, nsq=nsq, whole_seq=whole_seq),
        grid=(bsz, ls // t),
        in_specs=in_specs,
        out_specs=out_specs,
        out_shape=out_shape,
        scratch_shapes=[] if whole_seq else [pltpu.VMEM(st, f32)],
        compiler_params=_cparams(("parallel", "parallel" if whole_seq else "arbitrary"), VMEM_FULL_MIB),
        name="mix0_seq" if whole_seq else "mix0",
    )(*args)


def _loop(n, body, carry):
    for i in range(n):
        carry = body(i, carry)
    return carry


def _lru_body(gate_ref, xin_ref, cw_ref, cb_ref, wg_ref, lam_ref, h0f_ref, h0b_ref,
              o_ref, hfl_ref, hbf_ref, xs, xt, af, bfs, ab, bbs, *, nseg, chained):
    seg, pitch = LRU_SEG, LRU_PITCH
    blk = seg

    def stage(s, carry):
        xs[pl.ds(s * pitch, seg), :] = xin_ref[pl.ds(s * seg, seg), :].astype(f32)
        return carry

    _loop(nseg, stage, 0)

    def tile(j):
        return pl.ds(j * nseg, nseg)

    def gather(j, carry):
        xt[tile(j + 2), :] = xs[pl.ds(j, nseg, stride=pitch), :]
        return carry

    _loop(seg, gather, 0)
    if chained:
        srow = lax.broadcasted_iota(jnp.int32, (nseg, LANES), 0)
        xt[tile(0), :] = jnp.where(srow == 0, 0.0, pltpu.roll(xt[tile(seg), :], 1, axis=0))
        xt[tile(1), :] = jnp.where(srow == 0, 0.0, pltpu.roll(xt[tile(seg + 1), :], 1, axis=0))
        xt[tile(seg + 2), :] = jnp.where(srow == nseg - 1, 0.0, pltpu.roll(xt[tile(2), :], nseg - 1, axis=0))
    else:
        zt = jnp.zeros((nseg, LANES), f32)
        xt[tile(0), :] = zt
        xt[tile(1), :] = zt
        xt[tile(seg + 2), :] = zt

    cw = cw_ref[...]
    cb = cb_ref[...]
    sp4 = (0.5 * LRU_C) * jax.nn.softplus(-lam_ref[...])
    sp4e = sp4 * (-LOG2_E)
    ones = (lax.broadcasted_iota(jnp.int32, (blk, LANES), 1) < N_BIAS_TERMS).astype(bf16)

    def gates(jb, carry):
        p0 = jb * blk
        xm2 = xt[pl.ds(p0, blk), :]
        xm1 = xt[pl.ds(p0 + nseg, blk), :]
        x0 = xt[pl.ds(p0 + 2 * nseg, blk), :]
        xp1 = xt[pl.ds(p0 + 3 * nseg, blk), :]
        xch = cb + xm2 * cw[0:1] + xm1 * cw[1:2] + x0 * cw[2:3] + xp1 * cw[3:4]
        lhs = jnp.concatenate([xch.astype(bf16), ones], axis=1)
        gt = jnp.dot(lhs, wg_ref[...], preferred_element_type=f32)
        for d, (a_s, b_s) in enumerate(((af, bfs), (ab, bbs))):
            tr = jnp.tanh(gt[:, (2 * d) * LANES:(2 * d + 1) * LANES])
            ti = jnp.tanh(gt[:, (2 * d + 1) * LANES:(2 * d + 2) * LANES])
            r2 = tr + 1.0
            nla = r2 * sp4[d:d + 1]
            a = jnp.exp2(r2 * sp4e[d:d + 1])
            m2 = jnp.tanh(nla) * (1.0 + a * a)
            mult = m2 * lax.rsqrt(jnp.maximum(m2, 1e-30))
            a_s[pl.ds(p0, blk), :] = a
            b_s[pl.ds(p0, blk), :] = mult * ((ti + 1.0) * xch)
        return carry

    _loop(nseg, gates, 0)

    if chained:
        def ends(j, carry):
            hf, pf, hb, pb = carry
            a = af[tile(j), :]
            hf = a * hf + bfs[tile(j), :]
            pf = a * pf
            a2 = ab[tile(seg - 1 - j), :]
            hb = a2 * hb + bbs[tile(seg - 1 - j), :]
            pb = a2 * pb
            return hf, pf, hb, pb

        zero = jnp.zeros((nseg, LANES), f32)
        one = jnp.ones((nseg, LANES), f32)
        hf_e, pf_e, hb_e, pb_e = _loop(seg, ends, (zero, one, zero, one))
        cf = [h0f_ref[0:1, :]]
        for s in range(1, nseg):
            cf.append(hf_e[s - 1:s] + pf_e[s - 1:s] * cf[-1])
        cbk = [h0b_ref[0:1, :]]
        for s in range(nseg - 2, -1, -1):
            cbk.append(hb_e[s + 1:s + 2] + pb_e[s + 1:s + 2] * cbk[-1])
        hf0 = jnp.concatenate(cf, axis=0)
        hb0 = jnp.concatenate(cbk[::-1], axis=0)
    else:
        hf0 = h0f_ref[...]
        hb0 = h0b_ref[...]

    def scan_b(k, hb):
        j = seg - 1 - k
        hb = ab[tile(j), :] * hb + bbs[tile(j), :]
        xt[tile(j), :] = hb
        return hb

    hb_l = _loop(seg, scan_b, hb0)

    def scan_f(j, hf):
        hf = af[tile(j), :] * hf + bfs[tile(j), :]
        xs[pl.ds(j, nseg, stride=pitch), :] = hf + xt[tile(j), :]
        return hf

    hf_l = _loop(seg, scan_f, hf0)
    if chained:
        hfl_ref[...] = hf_l[nseg - 1:nseg]
        hbf_ref[...] = hb_l[0:1]
    else:
        hfl_ref[...] = hf_l
        hbf_ref[...] = hb_l

    def combine(s, carry):
        g = gate_ref[pl.ds(s * seg, seg), :].astype(f32)
        o_ref[pl.ds(s * seg, seg), :] = (g * xs[pl.ds(s * pitch, seg), :]).astype(bf16)
        return carry

    _loop(nseg, combine, 0)


def _lru_specs(ntok, nseg, gc, pair):
    rows = LRU_SEG * nseg
    nt = ntok // rows
    nb = LRU_BLOCKS
    in_specs = [
        pl.BlockSpec((rows, LANES), lambda *g: pair(*g)),
        pl.BlockSpec((rows, LANES), lambda *g: (pair(*g)[0], nb + pair(*g)[1])),
        pl.BlockSpec((4, LANES), lambda *g: (0, pair(*g)[1])),
        pl.BlockSpec((1, LANES), lambda *g: (0, pair(*g)[1])),
        pl.BlockSpec((None, 2 * LANES, 4 * LANES), lambda *g: (pair(*g)[1], 0, 0)),
        pl.BlockSpec((2, LANES), lambda *g: (0, pair(*g)[1])),
        pl.BlockSpec((None, gc, LANES), lambda *g: (pair(*g)[0], 0, pair(*g)[1])),
        pl.BlockSpec((None, gc, LANES), lambda *g: (pair(*g)[0], 0, pair(*g)[1])),
    ]
    out_specs = [
        pl.BlockSpec((rows, LANES), lambda *g: pair(*g)),
        pl.BlockSpec((None, gc, LANES), lambda *g: (pair(*g)[0], 0, pair(*g)[1])),
        pl.BlockSpec((None, gc, LANES), lambda *g: (pair(*g)[0], 0, pair(*g)[1])),
    ]
    out_shape = [
        jax.ShapeDtypeStruct((ntok, D_RNN), bf16),
        jax.ShapeDtypeStruct((nt, gc, D_RNN), f32),
        jax.ShapeDtypeStruct((nt, gc, D_RNN), f32),
    ]
    scr = pltpu.VMEM((rows, LANES), f32)
    scratch = [pltpu.VMEM((nseg * LRU_PITCH, LANES), f32), pltpu.VMEM((rows + 3 * nseg, LANES), f32),
               scr, scr, scr, scr]
    return in_specs, out_specs, out_shape, scratch


def _outmlp_body(x_ref, mix_ref, g1_ref, sh2_ref, sc2_ref, g2_ref, gain_ref, fgain_ref,
                 wo_ref, w1_ref, w2_ref, *, final):
    mix = jnp.dot(mix_ref[...], wo_ref[...], preferred_element_type=f32)
    x1 = x_ref[...] + g1_ref[...] * mix
    h = (_rms(x1, gain_ref[...]) * (1.0 + sc2_ref[...]) + sh2_ref[...]).astype(bf16)
    acc = None
    for j in range(0, D_FF, FF_CHUNK):
        hid = jnp.dot(h, w1_ref[:, j:j + FF_CHUNK], preferred_element_type=f32)
        hid = jnp.square(jnp.maximum(hid, 0.0)).astype(bf16)
        part = jnp.dot(hid, w2_ref[j:j + FF_CHUNK, :], preferred_element_type=f32)
        acc = part if acc is None else acc + part
    x2 = x1 + g2_ref[...] * acc
    if final:
        x2 = _rms(x2, fgain_ref[...])
    return x2


N_MLP_IN = 11
N_LRU_IN = 8


def _outmlp_kernel(*refs, final, lru):
    n_in = N_MLP_IN + (N_LRU_IN if lru else 0)
    if lru is not None:
        _lru_body(*refs[N_MLP_IN:n_in], *refs[n_in + 1:], nseg=lru[0], chained=lru[1])
    refs[n_in][...] = _outmlp_body(*refs[:N_MLP_IN], final=final)


def _outmlp_call(x3, mix3, modl, gain, fgain, wo, w1, w2, *, layer, final, t, lru_args=None):
    bg, lg, d = x3.shape
    steps = lg // t
    dm = mix3.shape[-1]
    mod_spec = lambda k: pl.BlockSpec((None, 1, d), lambda b, i, k=k: (b, 0, k))

    def resident(shape, lead=None):
        if lead is None:
            return pl.BlockSpec(shape, lambda b, i: (0, 0), pipeline_mode=pl.Buffered(1))
        return pl.BlockSpec((None,) + shape, lambda b, i: (lead, 0, 0), pipeline_mode=pl.Buffered(1))

    in_specs = [
        pl.BlockSpec((None, t, d), lambda b, i: (b, i, 0)),
        pl.BlockSpec((None, t, dm), lambda b, i: (b, i, 0)),
        mod_spec(2), mod_spec(3), mod_spec(4), mod_spec(5),
        pl.BlockSpec((1, d), lambda b, i: (0, 0)),
        pl.BlockSpec((1, d), lambda b, i: (0, 0)),
        resident((dm, d)), resident((d, D_FF), layer), resident((D_FF, d), layer),
    ]
    args = [x3, mix3, modl, modl, modl, modl, gain.reshape(1, d), fgain.reshape(1, d), wo, w1, w2]
    out_specs = [pl.BlockSpec((None, t, d), lambda b, i: (b, i, 0))]
    out_shape = [jax.ShapeDtypeStruct((bg, lg, d), f32)]
    scratch = []
    lru = None
    if lru_args is not None:
        gx, cw, cb, wg, lam, h0f, h0b, nseg, chained = lru_args
        ntok = gx.shape[0]
        assert (ntok // (LRU_SEG * nseg)) * LRU_BLOCKS == bg * steps, "one RG-LRU (tile, block) pair per MLP step"

        def pair(b, i):
            p = b * steps + i
            return p // LRU_BLOCKS, p % LRU_BLOCKS

        li, lo, ls, scratch = _lru_specs(ntok, nseg, h0f.shape[1], pair)
        in_specs += li
        out_specs += lo
        out_shape += ls
        args += [gx, gx, cw, cb, wg, lam, h0f, h0b]
        lru = (nseg, chained)
    if lru is None:
        def outer(*refs):
            pltpu.emit_pipeline(functools.partial(_outmlp_kernel, final=final, lru=None), grid=(bg, steps),
                                in_specs=in_specs, out_specs=out_specs)(*refs)

        hbm = pl.BlockSpec(memory_space=pl.ANY)
        return pl.pallas_call(
            outer,
            in_specs=[hbm] * len(args),
            out_specs=[hbm],
            out_shape=out_shape,
            compiler_params=pltpu.CompilerParams(vmem_limit_bytes=VMEM_FULL_MIB * MIB),
            name="outmlp_final" if final else "outmlp",
        )(*args)[0]
    res = pl.pallas_call(
        functools.partial(_outmlp_kernel, final=final, lru=lru),
        grid=(bg, steps),
        in_specs=in_specs,
        out_specs=out_specs,
        out_shape=out_shape,
        scratch_shapes=scratch,
        compiler_params=_cparams(("parallel", "parallel"), VMEM_FULL_MIB),
        name=("outmlp_final" if final else "outmlp") + ("_rglru" if lru else ""),
    )(*args)
    return res if lru else res[0]


@functools.lru_cache(maxsize=None)
def _rope_tables(length):
    pos = np.arange(length)
    rowp = (pos // GRID_W).astype(np.float64)
    colp = (pos % GRID_W).astype(np.float64)
    nf = RET_DK // 4
    inv_freq = ROPE_BASE ** (-np.arange(nf, dtype=np.float64) / nf)
    ar = rowp[:, None] * inv_freq[None, :]
    ac = colp[:, None] * inv_freq[None, :]
    cos = np.concatenate([np.cos(ar), np.cos(ar), np.cos(ac), np.cos(ac)], axis=-1).astype(np.float32)
    sin = np.concatenate([-np.sin(ar), np.sin(ar), -np.sin(ac), np.sin(ac)], axis=-1).astype(np.float32)
    return cos, sin


def kernel(x_prompt, x_sample, c, c_ctx, state_ret_fwd, state_ret_bwd, state_lru_fwd, state_lru_bwd, mod_w, mod_b, norm_mix, norm_mlp, mlp_w1, mlp_w2, ab_w_in, ab_w_out, ret_decay_fwd, ret_decay_bwd, gmlp_ws, gmlp_bs, lru_w_in, lru_conv_w, lru_conv_b, lru_gate_a_w, lru_gate_a_b, lru_gate_x_w, lru_gate_x_b, lru_lambda, lru_w_out, final_norm):
    d = D_MODEL
    n_lat = c.shape[0]
    conds = jnp.concatenate([c_ctx[None, :], c, jnp.zeros((COND_ROWS - 1 - n_lat, d), f32)], axis=0)
    mod = _mod_call(conds, mod_w, mod_b)

    w_in0 = ab_w_in[0].astype(bf16)
    w_out0 = ab_w_out[0].astype(bf16)
    w1 = mlp_w1.astype(bf16)
    w2 = mlp_w2.astype(bf16)
    w_in1 = lru_w_in[0].astype(bf16)
    w_out1 = lru_w_out[0].astype(bf16)
    ws = gmlp_ws[0].astype(bf16)
    bs = gmlp_bs[0][:, :, None]
    dec = jnp.broadcast_to(jnp.stack([ret_decay_fwd[0], ret_decay_bwd[0]])[:, :, None, None],
                           (2, RET_HEADS, 1, LANES))
    wg = jnp.concatenate([lru_gate_a_w[0, 0], lru_gate_x_w[0, 0], lru_gate_a_w[0, 1], lru_gate_x_w[0, 1]],
                         axis=-1).astype(bf16)
    bgate = 0.5 * jnp.stack([lru_gate_a_b[0, 0], lru_gate_x_b[0, 0], lru_gate_a_b[0, 1], lru_gate_x_b[0, 1]], axis=0)
    bgate = bgate.reshape(4, LRU_BLOCKS, LRU_BS).transpose(1, 0, 2).reshape(LRU_BLOCKS, 1, 4 * LRU_BS)
    terms, rem = [], bgate
    for _ in range(N_BIAS_TERMS):
        terms.append(rem.astype(bf16))
        rem = rem - terms[-1].astype(f32)
    bias_rows = jnp.pad(jnp.concatenate(terms, axis=1), ((0, 0), (0, LANES - N_BIAS_TERMS), (0, 0)))
    wg = jnp.concatenate([wg, bias_rows], axis=1)
    lru_w = (0.5 * lru_conv_w[0], 0.5 * lru_conv_b[0].reshape(1, D_RNN), wg, lru_lambda[0])

    def mixer0(x3, modl, bsz, ls, ret_f0, ret_b0, latent):
        tc = min(MIX_CHUNKS, ls // RET_CHUNK)
        if latent:
            qkv, sbn = _inproj_call(x3, modl, norm_mix[0], w_in0, _rope_tables(ls), retention=True,
                                    sh_col=0, sc_col=1, bwd=(dec, ret_b0))
            mix = _mix0_call(qkv, dec, ret_f0, sbn, ws, bs, tc)[0]
            sff = sbf = None
        else:
            qkv = _inproj_call(x3, modl, norm_mix[0], w_in0, None, retention=True, sh_col=0, sc_col=1)
            nsq = MIX_CHUNKS // tc
            mix, sff, sbf = _mix0_call(qkv.reshape(bsz // nsq, nsq * ls, AB_IN), dec, None, None, ws, bs,
                                       nsq * tc, nsq)
        return mix.reshape(x3.shape[0], x3.shape[1], AB_OUT), sff, sbf

    def lru_inproj(x3, modl):
        gx = _inproj_call(x3, modl, norm_mix[1], w_in1, None, retention=False, sh_col=0, sc_col=1)
        return gx.reshape(-1, 2 * D_RNN)

    bp, lp, _ = x_prompt.shape
    bl, ll, _ = x_sample.shape
    assert lp == LRU_SEG and ll % LRU_SEG == 0
    xc = x_prompt.reshape(1, bp * lp, d)
    xl = x_sample
    mod_c = [mod[l, 0:1][:, None, :] for l in range(DEPTH)]
    mod_l = [mod[l, 1:1 + n_lat][:, None, :] for l in range(DEPTH)]

    mix_c, sff, sbf = mixer0(xc, mod_c[0], bp, lp, None, None, False)
    xc = _outmlp_call(xc, mix_c, mod_c[0], norm_mlp[0], final_norm, w_out0, w1, w2, layer=0, final=False,
                      t=PLAIN_MLP_TILE)
    gx_c = lru_inproj(xc, mod_c[1])
    mix_l, _, _ = mixer0(xl, mod_l[0], bl, ll, state_ret_fwd[:, 0], state_ret_bwd[:, 0], True)

    steps_l = bl * (ll // TOK_TILE)
    nseg_c = (bp * lp // LRU_SEG) * LRU_BLOCKS // steps_l
    zero_lru = jnp.zeros((bp // nseg_c, nseg_c, D_RNN), f32)
    xl, pre_c, hfl, hbf = _outmlp_call(
        xl, mix_l, mod_l[0], norm_mlp[0], final_norm, w_out0, w1, w2, layer=0, final=False, t=TOK_TILE,
        lru_args=(gx_c,) + lru_w + (zero_lru, zero_lru, nseg_c, False))
    gx_l = lru_inproj(xl, mod_l[1])

    nseg_l = ll // LRU_SEG
    t_c = bp * lp // (bl * LRU_BLOCKS)
    y_c, pre_l, _, _ = _outmlp_call(
        xc, pre_c.reshape(1, bp * lp, D_RNN), mod_c[1], norm_mlp[1], final_norm, w_out1, w1, w2,
        layer=1, final=True, t=t_c,
        lru_args=(gx_l,) + lru_w + (state_lru_fwd[:, 0].reshape(bl, 1, D_RNN),
                                    state_lru_bwd[:, 0].reshape(bl, 1, D_RNN), nseg_l, True))
    y_l = _outmlp_call(xl, pre_l.reshape(bl, ll, D_RNN), mod_l[1], norm_mlp[1], final_norm, w_out1, w1, w2,
                       layer=1, final=True, t=PLAIN_MLP_TILE)
    return (y_c.reshape(bp, lp, d), y_l, sff[:, None], sbf[:, None],
            hfl.reshape(bp, 1, D_RNN), hbf.reshape(bp, 1, D_RNN))
```

```python
import functools

import jax
import jax.numpy as jnp
import numpy as np
from jax import lax
from jax.experimental import pallas as pl
from jax.experimental.pallas import tpu as pltpu

f32 = jnp.float32
bf16 = jnp.bfloat16

D_MODEL = 1024
DEPTH = 2
GRID_W = 64
EPS = 1e-6
RET_HEADS = 4
RET_DK = 128
RET_DV = 128
RET_CHUNK = 128
ROPE_BASE = 10000.0
GMLP_GROUPS = 4
GMLP_CH = 128
AB_IN = 3072
AB_OUT = 1024
D_RNN = D_MODEL
LRU_BLOCKS = 8
LRU_BS = 128
LRU_C = 8.0
D_FF = 4 * D_MODEL

LANES = 128
SUBLANES = 8
MIB = 1024 * 1024

TOK_TILE = 512
PLAIN_MLP_TILE = 1024
INPROJ_TILE = 1024
COND_ROWS = 8
MOD_TN = 3072
LRU_SEG = 256
LRU_PITCH = LRU_SEG + 4
FF_CHUNK = 1024
N_BIAS_TERMS = 3
LOG2_E = 1.4426950408889634
GELU_C1 = 0.7978845608028654
GELU_C3 = 0.044715 * GELU_C1
MIX_CHUNKS = 8
PROJ_COLS = 512
ROPE_ROT = RET_DK // 4
VMEM_MOD_MIB = 40
VMEM_PROJ_MIB = 48
VMEM_FULL_MIB = 58


def _cparams(sem, vmem_mib):
    return pltpu.CompilerParams(dimension_semantics=sem, vmem_limit_bytes=vmem_mib * MIB)


def _gelu(x):
    hx = 0.5 * x
    return hx + hx * jnp.tanh(x * (GELU_C1 + GELU_C3 * (x * x)))


def _silu(x):
    hx = 0.5 * x
    return hx + hx * jnp.tanh(hx)


def _rms(x, gain):
    ms = jnp.mean(x * x, axis=-1, keepdims=True)
    return x * lax.rsqrt(ms + EPS) * gain


def _mod_kernel(cond_ref, w_ref, b_ref, o_ref):
    c = cond_ref[...]
    s = c * jax.nn.sigmoid(c)
    o_ref[...] = jnp.dot(s.astype(bf16), w_ref[...].astype(bf16), preferred_element_type=f32) + b_ref[...]


def _mod_call(conds, mod_w, mod_b):
    depth, d, n = mod_w.shape
    return pl.pallas_call(
        _mod_kernel,
        grid=(depth, n // MOD_TN),
        in_specs=[
            pl.BlockSpec((COND_ROWS, d), lambda l, j: (0, 0)),
            pl.BlockSpec((None, d, MOD_TN), lambda l, j: (l, 0, j)),
            pl.BlockSpec((None, 1, MOD_TN), lambda l, j: (l, 0, j)),
        ],
        out_specs=pl.BlockSpec((None, COND_ROWS, MOD_TN), lambda l, j: (l, 0, j)),
        out_shape=jax.ShapeDtypeStruct((depth, COND_ROWS, n), f32),
        compiler_params=_cparams(("parallel", "parallel"), VMEM_MOD_MIB),
        name="mod",
    )(conds, mod_w, mod_b.reshape(depth, 1, n))


def _rope(t, cos, sin_signed, first_half):
    up = pltpu.roll(t, LANES - ROPE_ROT, axis=1)
    dn = pltpu.roll(t, ROPE_ROT, axis=1)
    return t * cos + jnp.where(first_half, up, dn) * sin_signed


def _inproj_kernel(*refs, retention, rope, bwd_states):
    if bwd_states:
        (x_ref, sh_ref, sc_ref, gain_ref, w_ref, cos_ref, sin_ref, dec_ref, s0b_ref,
         o_ref, sbn_ref, sb_scr) = refs

        @pl.when(pl.program_id(1) == 0)
        def _():
            sb_scr[...] = s0b_ref[...]
    elif rope:
        x_ref, sh_ref, sc_ref, gain_ref, w_ref, cos_ref, sin_ref, o_ref = refs
    else:
        x_ref, sh_ref, sc_ref, gain_ref, w_ref, o_ref = refs
    h = _rms(x_ref[...], gain_ref[...]) * (1.0 + sc_ref[...]) + sh_ref[...]
    hb = h.astype(bf16)
    n = w_ref.shape[1]
    if not retention:
        for j in range(0, n, PROJ_COLS):
            y = jnp.dot(hb, w_ref[:, j:j + PROJ_COLS], preferred_element_type=f32)
            o_ref[:, j:j + PROJ_COLS] = (_gelu(y) if j < n // 2 else y).astype(bf16)
        return
    qk = RET_HEADS * RET_DK
    if rope:
        cos = cos_ref[...]
        sin = sin_ref[...]
        lane = lax.broadcasted_iota(jnp.int32, cos.shape, 1)
        first_half = (lane % (2 * ROPE_ROT)) < ROPE_ROT
    k_heads = []
    for part in range(2):
        y = jnp.dot(hb, w_ref[:, part * qk:(part + 1) * qk], preferred_element_type=f32)
        for hh in range(RET_HEADS):
            t = y[:, hh * RET_DK:(hh + 1) * RET_DK]
            if part == 1:
                t = t * (RET_DK ** -0.5)
            if rope:
                t = _rope(t, cos, sin, first_half)
            c0 = part * qk + hh * RET_DK
            tb = t.astype(bf16)
            o_ref[:, c0:c0 + RET_DK] = tb
            if part == 1:
                k_heads.append(tb)
    acts = (None, _silu, _gelu, _gelu)
    assert qk == PROJ_COLS and n - 2 * qk == len(acts) * PROJ_COLS
    for j, act in zip(range(2 * qk, n, PROJ_COLS), acts):
        y = jnp.dot(hb, w_ref[:, j:j + PROJ_COLS], preferred_element_type=f32)
        yb = (y if act is None else act(y)).astype(bf16)
        o_ref[:, j:j + PROJ_COLS] = yb
        if bwd_states and j == 2 * qk:
            vb = yb
    if bwd_states:
        row = lax.broadcasted_iota(jnp.int32, (RET_CHUNK, LANES), 0).astype(f32)
        nchunk = x_ref.shape[0] // RET_CHUNK
        for hh in range(RET_HEADS):
            lg_b = jax.nn.log_sigmoid(dec_ref[1, hh])
            kd_b = jnp.exp(lg_b * row)
            gc_b = jnp.exp(lg_b * float(RET_CHUNK))
            cs = slice(hh * RET_DK, (hh + 1) * RET_DK)
            for cc in reversed(range(nchunk)):
                rs = slice(cc * RET_CHUNK, (cc + 1) * RET_CHUNK)
                s_next = sb_scr[hh]
                sbn_ref[cc, hh] = s_next.astype(bf16)
                kd = (k_heads[hh][rs].astype(f32) * kd_b).astype(bf16)
                kv = lax.dot_general(kd, vb[rs, cs], (((0,), (0,)), ((), ())), preferred_element_type=f32)
                sb_scr[hh] = gc_b * s_next + kv


def _inproj_call(x3, modl, gain, w, rope_tabs, *, retention, sh_col, sc_col, bwd=None):
    bg, lg, d = x3.shape
    n = w.shape[1]
    t = INPROJ_TILE
    ns = lg // t
    rope = rope_tabs is not None
    pos = (lambda i: ns - 1 - i) if bwd is not None else (lambda i: i)
    in_specs = [
        pl.BlockSpec((None, t, d), lambda b, i: (b, pos(i), 0)),
        pl.BlockSpec((None, 1, d), lambda b, i: (b, 0, sh_col)),
        pl.BlockSpec((None, 1, d), lambda b, i: (b, 0, sc_col)),
        pl.BlockSpec((1, d), lambda b, i: (0, 0)),
        pl.BlockSpec((d, n), lambda b, i: (0, 0), pipeline_mode=pl.Buffered(1)),
    ]
    args = [x3, modl, modl, gain.reshape(1, d), w]
    if rope:
        in_specs += [pl.BlockSpec((t, LANES), lambda b, i: (pos(i), 0))] * 2
        args += list(rope_tabs)
    out_specs = [pl.BlockSpec((None, t, n), lambda b, i: (b, pos(i), 0))]
    out_shape = [jax.ShapeDtypeStruct((bg, lg, n), bf16)]
    scratch = []
    if bwd is not None:
        st = (RET_HEADS, RET_DK, RET_DV)
        tc = t // RET_CHUNK
        in_specs += [pl.BlockSpec((2, RET_HEADS, 1, LANES), lambda b, i: (0, 0, 0, 0)),
                     pl.BlockSpec((None,) + st, lambda b, i: (b, 0, 0, 0))]
        args += list(bwd)
        out_specs.append(pl.BlockSpec((None, tc) + st, lambda b, i: (b, pos(i), 0, 0, 0)))
        out_shape.append(jax.ShapeDtypeStruct((bg, lg // RET_CHUNK) + st, bf16))
        scratch.append(pltpu.VMEM(st, f32))
    res = pl.pallas_call(
        functools.partial(_inproj_kernel, retention=retention, rope=rope, bwd_states=bwd is not None),
        grid=(bg, ns),
        in_specs=in_specs,
        out_specs=out_specs,
        out_shape=out_shape,
        scratch_shapes=scratch,
        compiler_params=_cparams(("parallel", "arbitrary" if bwd is not None else "parallel"), VMEM_PROJ_MIB),
        name="inproj_ret" if retention else "inproj_lru",
    )(*args)
    return res if bwd is not None else res[0]


def _row_iota():
    return lax.broadcasted_iota(jnp.int32, (RET_CHUNK, LANES), 0).astype(f32)


def _mix0_kernel(*refs, tc, nsq, whole_seq):
    if whole_seq:
        dec_ref, q_ref, k_ref, v_ref, g_ref, u_ref, vv_ref, ws_ref, bs_ref, o_ref, sff_ref, sbf_ref = refs
    else:
        (dec_ref, q_ref, k_ref, v_ref, g_ref, u_ref, vv_ref, s0f_ref, sbn_ref, ws_ref, bs_ref,
         o_ref, sf_scr) = refs

        @pl.when(pl.program_id(1) == 0)
        def _():
            sf_scr[...] = s0f_ref[...]

    row = _row_iota()
    col = lax.broadcasted_iota(jnp.int32, (RET_CHUNK, LANES), 1).astype(f32)
    c = float(RET_CHUNK)
    nt = (((1,), (1,)), ((), ()))
    tn = (((0,), (0,)), ((), ()))
    for hh in range(RET_HEADS):
        lg_f = jax.nn.log_sigmoid(dec_ref[0, hh])
        lg_b = jax.nn.log_sigmoid(dec_ref[1, hh])
        decay = jnp.where(row >= col,
                          jnp.exp(lg_f * jnp.maximum(row - col, 0.0)),
                          jnp.exp(lg_b * jnp.maximum(col - row - 1.0, 0.0)))
        qd_f = jnp.exp(lg_f * (row + 1.0))
        qd_b = jnp.exp(lg_b * (c - 1.0 - row))
        cs = slice(hh * RET_DK, (hh + 1) * RET_DK)
        kd_f = jnp.exp(lg_f * (c - 1.0 - row))
        kd_b = jnp.exp(lg_b * row)
        gc_f = jnp.exp(lg_f * c)
        gc_b = jnp.exp(lg_b * c)
        s_prev, s_next = [None] * tc, [None] * tc
        cps = tc // nsq
        for sq in range(nsq):
            chunks = range(sq * cps, (sq + 1) * cps)
            sf = None if whole_seq else sf_scr[hh]
            sb = None
            for cc in chunks:
                rs = slice(cc * RET_CHUNK, (cc + 1) * RET_CHUNK)
                kf = k_ref[rs, cs].astype(f32)
                kv = lax.dot_general((kf * kd_f).astype(bf16), v_ref[rs, cs], tn, preferred_element_type=f32)
                s_prev[cc] = sf
                sf = kv if sf is None else gc_f * sf + kv
            if whole_seq:
                for cc in reversed(chunks):
                    rs = slice(cc * RET_CHUNK, (cc + 1) * RET_CHUNK)
                    kf = k_ref[rs, cs].astype(f32)
                    kv = lax.dot_general((kf * kd_b).astype(bf16), v_ref[rs, cs], tn, preferred_element_type=f32)
                    s_next[cc] = sb
                    sb = kv if sb is None else gc_b * sb + kv
                sff_ref[sq, hh] = sf
                sbf_ref[sq, hh] = sb
            else:
                sf_scr[hh] = sf
        for cc in range(tc):
            rs = slice(cc * RET_CHUNK, (cc + 1) * RET_CHUNK)
            q = q_ref[rs, cs]
            qf = q.astype(f32)
            s = lax.dot_general(q, k_ref[rs, cs], nt, preferred_element_type=f32)
            o = jnp.dot((s * decay).astype(bf16), v_ref[rs, cs], preferred_element_type=f32)
            sp_c = None if s_prev[cc] is None else s_prev[cc].astype(bf16)
            if whole_seq:
                sn_c = None if s_next[cc] is None else s_next[cc].astype(bf16)
            else:
                sn_c = sbn_ref[cc, hh]
            if sp_c is not None:
                o = o + jnp.dot((qf * qd_f).astype(bf16), sp_c, preferred_element_type=f32)
            if sn_c is not None:
                o = o + jnp.dot((qf * qd_b).astype(bf16), sn_c, preferred_element_type=f32)
            mu = jnp.mean(o, axis=-1, keepdims=True)
            var = jnp.mean(jnp.square(o - mu), axis=-1, keepdims=True)
            on = (o - mu) * lax.rsqrt(var + EPS)
            o_ref[rs, cs] = (g_ref[rs, cs].astype(f32) * on).astype(bf16)
    off = RET_HEADS * RET_DV
    for gg in range(GMLP_GROUPS):
        cs = slice(gg * GMLP_CH, (gg + 1) * GMLP_CH)
        w = ws_ref[gg]
        bias = bs_ref[gg]
        for cc in range(tc):
            rs = slice(cc * RET_CHUNK, (cc + 1) * RET_CHUNK)
            gu = u_ref[rs, cs].astype(f32)
            gv = vv_ref[rs, cs].astype(f32)
            gv = gv * lax.rsqrt(jnp.mean(gv * gv, axis=-1, keepdims=True) + EPS)
            sp = jnp.dot(w, gv.astype(bf16), preferred_element_type=f32) + bias
            o_ref[rs, off + gg * GMLP_CH: off + (gg + 1) * GMLP_CH] = (gu * sp).astype(bf16)


def _mix0_call(qkv, dec, s0f, sbn, ws, bs, tc, nsq=1):
    bsz, ls, _ = qkv.shape
    t = tc * RET_CHUNK
    hw = RET_HEADS * RET_DK
    st = (RET_HEADS, RET_DK, RET_DV)
    col_spec = lambda j: pl.BlockSpec((None, t, hw), lambda b, i, j=j: (b, i, j))
    whole_seq = s0f is None
    in_specs = [
        pl.BlockSpec((2, RET_HEADS, 1, LANES), lambda b, i: (0, 0, 0, 0)),
        col_spec(0), col_spec(1), col_spec(2), col_spec(3), col_spec(4), col_spec(5),
    ]
    args = [dec, qkv, qkv, qkv, qkv, qkv, qkv]
    out_specs = [pl.BlockSpec((None, t, AB_OUT), lambda b, i: (b, i, 0))]
    out_shape = [jax.ShapeDtypeStruct((bsz, ls, AB_OUT), bf16)]
    if whole_seq:
        assert ls == t
        out_specs += [pl.BlockSpec((nsq,) + st, lambda b, i: (b, 0, 0, 0))] * 2
        out_shape += [jax.ShapeDtypeStruct((bsz * nsq,) + st, f32)] * 2
    else:
        in_specs += [pl.BlockSpec((None,) + st, lambda b, i: (b, 0, 0, 0)),
                     pl.BlockSpec((None, tc) + st, lambda b, i: (b, i, 0, 0, 0))]
        args += [s0f, sbn]
    in_specs += [
        pl.BlockSpec((GMLP_GROUPS, RET_CHUNK, RET_CHUNK), lambda b, i: (0, 0, 0)),
        pl.BlockSpec((GMLP_GROUPS, RET_CHUNK, 1), lambda b, i: (0, 0, 0)),
    ]
    args += [ws, bs]
    return pl.pallas_call(
        functools.partial(_mix0_kernel, tc=tc, nsq=nsq, whole_seq=whole_seq),
        grid=(bsz, ls // t),
        in_specs=in_specs,
        out_specs=out_specs,
        out_shape=out_shape,
        scratch_shapes=[] if whole_seq else [pltpu.VMEM(st, f32)],
        compiler_params=_cparams(("parallel", "parallel" if whole_seq else "arbitrary"), VMEM_FULL_MIB),
        name="mix0_seq" if whole_seq else "mix0",
    )(*args)


def _loop(n, body, carry):
    for i in range(n):
        carry = body(i, carry)
    return carry


def _lru_body(gate_ref, xin_ref, cw_ref, cb_ref, wg_ref, lam_ref, h0f_ref, h0b_ref,
              o_ref, hfl_ref, hbf_ref, xs, xt, af, bfs, ab, bbs, *, nseg, chained):
    seg, pitch = LRU_SEG, LRU_PITCH
    blk = seg

    def stage(s, carry):
        xs[pl.ds(s * pitch, seg), :] = xin_ref[pl.ds(s * seg, seg), :].astype(f32)
        return carry

    _loop(nseg, stage, 0)

    def tile(j):
        return pl.ds(j * nseg, nseg)

    def gather(j, carry):
        xt[tile(j + 2), :] = xs[pl.ds(j, nseg, stride=pitch), :]
        return carry

    _loop(seg, gather, 0)
    if chained:
        srow = lax.broadcasted_iota(jnp.int32, (nseg, LANES), 0)
        xt[tile(0), :] = jnp.where(srow == 0, 0.0, pltpu.roll(xt[tile(seg), :], 1, axis=0))
        xt[tile(1), :] = jnp.where(srow == 0, 0.0, pltpu.roll(xt[tile(seg + 1), :], 1, axis=0))
        xt[tile(seg + 2), :] = jnp.where(srow == nseg - 1, 0.0, pltpu.roll(xt[tile(2), :], nseg - 1, axis=0))
    else:
        zt = jnp.zeros((nseg, LANES), f32)
        xt[tile(0), :] = zt
        xt[tile(1), :] = zt
        xt[tile(seg + 2), :] = zt

    cw = cw_ref[...]
    cb = cb_ref[...]
    sp4 = (0.5 * LRU_C) * jax.nn.softplus(-lam_ref[...])
    sp4e = sp4 * (-LOG2_E)
    ones = (lax.broadcasted_iota(jnp.int32, (blk, LANES), 1) < N_BIAS_TERMS).astype(bf16)

    def gates(jb, carry):
        p0 = jb * blk
        xm2 = xt[pl.ds(p0, blk), :]
        xm1 = xt[pl.ds(p0 + nseg, blk), :]
        x0 = xt[pl.ds(p0 + 2 * nseg, blk), :]
        xp1 = xt[pl.ds(p0 + 3 * nseg, blk), :]
        xch = cb + xm2 * cw[0:1] + xm1 * cw[1:2] + x0 * cw[2:3] + xp1 * cw[3:4]
        lhs = jnp.concatenate([xch.astype(bf16), ones], axis=1)
        gt = jnp.dot(lhs, wg_ref[...], preferred_element_type=f32)
        for d, (a_s, b_s) in enumerate(((af, bfs), (ab, bbs))):
            tr = jnp.tanh(gt[:, (2 * d) * LANES:(2 * d + 1) * LANES])
            ti = jnp.tanh(gt[:, (2 * d + 1) * LANES:(2 * d + 2) * LANES])
            r2 = tr + 1.0
            nla = r2 * sp4[d:d + 1]
            a = jnp.exp2(r2 * sp4e[d:d + 1])
            m2 = jnp.tanh(nla) * (1.0 + a * a)
            mult = m2 * lax.rsqrt(jnp.maximum(m2, 1e-30))
            a_s[pl.ds(p0, blk), :] = a
            b_s[pl.ds(p0, blk), :] = mult * ((ti + 1.0) * xch)
        return carry

    _loop(nseg, gates, 0)

    if chained:
        def ends(j, carry):
            hf, pf, hb, pb = carry
            a = af[tile(j), :]
            hf = a * hf + bfs[tile(j), :]
            pf = a * pf
            a2 = ab[tile(seg - 1 - j), :]
            hb = a2 * hb + bbs[tile(seg - 1 - j), :]
            pb = a2 * pb
            return hf, pf, hb, pb

        zero = jnp.zeros((nseg, LANES), f32)
        one = jnp.ones((nseg, LANES), f32)
        hf_e, pf_e, hb_e, pb_e = _loop(seg, ends, (zero, one, zero, one))
        cf = [h0f_ref[0:1, :]]
        for s in range(1, nseg):
            cf.append(hf_e[s - 1:s] + pf_e[s - 1:s] * cf[-1])
        cbk = [h0b_ref[0:1, :]]
        for s in range(nseg - 2, -1, -1):
            cbk.append(hb_e[s + 1:s + 2] + pb_e[s + 1:s + 2] * cbk[-1])
        hf0 = jnp.concatenate(cf, axis=0)
        hb0 = jnp.concatenate(cbk[::-1], axis=0)
    else:
        hf0 = h0f_ref[...]
        hb0 = h0b_ref[...]

    def scan_b(k, hb):
        j = seg - 1 - k
        hb = ab[tile(j), :] * hb + bbs[tile(j), :]
        xt[tile(j), :] = hb
        return hb

    hb_l = _loop(seg, scan_b, hb0)

    def scan_f(j, hf):
        hf = af[tile(j), :] * hf + bfs[tile(j), :]
        xs[pl.ds(j, nseg, stride=pitch), :] = hf + xt[tile(j), :]
        return hf

    hf_l = _loop(seg, scan_f, hf0)
    if chained:
        hfl_ref[...] = hf_l[nseg - 1:nseg]
        hbf_ref[...] = hb_l[0:1]
    else:
        hfl_ref[...] = hf_l
        hbf_ref[...] = hb_l

    def combine(s, carry):
        g = gate_ref[pl.ds(s * seg, seg), :].astype(f32)
        o_ref[pl.ds(s * seg, seg), :] = (g * xs[pl.ds(s * pitch, seg), :]).astype(bf16)
        return carry

    _loop(nseg, combine, 0)


def _lru_specs(ntok, nseg, gc, pair):
    rows = LRU_SEG * nseg
    nt = ntok // rows
    nb = LRU_BLOCKS
    in_specs = [
        pl.BlockSpec((rows, LANES), lambda *g: pair(*g)),
        pl.BlockSpec((rows, LANES), lambda *g: (pair(*g)[0], nb + pair(*g)[1])),
        pl.BlockSpec((4, LANES), lambda *g: (0, pair(*g)[1])),
        pl.BlockSpec((1, LANES), lambda *g: (0, pair(*g)[1])),
        pl.BlockSpec((None, 2 * LANES, 4 * LANES), lambda *g: (pair(*g)[1], 0, 0)),
        pl.BlockSpec((2, LANES), lambda *g: (0, pair(*g)[1])),
        pl.BlockSpec((None, gc, LANES), lambda *g: (pair(*g)[0], 0, pair(*g)[1])),
        pl.BlockSpec((None, gc, LANES), lambda *g: (pair(*g)[0], 0, pair(*g)[1])),
    ]
    out_specs = [
        pl.BlockSpec((rows, LANES), lambda *g: pair(*g)),
        pl.BlockSpec((None, gc, LANES), lambda *g: (pair(*g)[0], 0, pair(*g)[1])),
        pl.BlockSpec((None, gc, LANES), lambda *g: (pair(*g)[0], 0, pair(*g)[1])),
    ]
    out_shape = [
        jax.ShapeDtypeStruct((ntok, D_RNN), bf16),
        jax.ShapeDtypeStruct((nt, gc, D_RNN), f32),
        jax.ShapeDtypeStruct((nt, gc, D_RNN), f32),
    ]
    scr = pltpu.VMEM((rows, LANES), f32)
    scratch = [pltpu.VMEM((nseg * LRU_PITCH, LANES), f32), pltpu.VMEM((rows + 3 * nseg, LANES), f32),
               scr, scr, scr, scr]
    return in_specs, out_specs, out_shape, scratch


def _outmlp_body(x_ref, mix_ref, g1_ref, sh2_ref, sc2_ref, g2_ref, gain_ref, fgain_ref,
                 wo_ref, w1_ref, w2_ref, *, final):
    mix = jnp.dot(mix_ref[...], wo_ref[...], preferred_element_type=f32)
    x1 = x_ref[...] + g1_ref[...] * mix
    h = (_rms(x1, gain_ref[...]) * (1.0 + sc2_ref[...]) + sh2_ref[...]).astype(bf16)
    acc = None
    for j in range(0, D_FF, FF_CHUNK):
        hid = jnp.dot(h, w1_ref[:, j:j + FF_CHUNK], preferred_element_type=f32)
        hid = jnp.square(jnp.maximum(hid, 0.0)).astype(bf16)
        part = jnp.dot(hid, w2_ref[j:j + FF_CHUNK, :], preferred_element_type=f32)
        acc = part if acc is None else acc + part
    x2 = x1 + g2_ref[...] * acc
    if final:
        x2 = _rms(x2, fgain_ref[...])
    return x2


N_MLP_IN = 11
N_LRU_IN = 8


def _outmlp_kernel(*refs, final, lru):
    n_in = N_MLP_IN + (N_LRU_IN if lru else 0)
    if lru is not None:
        _lru_body(*refs[N_MLP_IN:n_in], *refs[n_in + 1:], nseg=lru[0], chained=lru[1])
    refs[n_in][...] = _outmlp_body(*refs[:N_MLP_IN], final=final)


def _outmlp_call(x3, mix3, modl, gain, fgain, wo, w1, w2, *, layer, final, t, lru_args=None):
    bg, lg, d = x3.shape
    steps = lg // t
    dm = mix3.shape[-1]
    mod_spec = lambda k: pl.BlockSpec((None, 1, d), lambda b, i, k=k: (b, 0, k))

    def resident(shape, lead=None):
        if lead is None:
            return pl.BlockSpec(shape, lambda b, i: (0, 0), pipeline_mode=pl.Buffered(1))
        return pl.BlockSpec((None,) + shape, lambda b, i: (lead, 0, 0), pipeline_mode=pl.Buffered(1))

    in_specs = [
        pl.BlockSpec((None, t, d), lambda b, i: (b, i, 0)),
        pl.BlockSpec((None, t, dm), lambda b, i: (b, i, 0)),
        mod_spec(2), mod_spec(3), mod_spec(4), mod_spec(5),
        pl.BlockSpec((1, d), lambda b, i: (0, 0)),
        pl.BlockSpec((1, d), lambda b, i: (0, 0)),
        resident((dm, d)), resident((d, D_FF), layer), resident((D_FF, d), layer),
    ]
    args = [x3, mix3, modl, modl, modl, modl, gain.reshape(1, d), fgain.reshape(1, d), wo, w1, w2]
    out_specs = [pl.BlockSpec((None, t, d), lambda b, i: (b, i, 0))]
    out_shape = [jax.ShapeDtypeStruct((bg, lg, d), f32)]
    scratch = []
    lru = None
    if lru_args is not None:
        gx, cw, cb, wg, lam, h0f, h0b, nseg, chained = lru_args
        ntok = gx.shape[0]
        assert (ntok // (LRU_SEG * nseg)) * LRU_BLOCKS == bg * steps, "one RG-LRU (tile, block) pair per MLP step"

        def pair(b, i):
            p = b * steps + i
            return p // LRU_BLOCKS, p % LRU_BLOCKS

        li, lo, ls, scratch = _lru_specs(ntok, nseg, h0f.shape[1], pair)
        in_specs += li
        out_specs += lo
        out_shape += ls
        args += [gx, gx, cw, cb, wg, lam, h0f, h0b]
        lru = (nseg, chained)
    res = pl.pallas_call(
        functools.partial(_outmlp_kernel, final=final, lru=lru),
        grid=(bg, steps),
        in_specs=in_specs,
        out_specs=out_specs,
        out_shape=out_shape,
        scratch_shapes=scratch,
        compiler_params=_cparams(("parallel", "parallel"), VMEM_FULL_MIB),
        name=("outmlp_final" if final else "outmlp") + ("_rglru" if lru else ""),
    )(*args)
    return res if lru else res[0]


@functools.lru_cache(maxsize=None)
def _rope_tables(length):
    pos = np.arange(length)
    rowp = (pos // GRID_W).astype(np.float64)
    colp = (pos % GRID_W).astype(np.float64)
    nf = RET_DK // 4
    inv_freq = ROPE_BASE ** (-np.arange(nf, dtype=np.float64) / nf)
    ar = rowp[:, None] * inv_freq[None, :]
    ac = colp[:, None] * inv_freq[None, :]
    cos = np.concatenate([np.cos(ar), np.cos(ar), np.cos(ac), np.cos(ac)], axis=-1).astype(np.float32)
    sin = np.concatenate([-np.sin(ar), np.sin(ar), -np.sin(ac), np.sin(ac)], axis=-1).astype(np.float32)
    return cos, sin


def kernel(x_prompt, x_sample, c, c_ctx, state_ret_fwd, state_ret_bwd, state_lru_fwd, state_lru_bwd, mod_w, mod_b, norm_mix, norm_mlp, mlp_w1, mlp_w2, ab_w_in, ab_w_out, ret_decay_fwd, ret_decay_bwd, gmlp_ws, gmlp_bs, lru_w_in, lru_conv_w, lru_conv_b, lru_gate_a_w, lru_gate_a_b, lru_gate_x_w, lru_gate_x_b, lru_lambda, lru_w_out, final_norm):
    d = D_MODEL
    n_lat = c.shape[0]
    conds = jnp.concatenate([c_ctx[None, :], c, jnp.zeros((COND_ROWS - 1 - n_lat, d), f32)], axis=0)
    mod = _mod_call(conds, mod_w, mod_b)

    w_in0 = ab_w_in[0].astype(bf16)
    w_out0 = ab_w_out[0].astype(bf16)
    w1 = mlp_w1.astype(bf16)
    w2 = mlp_w2.astype(bf16)
    w_in1 = lru_w_in[0].astype(bf16)
    w_out1 = lru_w_out[0].astype(bf16)
    ws = gmlp_ws[0].astype(bf16)
    bs = gmlp_bs[0][:, :, None]
    dec = jnp.broadcast_to(jnp.stack([ret_decay_fwd[0], ret_decay_bwd[0]])[:, :, None, None],
                           (2, RET_HEADS, 1, LANES))
    wg = jnp.concatenate([lru_gate_a_w[0, 0], lru_gate_x_w[0, 0], lru_gate_a_w[0, 1], lru_gate_x_w[0, 1]],
                         axis=-1).astype(bf16)
    bgate = 0.5 * jnp.stack([lru_gate_a_b[0, 0], lru_gate_x_b[0, 0], lru_gate_a_b[0, 1], lru_gate_x_b[0, 1]], axis=0)
    bgate = bgate.reshape(4, LRU_BLOCKS, LRU_BS).transpose(1, 0, 2).reshape(LRU_BLOCKS, 1, 4 * LRU_BS)
    terms, rem = [], bgate
    for _ in range(N_BIAS_TERMS):
        terms.append(rem.astype(bf16))
        rem = rem - terms[-1].astype(f32)
    bias_rows = jnp.pad(jnp.concatenate(terms, axis=1), ((0, 0), (0, LANES - N_BIAS_TERMS), (0, 0)))
    wg = jnp.concatenate([wg, bias_rows], axis=1)
    lru_w = (0.5 * lru_conv_w[0], 0.5 * lru_conv_b[0].reshape(1, D_RNN), wg, lru_lambda[0])

    def mixer0(x3, modl, bsz, ls, ret_f0, ret_b0, latent):
        tc = min(MIX_CHUNKS, ls // RET_CHUNK)
        if latent:
            qkv, sbn = _inproj_call(x3, modl, norm_mix[0], w_in0, _rope_tables(ls), retention=True,
                                    sh_col=0, sc_col=1, bwd=(dec, ret_b0))
            mix = _mix0_call(qkv, dec, ret_f0, sbn, ws, bs, tc)[0]
            sff = sbf = None
        else:
            qkv = _inproj_call(x3, modl, norm_mix[0], w_in0, None, retention=True, sh_col=0, sc_col=1)
            nsq = MIX_CHUNKS // tc
            mix, sff, sbf = _mix0_call(qkv.reshape(bsz // nsq, nsq * ls, AB_IN), dec, None, None, ws, bs,
                                       nsq * tc, nsq)
        return mix.reshape(x3.shape[0], x3.shape[1], AB_OUT), sff, sbf

    def lru_inproj(x3, modl):
        gx = _inproj_call(x3, modl, norm_mix[1], w_in1, None, retention=False, sh_col=0, sc_col=1)
        return gx.reshape(-1, 2 * D_RNN)

    bp, lp, _ = x_prompt.shape
    bl, ll, _ = x_sample.shape
    assert lp == LRU_SEG and ll % LRU_SEG == 0
    xc = x_prompt.reshape(1, bp * lp, d)
    xl = x_sample
    mod_c = [mod[l, 0:1][:, None, :] for l in range(DEPTH)]
    mod_l = [mod[l, 1:1 + n_lat][:, None, :] for l in range(DEPTH)]

    mix_c, sff, sbf = mixer0(xc, mod_c[0], bp, lp, None, None, False)
    xc = _outmlp_call(xc, mix_c, mod_c[0], norm_mlp[0], final_norm, w_out0, w1, w2, layer=0, final=False,
                      t=PLAIN_MLP_TILE)
    gx_c = lru_inproj(xc, mod_c[1])
    mix_l, _, _ = mixer0(xl, mod_l[0], bl, ll, state_ret_fwd[:, 0], state_ret_bwd[:, 0], True)

    steps_l = bl * (ll // TOK_TILE)
    nseg_c = (bp * lp // LRU_SEG) * LRU_BLOCKS // steps_l
    zero_lru = jnp.zeros((bp // nseg_c, nseg_c, D_RNN), f32)
    xl, pre_c, hfl, hbf = _outmlp_call(
        xl, mix_l, mod_l[0], norm_mlp[0], final_norm, w_out0, w1, w2, layer=0, final=False, t=TOK_TILE,
        lru_args=(gx_c,) + lru_w + (zero_lru, zero_lru, nseg_c, False))
    gx_l = lru_inproj(xl, mod_l[1])

    nseg_l = ll // LRU_SEG
    t_c = bp * lp // (bl * LRU_BLOCKS)
    y_c, pre_l, _, _ = _outmlp_call(
        xc, pre_c.reshape(1, bp * lp, D_RNN), mod_c[1], norm_mlp[1], final_norm, w_out1, w1, w2,
        layer=1, final=True, t=t_c,
        lru_args=(gx_l,) + lru_w + (state_lru_fwd[:, 0].reshape(bl, 1, D_RNN),
                                    state_lru_bwd[:, 0].reshape(bl, 1, D_RNN), nseg_l, True))
    y_l = _outmlp_call(xl, pre_l.reshape(bl, ll, D_RNN), mod_l[1], norm_mlp[1], final_norm, w_out1, w1, w2,
                       layer=1, final=True, t=PLAIN_MLP_TILE)
    return (y_c.reshape(bp, lp, d), y_l, sff[:, None], sbf[:, None],
            hfl.reshape(bp, 1, D_RNN), hbf.reshape(bp, 1, D_RNN))
```
